```python
import math
import jax
import jax.numpy as jnp
from jax import lax
import numpy as np

D_MODEL = 2048
BATCH = 4
SEQ = 2048
DEPTH = 2
DEC_BATCH = 32
DEC_SEQ = 1
PAST_LEN = 8192
PAGE_SIZE = 128

H_A = 8
DK_A = 64
DV_A = 2 * DK_A
W_A = H_A * DV_A
Q_BLOCK = 128
H_B = 8
DK_B = 128
DV_B = 128
W_BK = H_B * DK_B
W_B = H_B * DV_B
CHUNK = 64
N_BUCKETS = 32
MAX_EXACT = N_BUCKETS // 2
MAX_DISTANCE = 128
D_FF = 5632
N_EXPERTS = 8
TOP_K = 2
D_FF_EXPERT = 7168
N_DENSE = (DEPTH + 1) // 2
N_MOE = DEPTH // 2
ALPHA = (2 * DEPTH) ** 0.25
BETA = (8 * DEPTH) ** -0.25
LN_EPS = 1e-5
RMS_EPS = 1e-6
F_MIN = 1e-20
NEG_INF = -1e30
SPLITS = [H_A * 2 * DK_A, H_A * 2 * DK_A, W_A, W_BK, W_BK, W_B, W_B, D_MODEL, D_MODEL]
N_IN = sum(SPLITS)

kernel_name = "hybrid_diffattn_hgrn2_deepnorm_step"


def _split_offsets():
    offs, acc = [], 0
    for w in SPLITS[:-1]:
        acc += w
        offs.append(acc)
    return offs


def layer_norm(x, g, b):
    xf = x.astype(jnp.float32)
    mu = jnp.mean(xf, axis=-1, keepdims=True)
    var = jnp.mean(jnp.square(xf - mu), axis=-1, keepdims=True)
    y = (xf - mu) * lax.rsqrt(var + LN_EPS) * g.astype(jnp.float32) + b.astype(jnp.float32)
    return y.astype(x.dtype)


def rms_heads(o, g):
    of = o.astype(jnp.float32)
    of = of * lax.rsqrt(jnp.mean(jnp.square(of), axis=-1, keepdims=True) + RMS_EPS)
    return of * g.astype(jnp.float32).reshape(o.shape[-2:])


def t5_bucket(rel):
    n = jnp.maximum(rel, 0)
    large = MAX_EXACT + (jnp.log(jnp.maximum(n, 1).astype(jnp.float32) / MAX_EXACT)
                         / math.log(MAX_DISTANCE / MAX_EXACT) * (N_BUCKETS - MAX_EXACT)).astype(jnp.int32)
    large = jnp.clip(large, 0, N_BUCKETS - 1)
    return jnp.where(n < MAX_EXACT, n, large)


def diff_attn_core(q, k, v, qpos, kpos, rel_bias, lam):
    B, Lq = q.shape[:2]
    Lk = k.shape[1]
    rel = qpos[:, None] - kpos[None, :]
    bias = jnp.moveaxis(rel_bias[t5_bucket(rel)].astype(jnp.float32), -1, 0)
    qm = q.reshape(B, Lq, H_A, 2, DK_A)
    km = k.reshape(B, Lk, H_A, 2, DK_A)
    s = jnp.einsum("bqhmd,bkhmd->bmhqk", qm, km).astype(jnp.float32) * (DK_A ** -0.5) + bias
    s = jnp.where(rel >= 0, s, NEG_INF)
    p = jax.nn.softmax(s, axis=-1)
    w = p[:, 0] - lam * p[:, 1]
    return jnp.einsum("bhqk,bkhd->bqhd", w.astype(v.dtype), v)


def prompt_attention(q, k, v, rel_bias, lam):
    B, S = q.shape[:2]
    nb = S // Q_BLOCK
    qb = jnp.swapaxes(q.reshape(B, nb, Q_BLOCK, H_A, 2 * DK_A), 0, 1)
    pos_b = jnp.arange(S).reshape(nb, Q_BLOCK)
    kpos = jnp.arange(S)

    def blk(args):
        qi, pi = args
        return diff_attn_core(qi, k, v, pi, kpos, rel_bias, lam)

    o = lax.map(blk, (qb, pos_b))
    return jnp.swapaxes(o, 0, 1).reshape(B, S, H_A, DV_A)


def sample_attention(q, k, v, ck, cv, page_table, rel_bias, lam):
    DB, L = q.shape[:2]
    past = page_table.shape[1] * ck.shape[1]
    kp = ck[page_table].reshape(DB, past, H_A, 2 * DK_A)
    vp = cv[page_table].reshape(DB, past, H_A, DV_A)
    k_all = jnp.concatenate([kp, k], axis=1)
    v_all = jnp.concatenate([vp, v], axis=1)
    qpos = past + jnp.arange(L)
    kpos = jnp.arange(past + L)
    return diff_attn_core(q, k_all, v_all, qpos, kpos, rel_bias, lam)


def gla_chunk(S0, q, k, v, g):
    L = q.shape[1]
    b = jnp.cumsum(g, axis=1)
    causal = jnp.tril(jnp.ones((L, L), dtype=bool))[None, :, :, None, None]
    diff = b[:, :, None] - b[:, None, :]
    decay = jnp.where(causal, jnp.exp(jnp.where(causal, diff, 0.0)), 0.0)
    A = jnp.einsum("btshd,bshd->btsh", q[:, :, None] * decay, k)
    o = jnp.einsum("btsh,bshv->bthv", A, v) + jnp.einsum("bthd,bhdv->bthv", q * jnp.exp(b), S0)
    bL = b[:, -1]
    kd = k * jnp.exp(bL[:, None] - b)
    S = jnp.exp(bL)[..., None] * S0 + jnp.einsum("bshd,bshv->bhdv", kd, v)
    return S, o


def hgrn_scan(S0, q, k, v, g):
    B, S = q.shape[:2]
    nc = S // CHUNK

    def to_chunks(t):
        return jnp.swapaxes(t.reshape(B, nc, CHUNK, *t.shape[2:]), 0, 1)

    def step(Sc, inp):
        qc, kc, vc, gc = inp
        return gla_chunk(Sc, qc, kc, vc, gc)

    S_fin, o = lax.scan(step, S0, (to_chunks(q), to_chunks(k), to_chunks(v), to_chunks(g)))
    return S_fin, jnp.swapaxes(o, 0, 1).reshape(B, S, H_B, DV_B)


def token_mixer(h, w_in_l, w_pa_l, w_pb_l, w_out_l, lam, lam_init, subln_g_l, hgrn_g_l, lb_l, rel_bias, cache):
    B, L, _ = h.shape
    dt = h.dtype
    z = jnp.einsum("bld,dn->bln", h, w_in_l)
    q_a, k_a, v_a, f_b, q_b, i_b, og_b, gt_a, gt_b = jnp.split(z, _split_offsets(), axis=-1)
    q_a = q_a.reshape(B, L, H_A, 2 * DK_A)
    k_a = k_a.reshape(B, L, H_A, 2 * DK_A)
    v_a = v_a.reshape(B, L, H_A, DV_A)
    lb = lb_l.reshape(H_B, DK_B)
    fx = f_b.astype(jnp.float32).reshape(B, L, H_B, DK_B)
    f = lb + (1.0 - lb) * jax.nn.sigmoid(fx)
    log_f = jnp.log(jnp.maximum(f, F_MIN))
    k_b = (1.0 - lb) * jax.nn.sigmoid(-fx)
    q_b = jax.nn.silu(q_b.astype(jnp.float32)).reshape(B, L, H_B, DK_B)
    v_b = i_b.astype(jnp.float32).reshape(B, L, H_B, DV_B)
    if cache is None:
        o_a = prompt_attention(q_a, k_a, v_a, rel_bias, lam)
        S0 = jnp.zeros((B, H_B, DK_B, DV_B), jnp.float32)
        S, o_b = hgrn_scan(S0, q_b, k_b, v_b, log_f)
    else:
        ck, cv, page_table, s_in = cache
        o_a = sample_attention(q_a, k_a, v_a, ck, cv, page_table, rel_bias, lam)
        S, o_b = gla_chunk(s_in.astype(jnp.float32), q_b, k_b, v_b, log_f)
    y_a = (rms_heads(o_a, subln_g_l) * (1.0 - lam_init)).reshape(B, L, W_A).astype(dt)
    y_b = (rms_heads(o_b, hgrn_g_l) * jax.nn.sigmoid(og_b.astype(jnp.float32)).reshape(B, L, H_B, DV_B)
           ).reshape(B, L, W_B).astype(dt)
    m = jax.nn.sigmoid(gt_a) * (y_a @ w_pa_l) + jax.nn.sigmoid(gt_b) * (y_b @ w_pb_l)
    return m @ w_out_l, k_a, v_a, S


def swiglu(h, w1, w3, w2):
    return (jax.nn.silu(h @ w1) * (h @ w3)) @ w2


def moe_ffn(h, router, w1, w3, w2):
    logits = jnp.einsum("bld,de->ble", h, router).astype(jnp.float32)
    top_v, top_i = lax.top_k(logits, TOP_K)
    gate = jax.nn.softmax(top_v, axis=-1)
    comb = jnp.sum(jax.nn.one_hot(top_i, N_EXPERTS, dtype=jnp.float32) * gate[..., None], axis=-2)
    y = jnp.zeros_like(h)
    for e in range(N_EXPERTS):
        y = y + comb[..., e:e + 1].astype(h.dtype) * swiglu(h, w1[e], w3[e], w2[e])
    return y


def setup_inputs(seed: int = 0) -> dict:
    key = jax.random.key(seed)
    ks = jax.random.split(key, 24)
    f32 = jnp.float32
    n_pages = PAST_LEN // PAGE_SIZE
    n_used = DEC_BATCH * n_pages
    n_pool = n_used + max(n_used // 4, 1)

    def nrm(k, shape, scale):
        return jax.random.normal(k, shape, f32) * scale

    page_table = jax.random.permutation(ks[4], n_pool)[:n_used].reshape(DEC_BATCH, n_pages).astype(jnp.int32)
    return {
        "x_prompt": nrm(ks[0], (BATCH, SEQ, D_MODEL), 1.0),
        "x_sample": nrm(ks[1], (DEC_BATCH, DEC_SEQ, D_MODEL), 1.0),
        "cache_k": nrm(ks[2], (DEPTH, n_pool, PAGE_SIZE, H_A, 2 * DK_A), 1.0),
        "cache_v": nrm(ks[3], (DEPTH, n_pool, PAGE_SIZE, H_A, DV_A), 1.0),
        "page_table": page_table,
        "state_hgrn": nrm(ks[5], (DEPTH, DEC_BATCH, H_B, DK_B, DV_B), 0.5),
        "rel_bias": nrm(ks[6], (N_BUCKETS, H_A), 0.5),
        "w_in": nrm(ks[7], (DEPTH, D_MODEL, N_IN), D_MODEL ** -0.5),
        "w_pa": nrm(ks[8], (DEPTH, W_A, D_MODEL), W_A ** -0.5),
        "w_pb": nrm(ks[9], (DEPTH, W_B, D_MODEL), W_B ** -0.5),
        "w_out": nrm(ks[10], (DEPTH, D_MODEL, D_MODEL), BETA * D_MODEL ** -0.5),
        "lam_qk": nrm(ks[11], (DEPTH, 4, DK_A), 0.1),
        "subln_g": 1.0 + nrm(ks[12], (DEPTH, W_A), 0.02),
        "hgrn_g": 1.0 + nrm(ks[13], (DEPTH, W_B), 0.02),
        "lb_raw": nrm(ks[14], (DEPTH, W_BK), 1.0),
        "ln_g": 1.0 + nrm(ks[15], (DEPTH, 2, D_MODEL), 0.02),
        "ln_b": nrm(ks[16], (DEPTH, 2, D_MODEL), 0.02),
        "ffn_w1": nrm(ks[17], (N_DENSE, D_MODEL, D_FF), D_MODEL ** -0.5),
        "ffn_w3": nrm(ks[18], (N_DENSE, D_MODEL, D_FF), D_MODEL ** -0.5),
        "ffn_w2": nrm(ks[19], (N_DENSE, D_FF, D_MODEL), BETA * D_FF ** -0.5),
        "router_w": nrm(ks[20], (N_MOE, D_MODEL, N_EXPERTS), D_MODEL ** -0.5),
        "moe_w1": nrm(ks[21], (N_MOE, N_EXPERTS, D_MODEL, D_FF_EXPERT), D_MODEL ** -0.5),
        "moe_w3": nrm(ks[22], (N_MOE, N_EXPERTS, D_MODEL, D_FF_EXPERT), D_MODEL ** -0.5),
        "moe_w2": nrm(ks[23], (N_MOE, N_EXPERTS, D_FF_EXPERT, D_MODEL), BETA * D_FF_EXPERT ** -0.5),
    }


def reference(x_prompt, x_sample, cache_k, cache_v, page_table, state_hgrn, rel_bias, w_in, w_pa, w_pb, w_out,
              lam_qk, subln_g, hgrn_g, lb_raw, ln_g, ln_b, ffn_w1, ffn_w3, ffn_w2, router_w, moe_w1, moe_w3, moe_w2):
    lbs = jax.nn.softmax(lb_raw.astype(jnp.float32), axis=0)
    lbs = jnp.cumsum(lbs, axis=0) - lbs[0]
    xp, xs = x_prompt, x_sample
    kp_list, vp_list, sp_list, ks_list, vs_list, ss_list = [], [], [], [], [], []
    for l in range(DEPTH):
        lam_init = 0.8 - 0.6 * math.exp(-0.3 * l)
        lq = lam_qk[l].astype(jnp.float32)
        lam = jnp.exp(jnp.sum(lq[0] * lq[1])) - jnp.exp(jnp.sum(lq[2] * lq[3])) + lam_init
        mix = (w_in[l], w_pa[l], w_pb[l], w_out[l], lam, lam_init, subln_g[l], hgrn_g[l], lbs[l], rel_bias)
        mo_p, k_p, v_p, s_p = token_mixer(xp, *mix, None)
        mo_s, k_s, v_s, s_s = token_mixer(xs, *mix, (cache_k[l], cache_v[l], page_table, state_hgrn[l]))
        xp = layer_norm(ALPHA * xp + mo_p, ln_g[l, 0], ln_b[l, 0])
        xs = layer_norm(ALPHA * xs + mo_s, ln_g[l, 0], ln_b[l, 0])
        j = l // 2
        if l % 2 == 0:
            cp = swiglu(xp, ffn_w1[j], ffn_w3[j], ffn_w2[j])
            cs = swiglu(xs, ffn_w1[j], ffn_w3[j], ffn_w2[j])
        else:
            cp = moe_ffn(xp, router_w[j], moe_w1[j], moe_w3[j], moe_w2[j])
            cs = moe_ffn(xs, router_w[j], moe_w1[j], moe_w3[j], moe_w2[j])
        xp = layer_norm(ALPHA * xp + cp, ln_g[l, 1], ln_b[l, 1])
        xs = layer_norm(ALPHA * xs + cs, ln_g[l, 1], ln_b[l, 1])
        kp_list.append(k_p)
        vp_list.append(v_p)
        sp_list.append(s_p.astype(x_prompt.dtype))
        ks_list.append(k_s)
        vs_list.append(v_s)
        ss_list.append(s_s.astype(state_hgrn.dtype))
    return (xp, xs, jnp.stack(kp_list), jnp.stack(vp_list), jnp.stack(sp_list),
            jnp.stack(ks_list), jnp.stack(vs_list), jnp.stack(ss_list))
```

```python
import functools
import math

import numpy as np
import jax
import jax.numpy as jnp
from jax import lax
from jax.experimental import pallas as pl
from jax.experimental.pallas import tpu as pltpu

F32 = jnp.float32
BF16 = jnp.bfloat16

D_MODEL = 2048
DEPTH = 2
PAGE_SIZE = 128
H_A = 8
DK_A = 64
DV_A = 2 * DK_A
W_A = H_A * DV_A
H_B = 8
DK_B = 128
DV_B = 128
W_BK = H_B * DK_B
W_B = H_B * DV_B
CHUNK = 64
N_BUCKETS = 32
MAX_EXACT = N_BUCKETS // 2
MAX_DISTANCE = 128
N_EXPERTS = 8
TOP_K = 2
ALPHA = (2 * DEPTH) ** 0.25
LN_EPS = 1e-5
RMS_EPS = 1e-6
F_MIN = 1e-20
NEG_INF = -1e30
SPLITS = [H_A * 2 * DK_A, H_A * 2 * DK_A, W_A, W_BK, W_BK, W_B, W_B, D_MODEL, D_MODEL]
N_IN = sum(SPLITS)
COL = {}
_acc = 0
for _name, _w in zip(["q_a", "k_a", "v_a", "f_b", "q_b", "i_b", "og_b", "gt_a", "gt_b"], SPLITS):
    COL[_name] = _acc
    _acc += _w

LANES = 128
V7X_VMEM_BYTES = 64 * 1024 * 1024
VMEM_LIMIT = 56 * 1024 * 1024

ATT_T = 256
MOE_TM = 512
PAGES_PER_STEP = 8


def _cparams(n_axes):
    return pltpu.CompilerParams(
        dimension_semantics=("arbitrary",) * n_axes, vmem_limit_bytes=VMEM_LIMIT)


def _sigmoid(x):
    return jax.nn.sigmoid(x)


def _mm_kernel(x_ref, w_ref, o_ref, wb_ref):
    @pl.when(pl.program_id(1) == 0)
    def _():
        wb_ref[...] = w_ref[...].astype(BF16)

    o_ref[...] = jnp.dot(x_ref[...], wb_ref[...], preferred_element_type=F32).astype(o_ref.dtype)


def matmul(x, w, layer, *, tm, tn, out_dtype=F32):
    M, K = x.shape
    N = w.shape[2]
    tm = min(tm, M)
    assert M % tm == 0 and N % tn == 0
    return pl.pallas_call(
        _mm_kernel,
        grid=(N // tn, M // tm),
        in_specs=[
            pl.BlockSpec((tm, K), lambda n, m: (m, 0)),
            pl.BlockSpec((None, K, tn), lambda n, m: (layer, 0, n)),
        ],
        out_specs=pl.BlockSpec((tm, tn), lambda n, m: (m, n)),
        out_shape=jax.ShapeDtypeStruct((M, N), out_dtype),
        scratch_shapes=[pltpu.VMEM((K, tn), BF16)],
        compiler_params=_cparams(2),
        name="matmul",
    )(x, w)


def _swiglu_kernel(x_ref, w1_ref, w3_ref, o_ref, w1b_ref, w3b_ref):
    @pl.when(pl.program_id(1) == 0)
    def _():
        w1b_ref[...] = w1_ref[...].astype(BF16)
        w3b_ref[...] = w3_ref[...].astype(BF16)

    x = x_ref[...]
    a = jnp.dot(x, w1b_ref[...], preferred_element_type=F32)
    b = jnp.dot(x, w3b_ref[...], preferred_element_type=F32)
    o_ref[...] = (a * _sigmoid(a) * b).astype(o_ref.dtype)


def swiglu_up(x, w1, w3, layer, *, tm, tn):
    M, K = x.shape
    N = w1.shape[2]
    tm = min(tm, M)
    assert M % tm == 0 and N % tn == 0
    return pl.pallas_call(
        _swiglu_kernel,
        grid=(N // tn, M // tm),
        in_specs=[
            pl.BlockSpec((tm, K), lambda n, m: (m, 0)),
            pl.BlockSpec((None, K, tn), lambda n, m: (layer, 0, n)),
            pl.BlockSpec((None, K, tn), lambda n, m: (layer, 0, n)),
        ],
        out_specs=pl.BlockSpec((tm, tn), lambda n, m: (m, n)),
        out_shape=jax.ShapeDtypeStruct((M, N), BF16),
        scratch_shapes=[pltpu.VMEM((K, tn), BF16), pltpu.VMEM((K, tn), BF16)],
        compiler_params=_cparams(2),
        name="swiglu_up",
    )(x, w1, w3)


def _merge_kernel(ya_ref, yb_ref, wa_ref, wb_ref, ga_ref, gb_ref, o_ref, wab_ref, wbb_ref):
    @pl.when(pl.program_id(1) == 0)
    def _():
        wab_ref[...] = wa_ref[...].astype(BF16)
        wbb_ref[...] = wb_ref[...].astype(BF16)

    pa = jnp.dot(ya_ref[...], wab_ref[...], preferred_element_type=F32)
    pb = jnp.dot(yb_ref[...], wbb_ref[...], preferred_element_type=F32)
    o_ref[...] = (_sigmoid(ga_ref[...]) * pa + _sigmoid(gb_ref[...]) * pb).astype(o_ref.dtype)


def branch_merge(ya, yb, z, w_pa, w_pb, layer, *, tm, tn):
    M, K = ya.shape
    N = w_pa.shape[2]
    tm = min(tm, M)
    assert M % tm == 0 and N % tn == 0
    ga0, gb0 = COL["gt_a"] // tn, COL["gt_b"] // tn
    return pl.pallas_call(
        _merge_kernel,
        grid=(N // tn, M // tm),
        in_specs=[
            pl.BlockSpec((tm, K), lambda n, m: (m, 0)),
            pl.BlockSpec((tm, K), lambda n, m: (m, 0)),
            pl.BlockSpec((None, K, tn), lambda n, m: (layer, 0, n)),
            pl.BlockSpec((None, K, tn), lambda n, m: (layer, 0, n)),
            pl.BlockSpec((tm, tn), lambda n, m: (m, ga0 + n)),
            pl.BlockSpec((tm, tn), lambda n, m: (m, gb0 + n)),
        ],
        out_specs=pl.BlockSpec((tm, tn), lambda n, m: (m, n)),
        out_shape=jax.ShapeDtypeStruct((M, N), BF16),
        scratch_shapes=[pltpu.VMEM((K, tn), BF16), pltpu.VMEM((K, tn), BF16)],
        compiler_params=_cparams(2),
        name="branch_merge",
    )(ya, yb, w_pa, w_pb, z, z)


def _add_ln(x, y, g, b):
    h = ALPHA * x + y
    mu = jnp.mean(h, axis=-1, keepdims=True)
    c = h - mu
    var = jnp.mean(c * c, axis=-1, keepdims=True)
    return c * lax.rsqrt(var + LN_EPS) * g + b


def _res_ln_kernel(x_ref, y_ref, g_ref, b_ref, o_ref, ob_ref):
    o = _add_ln(x_ref[...], y_ref[...], g_ref[...], b_ref[...])
    o_ref[...] = o
    ob_ref[...] = o.astype(BF16)


def residual_ln(x, y, g, b, *, tm=256):
    M, D = x.shape
    tm = min(tm, M)
    assert M % tm == 0
    return pl.pallas_call(
        _res_ln_kernel,
        grid=(M // tm,),
        in_specs=[
            pl.BlockSpec((tm, D), lambda m: (m, 0)),
            pl.BlockSpec((tm, D), lambda m: (m, 0)),
            pl.BlockSpec((1, D), lambda m: (0, 0)),
            pl.BlockSpec((1, D), lambda m: (0, 0)),
        ],
        out_specs=[pl.BlockSpec((tm, D), lambda m: (m, 0)), pl.BlockSpec((tm, D), lambda m: (m, 0))],
        out_shape=[jax.ShapeDtypeStruct((M, D), F32), jax.ShapeDtypeStruct((M, D), BF16)],
        compiler_params=_cparams(1),
        name="residual_ln",
    )(x, y, g.reshape(1, D), b.reshape(1, D))


def _t5_bucket(rel):
    n = jnp.maximum(rel, 0)
    large = MAX_EXACT + (jnp.log(jnp.maximum(n, 1).astype(F32) / MAX_EXACT)
                         / math.log(MAX_DISTANCE / MAX_EXACT) * (N_BUCKETS - MAX_EXACT)).astype(jnp.int32)
    large = jnp.clip(large, 0, N_BUCKETS - 1)
    return jnp.where(n < MAX_EXACT, n, large)


def _bucket_np(n):
    n = np.maximum(n, 0)
    large = MAX_EXACT + (np.log(np.maximum(n, 1) / MAX_EXACT)
                         / math.log(MAX_DISTANCE / MAX_EXACT) * (N_BUCKETS - MAX_EXACT)).astype(np.int64)
    return np.where(n < MAX_EXACT, n, np.clip(large, 0, N_BUCKETS - 1))


assert (_bucket_np(np.arange(min(ATT_T, PAGE_SIZE) + 1, 1 << 16)) == N_BUCKETS - 1).all()


def _bias_tiles(rel_bias, T):
    r = jnp.arange(T)[:, None] - jnp.arange(T)[None, :]
    tiles = []
    for d in range(3):
        rel = d * T + r
        bias = rel_bias[_t5_bucket(rel)].astype(F32)
        bias = jnp.where((rel >= 0)[..., None], bias, NEG_INF)
        tiles.append(jnp.moveaxis(bias, -1, 0))
    return jnp.stack(tiles, axis=1)


def _dot_nt(a, b):
    return lax.dot_general(a, b, (((1,), (1,)), ((), ())), preferred_element_type=F32)


def _dot_tn(a, b):
    return lax.dot_general(a, b, (((0,), (0,)), ((), ())), preferred_element_type=F32)


def _rms_head(o, g):
    return o * lax.rsqrt(jnp.mean(o * o, axis=-1, keepdims=True) + RMS_EPS) * g


def _prompt_attn_kernel(lam_ref, q_ref, k_ref, v_ref, bias_ref, g_ref, o_ref, *, T, out_scale):
    qi = pl.program_id(2)
    lane = lax.broadcasted_iota(jnp.int32, (1, 2 * DK_A), 1)
    q = q_ref[...] * (DK_A ** -0.5)
    q1 = jnp.where(lane < DK_A, q, 0.0).astype(BF16)
    q2 = jnp.where(lane >= DK_A, q, 0.0).astype(BF16)

    def step(kj, carry):
        m1, l1, a1, m2, l2, a2 = carry
        rows = pl.ds(pl.multiple_of(kj * T, T), T)
        k = k_ref[rows, :].astype(BF16)
        v = v_ref[rows, :].astype(BF16)
        bias = bias_ref[jnp.minimum(qi - kj, 2)]

        def one(qm, m, l, a):
            s = _dot_nt(qm, k) + bias
            m_new = jnp.maximum(m, jnp.max(s, axis=-1, keepdims=True))
            p = jnp.exp(s - m_new)
            r = jnp.exp(m - m_new)
            l_new = r * l + jnp.sum(p, axis=-1, keepdims=True)
            a_new = r * a + jnp.dot(p.astype(BF16), v, preferred_element_type=F32)
            return m_new, l_new, a_new

        m1, l1, a1 = one(q1, m1, l1, a1)
        m2, l2, a2 = one(q2, m2, l2, a2)
        return m1, l1, a1, m2, l2, a2

    m0 = jnp.full((T, 1), NEG_INF, F32)
    l0 = jnp.zeros((T, 1), F32)
    a0 = jnp.zeros((T, DV_A), F32)
    m1, l1, a1, m2, l2, a2 = lax.fori_loop(0, qi + 1, step, (m0, l0, a0, m0, l0, a0))
    lam = lam_ref[0]
    o = a1 / l1 - lam * (a2 / l2)
    o_ref[...] = (_rms_head(o, g_ref[...]) * out_scale).astype(o_ref.dtype)


def prompt_attention(z, bias_tiles, lam, subln_g_l, lam_init, B, S):
    T = ATT_T
    nq = S // T
    qc, kc, vc = COL["q_a"] // DV_A, COL["k_a"] // DV_A, COL["v_a"] // DV_A
    kern = functools.partial(_prompt_attn_kernel, T=T, out_scale=1.0 - lam_init)
    return pl.pallas_call(
        kern,
        grid=(B, H_A, nq),
        in_specs=[
            pl.BlockSpec(memory_space=pltpu.SMEM),
            pl.BlockSpec((T, DV_A), lambda b, h, i: (b * nq + i, qc + h)),
            pl.BlockSpec((S, DV_A), lambda b, h, i: (b, kc + h)),
            pl.BlockSpec((S, DV_A), lambda b, h, i: (b, vc + h)),
            pl.BlockSpec((None, 3, T, T), lambda b, h, i: (h, 0, 0, 0)),
            pl.BlockSpec((None, 1, DV_A), lambda b, h, i: (h, 0, 0)),
        ],
        out_specs=pl.BlockSpec((T, DV_A), lambda b, h, i: (b * nq + i, h)),
        out_shape=jax.ShapeDtypeStruct((B * S, W_A), BF16),
        compiler_params=_cparams(3),
        name="prompt_attention",
    )(lam.reshape(1), z, z, z, bias_tiles, subln_g_l.reshape(H_A, 1, DV_A))


_LEVELS = [CHUNK >> (i + 1) for i in range(int(math.log2(CHUNK)))]


def _cumsum_and_anchor_matrix():
    t = np.arange(CHUNK)
    tri = (t[None, :] <= t[:, None]).astype(np.float32)
    mats = [tri]
    for w in _LEVELS:
        anchor_row = t - (t % (2 * w)) + w - 1
        mats.append(tri[anchor_row])
    return np.concatenate(mats, axis=0)


def _split3(x):
    x1 = x.astype(BF16)
    r1 = x - x1.astype(F32)
    x2 = r1.astype(BF16)
    x3 = (r1 - x2.astype(F32)).astype(BF16)
    return x1, x2, x3


def _hgrn_gates(fx, qx, lb):
    f = lb + (1.0 - lb) * _sigmoid(fx)
    g = jnp.log(jnp.maximum(f, F_MIN))
    k = (1.0 - lb) * _sigmoid(-fx)
    q = qx * _sigmoid(qx)
    return g, k, q


def _hgrn_prompt_kernel(f_ref, q_ref, i_ref, og_ref, lb_ref, gain_ref, pm_ref, y_ref, s_ref, *, S):
    lb = lb_ref[...]
    gain = gain_ref[...]
    pm = pm_ref[...]
    t_io = lax.broadcasted_iota(jnp.int32, (CHUNK, CHUNK), 0)
    s_io = lax.broadcasted_iota(jnp.int32, (CHUNK, CHUNK), 1)
    masks = [(((t_io ^ s_io) < 2 * w) & ((t_io & w) != 0)) & ((s_io & w) == 0) for w in _LEVELS]
    diag = t_io == s_io

    def chunk(c, st):
        rows = pl.ds(pl.multiple_of(c * CHUNK, CHUNK), CHUNK)
        g, k, q = _hgrn_gates(f_ref[rows, :], q_ref[rows, :], lb)
        v = i_ref[rows, :]
        g1, g2, g3 = _split3(g)
        ball = (jnp.dot(pm, g1, preferred_element_type=F32)
                + jnp.dot(pm, g2, preferred_element_type=F32)
                + jnp.dot(pm, g3, preferred_element_type=F32))
        b = ball[0:CHUNK]
        vb = v.astype(BF16)
        a = jnp.where(diag, _dot_nt(q.astype(BF16), k.astype(BF16)), 0.0)
        for li in range(len(_LEVELS)):
            anc = ball[(li + 1) * CHUNK:(li + 2) * CHUNK]
            qs = (q * jnp.exp(jnp.minimum(b - anc, 0.0))).astype(BF16)
            ks = (k * jnp.exp(jnp.minimum(anc - b, 0.0))).astype(BF16)
            a = jnp.where(masks[li], _dot_nt(qs, ks), a)
        b_last = b[CHUNK - 1:CHUNK, :]
        qe = (q * jnp.exp(b)).astype(BF16)
        o = (jnp.dot(a.astype(BF16), vb, preferred_element_type=F32)
             + _dot_nt(qe, st.astype(BF16)))
        kd = (k * jnp.exp(b_last - b)).astype(BF16)
        st_new = st * jnp.exp(b_last) + _dot_tn(vb, kd)
        y = _rms_head(o, gain) * _sigmoid(og_ref[rows, :])
        y_ref[rows, :] = y.astype(y_ref.dtype)
        return st_new

    st = lax.fori_loop(0, S // CHUNK, chunk, jnp.zeros((DV_B, DK_B), F32))
    s_ref[...] = st.T


def hgrn_prompt(z, lb_l, hgrn_g_l, B, S):
    fc, qc, ic, oc = (COL[n] // DK_B for n in ("f_b", "q_b", "i_b", "og_b"))
    pm = jnp.asarray(_cumsum_and_anchor_matrix(), BF16)
    kern = functools.partial(_hgrn_prompt_kernel, S=S)
    blk = lambda c0: pl.BlockSpec((S, DK_B), lambda b, h: (b, c0 + h))
    return pl.pallas_call(
        kern,
        grid=(B, H_B),
        in_specs=[
            blk(fc), blk(qc), blk(ic), blk(oc),
            pl.BlockSpec((None, 1, DK_B), lambda b, h: (h, 0, 0)),
            pl.BlockSpec((None, 1, DV_B), lambda b, h: (h, 0, 0)),
            pl.BlockSpec(pm.shape, lambda b, h: (0, 0)),
        ],
        out_specs=[
            pl.BlockSpec((S, DV_B), lambda b, h: (b, h)),
            pl.BlockSpec((None, None, DK_B, DV_B), lambda b, h: (b, h, 0, 0)),
        ],
        out_shape=[
            jax.ShapeDtypeStruct((B * S, W_B), BF16),
            jax.ShapeDtypeStruct((B, H_B, DK_B, DV_B), F32),
        ],
        compiler_params=_cparams(2),
        name="hgrn_prompt",
    )(z, z, z, z, lb_l.reshape(H_B, 1, DK_B), hgrn_g_l.reshape(H_B, 1, DV_B), pm)


def _paged_attn_kernel(pt_ref, lam_ref, q_ref, kn_ref, vn_ref, bias_ref, g_ref, *rest,
                       n_groups, out_scale):
    G = PAGES_PER_STEP
    k_refs = rest[:G]
    v_refs = rest[G:2 * G]
    o_ref = rest[2 * G]
    m_ref, l_ref, a_ref = rest[2 * G + 1:]
    j = pl.program_id(1)
    lane = lax.broadcasted_iota(jnp.int32, (1, 2 * DK_A), 1)
    q = q_ref[...] * (DK_A ** -0.5)
    qs = (jnp.where(lane < DK_A, q, 0.0), jnp.where(lane >= DK_A, q, 0.0))

    @pl.when(j == 0)
    def _():
        m_ref[...] = jnp.full(m_ref.shape, NEG_INF, F32)
        l_ref[...] = jnp.zeros(l_ref.shape, F32)
        a_ref[...] = jnp.zeros(a_ref.shape, F32)

    def absorb(k, v, bias):
        for mi in range(2):
            s = jnp.sum(k * qs[mi], axis=-1, keepdims=True) + bias
            m_old = m_ref[mi]
            m_new = jnp.maximum(m_old, jnp.max(s, axis=0))
            p = jnp.exp(s - m_new)
            r = jnp.exp(m_old - m_new)
            l_ref[mi] = r * l_ref[mi] + jnp.sum(p, axis=0)
            a_ref[mi] = r * a_ref[mi] + jnp.sum(p * v, axis=0)
            m_ref[mi] = m_new

    far = bias_ref[PAGE_SIZE]

    @pl.when(j < n_groups - 1)
    def _():
        for i in range(G):
            absorb(k_refs[i][...], v_refs[i][...], far)

    @pl.when(j == n_groups - 1)
    def _():
        for i in range(G - 1):
            absorb(k_refs[i][...], v_refs[i][...], far)
        absorb(k_refs[G - 1][...], v_refs[G - 1][...], bias_ref[0:PAGE_SIZE])
        absorb(kn_ref[...][None], vn_ref[...][None], bias_ref[PAGE_SIZE + 1])
        lam = lam_ref[0]
        o = a_ref[0] / l_ref[0] - lam * (a_ref[1] / l_ref[1])
        o_ref[...] = (_rms_head(o, g_ref[...]) * out_scale).astype(o_ref.dtype)


def sample_attention(q, k_new, v_new, cache_k, cache_v, page_table, rel_bias, lam, subln_g_l,
                     lam_init, layer):
    DB = q.shape[0]
    n_pages = page_table.shape[1]
    G = PAGES_PER_STEP
    assert n_pages % G == 0
    n_groups = n_pages // G
    dist = jnp.concatenate([PAGE_SIZE - jnp.arange(PAGE_SIZE), jnp.array([2 * PAGE_SIZE, 0])])
    bias = rel_bias[_t5_bucket(dist)].astype(F32)[..., None]
    kern = functools.partial(_paged_attn_kernel, n_groups=n_groups, out_scale=1.0 - lam_init)

    def page_spec(i):
        return pl.BlockSpec((None, None, PAGE_SIZE, H_A, DV_A),
                            lambda b, j, pt: (layer, pt[b * n_pages + j * G + i], 0, 0, 0))

    tok = pl.BlockSpec((None, H_A, DV_A), lambda b, j, pt: (b, 0, 0))
    grid_spec = pltpu.PrefetchScalarGridSpec(
        num_scalar_prefetch=1,
        grid=(DB, n_groups),
        in_specs=[
            pl.BlockSpec(memory_space=pltpu.SMEM),
            tok, tok, tok,
            pl.BlockSpec(bias.shape, lambda b, j, pt: (0, 0, 0)),
            pl.BlockSpec((H_A, DV_A), lambda b, j, pt: (0, 0)),
        ] + [page_spec(i) for i in range(G)] + [page_spec(i) for i in range(G)],
        out_specs=tok,
        scratch_shapes=[pltpu.VMEM((2, H_A, 1), F32), pltpu.VMEM((2, H_A, 1), F32),
                        pltpu.VMEM((2, H_A, DV_A), F32)],
    )
    return pl.pallas_call(
        kern,
        grid_spec=grid_spec,
        out_shape=jax.ShapeDtypeStruct((DB, H_A, DV_A), BF16),
        compiler_params=_cparams(2),
        name="sample_attention",
    )(page_table.reshape(-1), lam.reshape(1), q, k_new, v_new, bias, subln_g_l.reshape(H_A, DV_A),
      *([cache_k] * G), *([cache_v] * G))


def _hgrn_step_kernel(fx_ref, qx_ref, v_ref, og_ref, lb_ref, gain_ref, s_ref, y_ref, so_ref):
    g, k, q = _hgrn_gates(fx_ref[...], qx_ref[...], lb_ref[...])
    v = v_ref[...]
    s_new = jnp.exp(g) * s_ref[...].astype(F32) + k * v
    so_ref[...] = s_new.astype(so_ref.dtype)
    o = jnp.sum(q * s_new, axis=1, keepdims=True)
    y = _rms_head(o, gain_ref[...]) * _sigmoid(og_ref[...])
    y_ref[...] = y.astype(y_ref.dtype)


def hgrn_step(fx, qx, v, og, lb_l, hgrn_g_l, state):
    DB = fx.shape[0]
    col = pl.BlockSpec((None, H_B, DK_B, 1), lambda b: (b, 0, 0, 0))
    row = pl.BlockSpec((None, H_B, 1, DV_B), lambda b: (b, 0, 0, 0))
    st = pl.BlockSpec((None, H_B, DK_B, DV_B), lambda b: (b, 0, 0, 0))
    return pl.pallas_call(
        _hgrn_step_kernel,
        grid=(DB,),
        in_specs=[col, col, row, row,
                  pl.BlockSpec((H_B, DK_B, 1), lambda b: (0, 0, 0)),
                  pl.BlockSpec((H_B, 1, DV_B), lambda b: (0, 0, 0)),
                  st],
        out_specs=[row, st],
        out_shape=[jax.ShapeDtypeStruct((DB, H_B, 1, DV_B), BF16),
                   jax.ShapeDtypeStruct(state.shape, state.dtype)],
        compiler_params=_cparams(1),
        name="hgrn_step",
    )(fx, qx, v, og, lb_l.reshape(H_B, DK_B, 1), hgrn_g_l.reshape(H_B, 1, DV_B), state)


def _router_kernel(x_ref, w_ref, idx_ref, gate_ref):
    x1, x2, _ = _split3(x_ref[...])
    w1, w2, _ = _split3(w_ref[...])
    lt = _dot_nt(w1, x1) + (_dot_nt(w1, x2) + _dot_nt(w2, x1))
    e_io = lax.broadcasted_iota(jnp.int32, lt.shape, 0)
    m1 = jnp.max(lt, axis=0, keepdims=True)
    i1 = jnp.min(jnp.where(lt == m1, e_io, N_EXPERTS), axis=0, keepdims=True)
    lt2 = jnp.where(e_io == i1, -jnp.inf, lt)
    m2 = jnp.max(lt2, axis=0, keepdims=True)
    i2 = jnp.min(jnp.where(lt2 == m2, e_io, N_EXPERTS), axis=0, keepdims=True)
    e2 = jnp.exp(m2 - m1)
    den = 1.0 + e2
    idx_ref[0:1, :] = i1
    idx_ref[1:2, :] = i2
    gate_ref[0:1, :] = 1.0 / den
    gate_ref[1:2, :] = e2 / den


def router(x, router_w_l, *, tm=512):
    M, D = x.shape
    Mp = pl.cdiv(M, tm) * tm
    return pl.pallas_call(
        _router_kernel,
        grid=(Mp // tm,),
        in_specs=[pl.BlockSpec((tm, D), lambda m: (m, 0)),
                  pl.BlockSpec((N_EXPERTS, D), lambda m: (0, 0))],
        out_specs=[pl.BlockSpec((TOP_K, tm), lambda m: (0, m)), pl.BlockSpec((TOP_K, tm), lambda m: (0, m))],
        out_shape=[jax.ShapeDtypeStruct((TOP_K, M), jnp.int32), jax.ShapeDtypeStruct((TOP_K, M), F32)],
        compiler_params=_cparams(1),
        name="router",
    )(x, router_w_l.T)


def _gather_rows_kernel(src_ref, x_hbm, o_ref, sem, *, rows):
    base = pl.program_id(0) * rows

    def row_copy(r, src_row):
        return pltpu.make_async_copy(x_hbm.at[src_row], o_ref.at[r], sem)

    def issue(r, c):
        row_copy(r, src_ref[base + r]).start()
        return c

    lax.fori_loop(0, rows, issue, 0)

    def drain(r, c):
        row_copy(r, 0).wait()
        return c

    lax.fori_loop(0, rows, drain, 0)


def gather_rows(x, src, *, rows=256):
    M, D = x.shape
    P = src.shape[0]
    assert P % rows == 0 and D % LANES == 0
    x3 = x.reshape(M, D // LANES, LANES)
    out = pl.pallas_call(
        functools.partial(_gather_rows_kernel, rows=rows),
        grid_spec=pltpu.PrefetchScalarGridSpec(
            num_scalar_prefetch=1,
            grid=(P // rows,),
            in_specs=[pl.BlockSpec(memory_space=pl.ANY)],
            out_specs=pl.BlockSpec((rows, D // LANES, LANES), lambda i, s: (i, 0, 0)),
            scratch_shapes=[pltpu.SemaphoreType.DMA(())],
        ),
        out_shape=jax.ShapeDtypeStruct((P, D // LANES, LANES), x.dtype),
        compiler_params=_cparams(1),
        name="gather_rows",
    )(src, x3)
    return out.reshape(P, D)


def _moe_up_kernel(te_ref, tf_ref, tv_ref, x_ref, w1_ref, w3_ref, o_ref, w1b_ref, w3b_ref):
    t = pl.program_id(1)

    @pl.when(tf_ref[t] == 1)
    def _():
        w1b_ref[...] = w1_ref[...].astype(BF16)
        w3b_ref[...] = w3_ref[...].astype(BF16)

    @pl.when(tv_ref[t] == 1)
    def _():
        x = x_ref[...]
        a = jnp.dot(x, w1b_ref[...], preferred_element_type=F32)
        b = jnp.dot(x, w3b_ref[...], preferred_element_type=F32)
        o_ref[...] = (a * _sigmoid(a) * b).astype(o_ref.dtype)

    @pl.when(tv_ref[t] == 0)
    def _():
        o_ref[...] = jnp.zeros(o_ref.shape, o_ref.dtype)


def moe_up(xs, w1, w3, layer, tile_expert, tile_first, tile_valid, *, tn):
    P, K = xs.shape
    N = w1.shape[3]
    tm = MOE_TM
    nt = P // tm
    wspec = pl.BlockSpec((None, None, K, tn), lambda n, t, te, tf, tv: (layer, te[t], 0, n))
    return pl.pallas_call(
        _moe_up_kernel,
        grid_spec=pltpu.PrefetchScalarGridSpec(
            num_scalar_prefetch=3,
            grid=(N // tn, nt),
            in_specs=[pl.BlockSpec((tm, K), lambda n, t, te, tf, tv: (t, 0)), wspec, wspec],
            out_specs=pl.BlockSpec((tm, tn), lambda n, t, te, tf, tv: (t, n)),
            scratch_shapes=[pltpu.VMEM((K, tn), BF16), pltpu.VMEM((K, tn), BF16)],
        ),
        out_shape=jax.ShapeDtypeStruct((P, N), BF16),
        compiler_params=_cparams(2),
        name="moe_up",
    )(tile_expert, tile_first, tile_valid, xs, w1, w3)


def _moe_down_kernel(te_ref, tf_ref, tv_ref, h_ref, w_ref, o_ref, wb_ref):
    t = pl.program_id(1)

    @pl.when(tf_ref[t] == 1)
    def _():
        wb_ref[...] = w_ref[...].astype(BF16)

    @pl.when(tv_ref[t] == 1)
    def _():
        o_ref[...] = jnp.dot(h_ref[...], wb_ref[...], preferred_element_type=F32)

    @pl.when(tv_ref[t] == 0)
    def _():
        o_ref[...] = jnp.zeros(o_ref.shape, o_ref.dtype)


def moe_down(h, w2, layer, tile_expert, tile_first, tile_valid, *, tn):
    P, K = h.shape
    N = w2.shape[3]
    tm = MOE_TM
    nt = P // tm
    return pl.pallas_call(
        _moe_down_kernel,
        grid_spec=pltpu.PrefetchScalarGridSpec(
            num_scalar_prefetch=3,
            grid=(N // tn, nt),
            in_specs=[pl.BlockSpec((tm, K), lambda n, t, te, tf, tv: (t, 0)),
                      pl.BlockSpec((None, None, K, tn), lambda n, t, te, tf, tv: (layer, te[t], 0, n))],
            out_specs=pl.BlockSpec((tm, tn), lambda n, t, te, tf, tv: (t, n)),
            scratch_shapes=[pltpu.VMEM((K, tn), BF16)],
        ),
        out_shape=jax.ShapeDtypeStruct((P, N), F32),
        compiler_params=_cparams(2),
        name="moe_down",
    )(tile_expert, tile_first, tile_valid, h, w2)


def _moe_combine_kernel(pos_ref, gate_ref, e_hbm, o_ref, buf_ref, sem, *, rows):
    base = pl.program_id(0) * rows

    def row_copy(kk, r, src_row):
        return pltpu.make_async_copy(e_hbm.at[src_row], buf_ref.at[kk, r], sem)

    def issue(r, c):
        for kk in range(TOP_K):
            row_copy(kk, r, pos_ref[kk, base + r]).start()
        return c

    lax.fori_loop(0, rows, issue, 0)

    def drain(r, c):
        for kk in range(TOP_K):
            row_copy(kk, r, 0).wait()
        return c

    lax.fori_loop(0, rows, drain, 0)
    o_ref[...] = gate_ref[:, 0] * buf_ref[0] + gate_ref[:, 1] * buf_ref[1]


def moe_combine(eout, pos, gate, *, rows=128):
    M = gate.shape[0]
    P, D = eout.shape
    Mp = pl.cdiv(M, rows) * rows
    e3 = eout.reshape(P, D // LANES, LANES)
    pos = jnp.pad(pos, ((0, 0), (0, Mp - M)))
    out = pl.pallas_call(
        functools.partial(_moe_combine_kernel, rows=rows),
        grid_spec=pltpu.PrefetchScalarGridSpec(
            num_scalar_prefetch=1,
            grid=(Mp // rows,),
            in_specs=[pl.BlockSpec((rows, TOP_K, 1, 1), lambda i, p: (i, 0, 0, 0)),
                      pl.BlockSpec(memory_space=pl.ANY)],
            out_specs=pl.BlockSpec((rows, D // LANES, LANES), lambda i, p: (i, 0, 0)),
            scratch_shapes=[pltpu.VMEM((TOP_K, rows, D // LANES, LANES), F32),
                            pltpu.SemaphoreType.DMA(())],
        ),
        out_shape=jax.ShapeDtypeStruct((M, D // LANES, LANES), F32),
        compiler_params=_cparams(1),
        name="moe_combine",
    )(pos, gate.reshape(M, TOP_K, 1, 1), e3)
    return out.reshape(M, D)


def moe_ffn(x, xb, router_w_l, w1, w3, w2, layer):
    M, D = x.shape
    tm = MOE_TM
    idx, gate = router(x, router_w_l)
    e_flat = idx.reshape(-1)
    onehot = (e_flat[:, None] == jnp.arange(N_EXPERTS)[None, :]).astype(jnp.int32)
    counts = jnp.sum(onehot, axis=0)
    rank = jnp.sum((jnp.cumsum(onehot, axis=0) - onehot) * onehot, axis=1)
    tiles_per = (counts + tm - 1) // tm
    tile_start = jnp.cumsum(tiles_per) - tiles_per
    n_tiles = (TOP_K * M) // tm + N_EXPERTS
    P = n_tiles * tm
    pos = tile_start[e_flat] * tm + rank
    token = jnp.tile(jnp.arange(M, dtype=jnp.int32), TOP_K)
    src = jnp.zeros((P,), jnp.int32).at[pos].set(token)
    t_io = jnp.arange(n_tiles)
    used = jnp.sum(tiles_per)
    tile_expert = jnp.clip(jnp.sum(t_io[:, None] >= tile_start[None, :], axis=1) - 1, 0, N_EXPERTS - 1)
    tile_valid = (t_io < used).astype(jnp.int32)
    last_used_expert = tile_expert[jnp.maximum(used - 1, 0)]
    tile_expert = jnp.where(tile_valid == 1, tile_expert, last_used_expert).astype(jnp.int32)
    prev = jnp.concatenate([jnp.array([-1], jnp.int32), tile_expert[:-1]])
    tile_first = (tile_expert != prev).astype(jnp.int32)
    xs = gather_rows(xb, src)
    h = moe_up(xs, w1, w3, layer, tile_expert, tile_first, tile_valid, tn=512)
    eout = moe_down(h, w2, layer, tile_expert, tile_first, tile_valid, tn=256)
    return moe_combine(eout, pos.reshape(TOP_K, M).astype(jnp.int32), gate.T)


def _mixer_out(z, layer, w_pa, w_pb, w_out, ya, yb):
    m = branch_merge(ya, yb, z, w_pa, w_pb, layer, tm=1024, tn=512)
    return matmul(m, w_out, layer, tm=1024, tn=512)


def kernel(x_prompt, x_sample, cache_k, cache_v, page_table, state_hgrn, rel_bias, w_in, w_pa, w_pb, w_out,
           lam_qk, subln_g, hgrn_g, lb_raw, ln_g, ln_b, ffn_w1, ffn_w3, ffn_w2, router_w, moe_w1, moe_w3, moe_w2):
    B, S, D = x_prompt.shape
    DB = x_sample.shape[0]
    MP = B * S
    lbs = jax.nn.softmax(lb_raw.astype(F32), axis=0)
    lbs = jnp.cumsum(lbs, axis=0) - lbs[0]
    bias_tiles = _bias_tiles(rel_bias, ATT_T)

    xp = x_prompt.reshape(MP, D)
    xs = x_sample.reshape(DB, D)
    xpb = xp.astype(BF16)
    xsb = xs.astype(BF16)
    outs = {n: [] for n in ("kp", "vp", "sp", "ks", "vs", "ss")}
    for l in range(DEPTH):
        lam_init = 0.8 - 0.6 * math.exp(-0.3 * l)
        lq = lam_qk[l].astype(F32)
        lam = jnp.exp(jnp.sum(lq[0] * lq[1])) - jnp.exp(jnp.sum(lq[2] * lq[3])) + lam_init

        zp = matmul(xpb, w_in, l, tm=1024, tn=512)
        ya_p = prompt_attention(zp, bias_tiles, lam, subln_g[l], lam_init, B, S)
        yb_p, s_p = hgrn_prompt(zp, lbs[l], hgrn_g[l], B, S)
        mo_p = _mixer_out(zp, l, w_pa, w_pb, w_out, ya_p, yb_p)
        zs = matmul(xsb, w_in, l, tm=1024, tn=512)
        q_s = zs[:, COL["q_a"]:COL["q_a"] + W_A].reshape(DB, H_A, DV_A)
        k_s = zs[:, COL["k_a"]:COL["k_a"] + W_A].reshape(DB, H_A, DV_A)
        v_s = zs[:, COL["v_a"]:COL["v_a"] + W_A].reshape(DB, H_A, DV_A)
        ya_s = sample_attention(q_s, k_s, v_s, cache_k, cache_v, page_table, rel_bias, lam, subln_g[l],
                                lam_init, l).reshape(DB, W_A)
        colv = lambda n: zs[:, COL[n]:COL[n] + W_BK].reshape(DB, H_B, DK_B, 1)
        rowv = lambda n: zs[:, COL[n]:COL[n] + W_B].reshape(DB, H_B, 1, DV_B)
        yb_s, s_s = hgrn_step(colv("f_b"), colv("q_b"), rowv("i_b"), rowv("og_b"), lbs[l], hgrn_g[l],
                              state_hgrn[l])
        mo_s = _mixer_out(zs, l, w_pa, w_pb, w_out, ya_s, yb_s.reshape(DB, W_B))

        xp, xpb = residual_ln(xp, mo_p, ln_g[l, 0], ln_b[l, 0])
        xs, xsb = residual_ln(xs, mo_s, ln_g[l, 0], ln_b[l, 0])

        j = l // 2
        if l % 2 == 0:
            hp = swiglu_up(xpb, ffn_w1, ffn_w3, j, tm=1024, tn=512)
            cp = matmul(hp, ffn_w2, j, tm=512, tn=256)
            hs = swiglu_up(xsb, ffn_w1, ffn_w3, j, tm=1024, tn=512)
            cs = matmul(hs, ffn_w2, j, tm=512, tn=256)
            xp, xpb = residual_ln(xp, cp, ln_g[l, 1], ln_b[l, 1])
            xs, xsb = residual_ln(xs, cs, ln_g[l, 1], ln_b[l, 1])
        else:
            xall = jnp.concatenate([xp, xs], axis=0)
            xallb = jnp.concatenate([xpb, xsb], axis=0)
            call = moe_ffn(xall, xallb, router_w[j], moe_w1, moe_w3, moe_w2, j)
            xp, xpb = residual_ln(xp, call[:MP], ln_g[l, 1], ln_b[l, 1])
            xs, xsb = residual_ln(xs, call[MP:], ln_g[l, 1], ln_b[l, 1])

        outs["kp"].append(zp[:, COL["k_a"]:COL["k_a"] + W_A].reshape(B, S, H_A, 2 * DK_A))
        outs["vp"].append(zp[:, COL["v_a"]:COL["v_a"] + W_A].reshape(B, S, H_A, DV_A))
        outs["sp"].append(s_p)
        outs["ks"].append(k_s.reshape(DB, 1, H_A, 2 * DK_A))
        outs["vs"].append(v_s.reshape(DB, 1, H_A, DV_A))
        outs["ss"].append(s_s)
    return (xp.reshape(B, S, D), xs.reshape(DB, 1, D),
            jnp.stack(outs["kp"]), jnp.stack(outs["vp"]), jnp.stack(outs["sp"]),
            jnp.stack(outs["ks"]), jnp.stack(outs["vs"]), jnp.stack(outs["ss"]))
```

```python
import functools
import math

import numpy as np
import jax
import jax.numpy as jnp
from jax import lax
from jax.experimental import pallas as pl
from jax.experimental.pallas import tpu as pltpu

F32 = jnp.float32
BF16 = jnp.bfloat16

D_MODEL = 2048
DEPTH = 2
PAGE_SIZE = 128
H_A = 8
DK_A = 64
DV_A = 2 * DK_A
W_A = H_A * DV_A
H_B = 8
DK_B = 128
DV_B = 128
W_BK = H_B * DK_B
W_B = H_B * DV_B
CHUNK = 64
N_BUCKETS = 32
MAX_EXACT = N_BUCKETS // 2
MAX_DISTANCE = 128
N_EXPERTS = 8
TOP_K = 2
ALPHA = (2 * DEPTH) ** 0.25
LN_EPS = 1e-5
RMS_EPS = 1e-6
F_MIN = 1e-20
NEG_INF = -1e30
SPLITS = [H_A * 2 * DK_A, H_A * 2 * DK_A, W_A, W_BK, W_BK, W_B, W_B, D_MODEL, D_MODEL]
N_IN = sum(SPLITS)
COL = {}
_acc = 0
for _name, _w in zip(["q_a", "k_a", "v_a", "f_b", "q_b", "i_b", "og_b", "gt_a", "gt_b"], SPLITS):
    COL[_name] = _acc
    _acc += _w

LANES = 128
SUBLANES = 8
V7X_VMEM_BYTES = 64 * 1024 * 1024
VMEM_LIMIT = 56 * 1024 * 1024

ATT_T = 256
ATT_HALF = LANES
MOE_TM = 512
PAGES_PER_STEP = 8
HGRN_UNROLL = 4


def _cparams(n_axes):
    return pltpu.CompilerParams(
        dimension_semantics=("arbitrary",) * n_axes, vmem_limit_bytes=VMEM_LIMIT)


def _sigmoid(x):
    return jax.nn.sigmoid(x)


def _dot_nt(a, b):
    return lax.dot_general(a, b, (((1,), (1,)), ((), ())), preferred_element_type=F32)


def _dot_tn(a, b):
    return lax.dot_general(a, b, (((0,), (0,)), ((), ())), preferred_element_type=F32)


def _for_rows(m, M, tm, fn):
    n_full, tail = M // tm, M % tm
    if n_full:
        @pl.when(m < n_full)
        def _():
            fn(slice(None))
    if tail:
        @pl.when(m == n_full)
        def _():
            fn(slice(0, tail))


def _mm_kernel(x_ref, w_ref, o_ref, wb_ref, *, M, tm):
    m = pl.program_id(1)

    @pl.when(m == 0)
    def _():
        wb_ref[...] = w_ref[...].astype(BF16)

    def body(rs):
        o_ref[rs, :] = jnp.dot(x_ref[rs, :], wb_ref[...], preferred_element_type=F32).astype(o_ref.dtype)

    _for_rows(m, M, tm, body)


def matmul(x, w, layer, *, tm, tn, out_dtype=F32):
    M, K = x.shape
    N = w.shape[2]
    assert N % tn == 0
    return pl.pallas_call(
        functools.partial(_mm_kernel, M=M, tm=tm),
        grid=(N // tn, pl.cdiv(M, tm)),
        in_specs=[
            pl.BlockSpec((tm, K), lambda n, m: (m, 0)),
            pl.BlockSpec((None, K, tn), lambda n, m: (layer, 0, n)),
        ],
        out_specs=pl.BlockSpec((tm, tn), lambda n, m: (m, n)),
        out_shape=jax.ShapeDtypeStruct((M, N), out_dtype),
        scratch_shapes=[pltpu.VMEM((K, tn), BF16)],
        compiler_params=_cparams(2),
        name="matmul",
    )(x, w)


def _swiglu_kernel(x_ref, w1_ref, w3_ref, o_ref, w1b_ref, w3b_ref, *, M, tm):
    m = pl.program_id(1)

    @pl.when(m == 0)
    def _():
        w1b_ref[...] = w1_ref[...].astype(BF16)
        w3b_ref[...] = w3_ref[...].astype(BF16)

    def body(rs):
        x = x_ref[rs, :]
        a = jnp.dot(x, w1b_ref[...], preferred_element_type=F32)
        b = jnp.dot(x, w3b_ref[...], preferred_element_type=F32)
        o_ref[rs, :] = (a * _sigmoid(a) * b).astype(o_ref.dtype)

    _for_rows(m, M, tm, body)


def swiglu_up(x, w1, w3, layer, *, tm, tn):
    M, K = x.shape
    N = w1.shape[2]
    assert N % tn == 0
    return pl.pallas_call(
        functools.partial(_swiglu_kernel, M=M, tm=tm),
        grid=(N // tn, pl.cdiv(M, tm)),
        in_specs=[
            pl.BlockSpec((tm, K), lambda n, m: (m, 0)),
            pl.BlockSpec((None, K, tn), lambda n, m: (layer, 0, n)),
            pl.BlockSpec((None, K, tn), lambda n, m: (layer, 0, n)),
        ],
        out_specs=pl.BlockSpec((tm, tn), lambda n, m: (m, n)),
        out_shape=jax.ShapeDtypeStruct((M, N), BF16),
        scratch_shapes=[pltpu.VMEM((K, tn), BF16), pltpu.VMEM((K, tn), BF16)],
        compiler_params=_cparams(2),
        name="swiglu_up",
    )(x, w1, w3)


def _merge_kernel(ya_ref, yb_ref, yas_ref, ybs_ref, wa_ref, wb_ref, ga_ref, gb_ref, o_ref, wab_ref, wbb_ref, *,
                  n_full, tail):
    m = pl.program_id(1)

    @pl.when(m == 0)
    def _():
        wab_ref[...] = wa_ref[...].astype(BF16)
        wbb_ref[...] = wb_ref[...].astype(BF16)

    def merge(ya, yb, rs):
        pa = jnp.dot(ya, wab_ref[...], preferred_element_type=F32)
        pb = jnp.dot(yb, wbb_ref[...], preferred_element_type=F32)
        o_ref[rs, :] = (_sigmoid(ga_ref[rs, :]) * pa + _sigmoid(gb_ref[rs, :]) * pb).astype(o_ref.dtype)

    @pl.when(m < n_full)
    def _():
        merge(ya_ref[...], yb_ref[...], slice(None))

    @pl.when(m == n_full)
    def _():
        merge(yas_ref[...], ybs_ref[...], slice(0, tail))


def branch_merge(ya, yb, ya_s, yb_s, z, w_pa, w_pb, layer, *, tm, tn):
    MP, K = ya.shape
    DB = ya_s.shape[0]
    M = z.shape[0]
    N = w_pa.shape[2]
    assert N % tn == 0 and MP % tm == 0 and M == MP + DB and DB < tm
    n_full = MP // tm
    ga0, gb0 = COL["gt_a"] // tn, COL["gt_b"] // tn
    prow = pl.BlockSpec((tm, K), lambda n, m: (jnp.minimum(m, n_full - 1), 0))
    srow = pl.BlockSpec((DB, K), lambda n, m: (0, 0))
    return pl.pallas_call(
        functools.partial(_merge_kernel, n_full=n_full, tail=DB),
        grid=(N // tn, n_full + 1),
        in_specs=[
            prow, prow, srow, srow,
            pl.BlockSpec((None, K, tn), lambda n, m: (layer, 0, n)),
            pl.BlockSpec((None, K, tn), lambda n, m: (layer, 0, n)),
            pl.BlockSpec((tm, tn), lambda n, m: (m, ga0 + n)),
            pl.BlockSpec((tm, tn), lambda n, m: (m, gb0 + n)),
        ],
        out_specs=pl.BlockSpec((tm, tn), lambda n, m: (m, n)),
        out_shape=jax.ShapeDtypeStruct((M, N), BF16),
        scratch_shapes=[pltpu.VMEM((K, tn), BF16), pltpu.VMEM((K, tn), BF16)],
        compiler_params=_cparams(2),
        name="branch_merge",
    )(ya, yb, ya_s, yb_s, w_pa, w_pb, z, z)


def _add_ln(x, y, g, b):
    h = ALPHA * x + y
    mu = jnp.mean(h, axis=-1, keepdims=True)
    c = h - mu
    var = jnp.mean(c * c, axis=-1, keepdims=True)
    return c * lax.rsqrt(var + LN_EPS) * g + b


def _res_ln_kernel(x_ref, y_ref, g_ref, b_ref, o_ref, ob_ref, *, M, tm):
    def body(rs):
        o = _add_ln(x_ref[rs, :], y_ref[rs, :], g_ref[...], b_ref[...])
        o_ref[rs, :] = o
        ob_ref[rs, :] = o.astype(BF16)

    _for_rows(pl.program_id(0), M, tm, body)


def residual_ln(x, y, g, b, *, tm=256):
    M, D = x.shape
    row = pl.BlockSpec((tm, D), lambda m: (m, 0))
    vec = pl.BlockSpec((1, D), lambda m: (0, 0))
    return pl.pallas_call(
        functools.partial(_res_ln_kernel, M=M, tm=tm),
        grid=(pl.cdiv(M, tm),),
        in_specs=[row, row, vec, vec],
        out_specs=[row, row],
        out_shape=[jax.ShapeDtypeStruct((M, D), F32), jax.ShapeDtypeStruct((M, D), BF16)],
        compiler_params=_cparams(1),
        name="residual_ln",
    )(x, y, g.reshape(1, D), b.reshape(1, D))


def _t5_bucket(rel):
    n = jnp.maximum(rel, 0)
    large = MAX_EXACT + (jnp.log(jnp.maximum(n, 1).astype(F32) / MAX_EXACT)
                         / math.log(MAX_DISTANCE / MAX_EXACT) * (N_BUCKETS - MAX_EXACT)).astype(jnp.int32)
    large = jnp.clip(large, 0, N_BUCKETS - 1)
    return jnp.where(n < MAX_EXACT, n, large)


def _bucket_np(n):
    n = np.maximum(n, 0)
    large = MAX_EXACT + (np.log(np.maximum(n, 1) / MAX_EXACT)
                         / math.log(MAX_DISTANCE / MAX_EXACT) * (N_BUCKETS - MAX_EXACT)).astype(np.int64)
    return np.where(n < MAX_EXACT, n, np.clip(large, 0, N_BUCKETS - 1))


assert (_bucket_np(np.arange(min(ATT_T, PAGE_SIZE) + 1, 1 << 16)) == N_BUCKETS - 1).all()


def _bias_tiles_t(rel_bias, T):
    n = jnp.arange(-(T - 1), 3 * T)
    vec = jnp.where((n >= 0)[:, None], rel_bias[_t5_bucket(n)].astype(F32), NEG_INF).T
    L = 2 * T
    tiles = []
    for d in range(3):
        f0 = d * T + (T - 1)
        a = jnp.concatenate([vec[:, f0:f0 + T], vec[:, :1], vec[:, f0 - (T - 1):f0]], axis=1)
        skew = jnp.tile(a, (1, T))[:, :T * (L - 1)].reshape(H_A, T, L - 1)
        tiles.append(skew[:, :, :T])
    return jnp.stack(tiles, axis=1)


def _rms_head(o, g):
    return o * lax.rsqrt(jnp.mean(o * o, axis=-1, keepdims=True) + RMS_EPS) * g


def _prompt_attn_kernel(lam_ref, q_ref, k_ref, v_ref, bias_ref, g_ref, o_ref, kb_ref, vb_ref, *,
                        T, out_scale):
    qi = pl.program_id(2)
    n_half = T // ATT_HALF

    @pl.when(qi == 0)
    def _():
        kb_ref[...] = k_ref[...].astype(BF16)
        vb_ref[...] = v_ref[...].astype(BF16)

    lane = lax.broadcasted_iota(jnp.int32, (1, 2 * DK_A), 1)
    q = q_ref[...] * (DK_A ** -0.5)
    qm = (jnp.where(lane < DK_A, q, 0.0).astype(BF16), jnp.where(lane >= DK_A, q, 0.0).astype(BF16))
    blocks = [(mi, hi) for mi in range(2) for hi in range(n_half)]

    def step(kj, carry):
        rows = pl.ds(pl.multiple_of(kj * T, T), T)
        k = kb_ref[rows, :]
        v = vb_ref[rows, :]
        tile = jnp.minimum(qi - kj, 2)

        def scores(i):
            mi, hi = blocks[i]
            cols = slice(hi * ATT_HALF, (hi + 1) * ATT_HALF)
            return _dot_nt(k, qm[mi][cols, :]) + bias_ref[tile, :, cols]

        def softmax_pv(s, state):
            m, l, a = state
            m_new = jnp.maximum(m, jnp.max(s, axis=0, keepdims=True))
            p = jnp.exp(s - m_new)
            r = jnp.exp(m - m_new)
            l_new = r * l + jnp.sum(p, axis=0, keepdims=True)
            a_new = r * a + _dot_tn(v, p.astype(BF16))
            return m_new, l_new, a_new

        s_next = scores(0)
        out = []
        for i in range(len(blocks)):
            s_cur = s_next
            if i + 1 < len(blocks):
                s_next = scores(i + 1)
            out.append(softmax_pv(s_cur, carry[i]))
        return tuple(out)

    init = tuple((jnp.full((1, ATT_HALF), NEG_INF, F32), jnp.zeros((1, ATT_HALF), F32),
                  jnp.zeros((DV_A, ATT_HALF), F32)) for _ in blocks)
    res = lax.fori_loop(0, qi + 1, step, init)
    lam = lam_ref[0]
    for hi in range(n_half):
        _, l1, a1 = res[hi]
        _, l2, a2 = res[n_half + hi]
        ot = a1 / l1 - lam * (a2 / l2)
        ot = ot * lax.rsqrt(jnp.mean(ot * ot, axis=0, keepdims=True) + RMS_EPS)
        o_ref[hi * ATT_HALF:(hi + 1) * ATT_HALF, :] = (ot.T * g_ref[...] * out_scale).astype(o_ref.dtype)


def prompt_attention(z, bias_tiles, lam, subln_g_l, lam_init, B, S):
    T = ATT_T
    nq = S // T
    qc, kc, vc = COL["q_a"] // DV_A, COL["k_a"] // DV_A, COL["v_a"] // DV_A
    kern = functools.partial(_prompt_attn_kernel, T=T, out_scale=1.0 - lam_init)
    return pl.pallas_call(
        kern,
        grid=(B, H_A, nq),
        in_specs=[
            pl.BlockSpec(memory_space=pltpu.SMEM),
            pl.BlockSpec((T, DV_A), lambda b, h, i: (b * nq + i, qc + h)),
            pl.BlockSpec((S, DV_A), lambda b, h, i: (b, kc + h)),
            pl.BlockSpec((S, DV_A), lambda b, h, i: (b, vc + h)),
            pl.BlockSpec((None, 3, T, T), lambda b, h, i: (h, 0, 0, 0)),
            pl.BlockSpec((None, 1, DV_A), lambda b, h, i: (h, 0, 0)),
        ],
        out_specs=pl.BlockSpec((T, DV_A), lambda b, h, i: (b * nq + i, h)),
        out_shape=jax.ShapeDtypeStruct((B * S, W_A), BF16),
        scratch_shapes=[pltpu.VMEM((S, DV_A), BF16), pltpu.VMEM((S, DV_A), BF16)],
        compiler_params=_cparams(3),
        name="prompt_attention",
    )(lam.reshape(1), z, z, z, bias_tiles, subln_g_l.reshape(H_A, 1, DV_A))


_LEVELS = [CHUNK >> (i + 1) for i in range(int(math.log2(CHUNK)))]


def _cumsum_and_anchor_matrix():
    t = np.arange(CHUNK)
    tri = (t[None, :] <= t[:, None]).astype(np.float32)
    mats = [tri]
    for w in _LEVELS:
        anchor_row = t - (t % (2 * w)) + w - 1
        mats.append(tri[anchor_row])
    return np.concatenate(mats, axis=0)


def _split3(x):
    x1 = x.astype(BF16)
    r1 = x - x1.astype(F32)
    x2 = r1.astype(BF16)
    x3 = (r1 - x2.astype(F32)).astype(BF16)
    return x1, x2, x3


def _hgrn_gates(fx, qx, lb):
    f = lb + (1.0 - lb) * _sigmoid(fx)
    g = jnp.log(jnp.maximum(f, F_MIN))
    k = (1.0 - lb) * _sigmoid(-fx)
    q = qx * _sigmoid(qx)
    return g, k, q


def _hgrn_prompt_kernel(f_ref, q_ref, i_ref, og_ref, lb_ref, gain_ref, pm_ref, y_ref, s_ref,
                        oi_ref, qe_ref, u_ref, dl_ref, *, S):
    n_chunks = S // CHUNK
    lb = lb_ref[...]
    gain = gain_ref[...]
    pm = pm_ref[...]
    t_io = lax.broadcasted_iota(jnp.int32, (CHUNK, CHUNK), 0)
    s_io = lax.broadcasted_iota(jnp.int32, (CHUNK, CHUNK), 1)
    masks = [(((t_io ^ s_io) < 2 * w) & ((t_io & w) != 0)) & ((s_io & w) == 0) for w in _LEVELS]
    diag = t_io == s_io

    def group(gi, carry):
        cs = [gi * HGRN_UNROLL + u for u in range(HGRN_UNROLL)]
        rows = [pl.ds(pl.multiple_of(c * CHUNK, CHUNK), CHUNK) for c in cs]
        gkq = [_hgrn_gates(f_ref[r, :], q_ref[r, :], lb) for r in rows]
        vbs = [i_ref[r, :].astype(BF16) for r in rows]
        balls = []
        for g, _, _ in gkq:
            g1, g2, g3 = _split3(g)
            balls.append(jnp.dot(pm, g1, preferred_element_type=F32)
                         + jnp.dot(pm, g2, preferred_element_type=F32)
                         + jnp.dot(pm, g3, preferred_element_type=F32))
        bs = [ball[0:CHUNK] for ball in balls]
        a_s = [jnp.where(diag, _dot_nt(q.astype(BF16), k.astype(BF16)), 0.0) for _, k, q in gkq]
        for li in range(len(_LEVELS)):
            for u, (_, k, q) in enumerate(gkq):
                anc = balls[u][(li + 1) * CHUNK:(li + 2) * CHUNK]
                qs = (q * jnp.exp(jnp.minimum(bs[u] - anc, 0.0))).astype(BF16)
                ks = (k * jnp.exp(jnp.minimum(anc - bs[u], 0.0))).astype(BF16)
                a_s[u] = jnp.where(masks[li], _dot_nt(qs, ks), a_s[u])
        for u, (_, k, q) in enumerate(gkq):
            b = bs[u]
            b_last = b[CHUNK - 1:CHUNK, :]
            oi_ref[rows[u], :] = jnp.dot(a_s[u].astype(BF16), vbs[u], preferred_element_type=F32)
            qe_ref[rows[u], :] = (q * jnp.exp(b)).astype(BF16)
            kd = (k * jnp.exp(b_last - b)).astype(BF16)
            u_ref[cs[u]] = _dot_tn(vbs[u], kd)
            dl_ref[cs[u]] = jnp.exp(b_last)
        return carry

    lax.fori_loop(0, n_chunks // HGRN_UNROLL, group, 0)

    def phase_b(c, st):
        rows = pl.ds(pl.multiple_of(c * CHUNK, CHUNK), CHUNK)
        o = oi_ref[rows, :] + _dot_nt(qe_ref[rows, :], st.astype(BF16))
        y = _rms_head(o, gain) * _sigmoid(og_ref[rows, :])
        y_ref[rows, :] = y.astype(y_ref.dtype)
        return st * dl_ref[c] + u_ref[c]

    st = lax.fori_loop(0, n_chunks, phase_b, jnp.zeros((DV_B, DK_B), F32), unroll=HGRN_UNROLL)
    s_ref[...] = st.T


def hgrn_prompt(z, lb_l, hgrn_g_l, B, S):
    assert (S // CHUNK) % HGRN_UNROLL == 0
    fc, qc, ic, oc = (COL[n] // DK_B for n in ("f_b", "q_b", "i_b", "og_b"))
    pm = jnp.asarray(_cumsum_and_anchor_matrix(), BF16)
    kern = functools.partial(_hgrn_prompt_kernel, S=S)
    blk = lambda c0: pl.BlockSpec((S, DK_B), lambda b, h: (b, c0 + h))
    n_chunks = S // CHUNK
    return pl.pallas_call(
        kern,
        grid=(B, H_B),
        in_specs=[
            blk(fc), blk(qc), blk(ic), blk(oc),
            pl.BlockSpec((None, 1, DK_B), lambda b, h: (h, 0, 0)),
            pl.BlockSpec((None, 1, DV_B), lambda b, h: (h, 0, 0)),
            pl.BlockSpec(pm.shape, lambda b, h: (0, 0)),
        ],
        out_specs=[
            pl.BlockSpec((S, DV_B), lambda b, h: (b, h)),
            pl.BlockSpec((None, None, DK_B, DV_B), lambda b, h: (b, h, 0, 0)),
        ],
        out_shape=[
            jax.ShapeDtypeStruct((B * S, W_B), BF16),
            jax.ShapeDtypeStruct((B, H_B, DK_B, DV_B), F32),
        ],
        scratch_shapes=[pltpu.VMEM((S, DV_B), F32), pltpu.VMEM((S, DK_B), BF16),
                        pltpu.VMEM((n_chunks, DV_B, DK_B), F32), pltpu.VMEM((n_chunks, 1, DK_B), F32)],
        compiler_params=_cparams(2),
        name="hgrn_prompt",
    )(z, z, z, z, lb_l.reshape(H_B, 1, DK_B), hgrn_g_l.reshape(H_B, 1, DV_B), pm)


PAGE_ROWS = PAGE_SIZE * H_A
assert H_A & (H_A - 1) == 0


def _paged_attn_kernel(pt_ref, lam_ref, q_ref, kn_ref, vn_ref, bfar_ref, blast_ref, bnew_ref, g_ref, *rest,
                       n_groups, out_scale):
    G = PAGES_PER_STEP
    R = 2 * H_A
    k_refs = rest[:G]
    v_refs = rest[G:2 * G]
    o_ref = rest[2 * G]
    m_ref, l_ref, a_ref = rest[2 * G + 1:]
    j = pl.program_id(1)
    lane = lax.broadcasted_iota(jnp.int32, (1, 2 * DK_A), 1)
    q = q_ref[...] * (DK_A ** -0.5)
    qf = jnp.concatenate([jnp.where(lane < DK_A, q, 0.0), jnp.where(lane >= DK_A, q, 0.0)], axis=0)
    qb = qf.astype(BF16)
    col = lax.broadcasted_iota(jnp.int32, (R, PAGE_ROWS), 1)
    row = lax.broadcasted_iota(jnp.int32, (R, PAGE_ROWS), 0)
    valid = (col & (H_A - 1)) == (row & (H_A - 1))

    @pl.when(j == 0)
    def _():
        m_ref[...] = jnp.full(m_ref.shape, NEG_INF, F32)
        l_ref[...] = jnp.zeros(l_ref.shape, F32)
        a_ref[...] = jnp.zeros(a_ref.shape, F32)

    def scores(i, bias):
        s = _dot_nt(qb, k_refs[i][...].astype(BF16)) + bias
        return jnp.where(valid, s, NEG_INF)

    def absorb(s_list, new_token):
        m_old = m_ref[...]
        m_new = m_old
        for s in s_list:
            m_new = jnp.maximum(m_new, jnp.max(s, axis=-1, keepdims=True))
        if new_token:
            kn2 = jnp.concatenate([kn_ref[...], kn_ref[...]], axis=0)
            vn2 = jnp.concatenate([vn_ref[...], vn_ref[...]], axis=0)
            s_new = jnp.sum(qf * kn2, axis=-1, keepdims=True) + bnew_ref[...]
            m_new = jnp.maximum(m_new, s_new)
        r = jnp.exp(m_old - m_new)
        l = r * l_ref[...]
        acc = r * a_ref[...]
        for i, s in enumerate(s_list):
            p = jnp.exp(s - m_new)
            l = l + jnp.sum(p, axis=-1, keepdims=True)
            acc = acc + jnp.dot(p.astype(BF16), v_refs[i][...].astype(BF16), preferred_element_type=F32)
        if new_token:
            p_new = jnp.exp(s_new - m_new)
            l = l + p_new
            acc = acc + p_new * vn2
        m_ref[...] = m_new
        l_ref[...] = l
        a_ref[...] = acc

    far = bfar_ref[...]

    @pl.when(j < n_groups - 1)
    def _():
        absorb([scores(i, far) for i in range(G)], False)

    @pl.when(j == n_groups - 1)
    def _():
        absorb([scores(i, far) for i in range(G - 1)] + [scores(G - 1, blast_ref[...])], True)
        lam = lam_ref[0]
        o = a_ref[0:H_A] / l_ref[0:H_A] - lam * (a_ref[H_A:R] / l_ref[H_A:R])
        o_ref[...] = (_rms_head(o, g_ref[...]) * out_scale).astype(o_ref.dtype)


def sample_attention(q, k_new, v_new, cache_k, cache_v, page_table, rel_bias, lam, subln_g_l,
                     lam_init, layer):
    DB = q.shape[0]
    n_pages = page_table.shape[1]
    G = PAGES_PER_STEP
    R = 2 * H_A
    assert n_pages % G == 0
    n_groups = n_pages // G
    hmap = jnp.tile(jnp.arange(H_A), 2)
    bias_of = lambda dist: rel_bias[_t5_bucket(dist)].astype(F32)
    bfar = bias_of(jnp.array(2 * PAGE_SIZE))[hmap][:, None]
    bnew = bias_of(jnp.array(0))[hmap][:, None]
    blast = jnp.broadcast_to(bias_of(PAGE_SIZE - jnp.arange(PAGE_SIZE)).reshape(1, PAGE_ROWS), (R, PAGE_ROWS))
    kern = functools.partial(_paged_attn_kernel, n_groups=n_groups, out_scale=1.0 - lam_init)

    def page_spec(i):
        return pl.BlockSpec((None, None, PAGE_ROWS, DV_A),
                            lambda b, j, pt: (layer, pt[b * n_pages + j * G + i], 0, 0))

    tok = pl.BlockSpec((None, H_A, DV_A), lambda b, j, pt: (b, 0, 0))
    const = lambda shape: pl.BlockSpec(shape, lambda b, j, pt: (0,) * len(shape))
    grid_spec = pltpu.PrefetchScalarGridSpec(
        num_scalar_prefetch=1,
        grid=(DB, n_groups),
        in_specs=[
            pl.BlockSpec(memory_space=pltpu.SMEM),
            tok, tok, tok,
            const((R, 1)), const((R, PAGE_ROWS)), const((R, 1)), const((H_A, DV_A)),
        ] + [page_spec(i) for i in range(G)] + [page_spec(i) for i in range(G)],
        out_specs=tok,
        scratch_shapes=[pltpu.VMEM((R, 1), F32), pltpu.VMEM((R, 1), F32), pltpu.VMEM((R, DV_A), F32)],
    )
    return pl.pallas_call(
        kern,
        grid_spec=grid_spec,
        out_shape=jax.ShapeDtypeStruct((DB, H_A, DV_A), BF16),
        compiler_params=_cparams(2),
        name="sample_attention",
    )(page_table.reshape(-1), lam.reshape(1), q, k_new, v_new, bfar, blast, bnew,
      subln_g_l.reshape(H_A, DV_A), *([cache_k] * G), *([cache_v] * G))


def _hgrn_step_kernel(fx_ref, qx_ref, v_ref, og_ref, lb_ref, gain_ref, s_ref, y_ref, so_ref):
    g, k, q = _hgrn_gates(fx_ref[...], qx_ref[...], lb_ref[...])
    v = v_ref[...]
    s_new = jnp.exp(g) * s_ref[...].astype(F32) + k * v
    so_ref[...] = s_new.astype(so_ref.dtype)
    o = jnp.sum(q * s_new, axis=1, keepdims=True)
    y = _rms_head(o, gain_ref[...]) * _sigmoid(og_ref[...])
    y_ref[...] = y.astype(y_ref.dtype)


def hgrn_step(fx, qx, v, og, lb_l, hgrn_g_l, state):
    DB = fx.shape[0]
    col = pl.BlockSpec((None, H_B, DK_B, 1), lambda b: (b, 0, 0, 0))
    row = pl.BlockSpec((None, H_B, 1, DV_B), lambda b: (b, 0, 0, 0))
    st = pl.BlockSpec((None, H_B, DK_B, DV_B), lambda b: (b, 0, 0, 0))
    return pl.pallas_call(
        _hgrn_step_kernel,
        grid=(DB,),
        in_specs=[col, col, row, row,
                  pl.BlockSpec((H_B, DK_B, 1), lambda b: (0, 0, 0)),
                  pl.BlockSpec((H_B, 1, DV_B), lambda b: (0, 0, 0)),
                  st],
        out_specs=[row, st],
        out_shape=[jax.ShapeDtypeStruct((DB, H_B, 1, DV_B), BF16),
                   jax.ShapeDtypeStruct(state.shape, state.dtype)],
        compiler_params=_cparams(1),
        name="hgrn_step",
    )(fx, qx, v, og, lb_l.reshape(H_B, DK_B, 1), hgrn_g_l.reshape(H_B, 1, DV_B), state)


def _router_kernel(x_ref, w_ref, idx_ref, gate_ref):
    x1, x2, _ = _split3(x_ref[...])
    w1, w2, _ = _split3(w_ref[...])
    lt = _dot_nt(w1, x1) + (_dot_nt(w1, x2) + _dot_nt(w2, x1))
    e_io = lax.broadcasted_iota(jnp.int32, lt.shape, 0)
    m1 = jnp.max(lt, axis=0, keepdims=True)
    i1 = jnp.min(jnp.where(lt == m1, e_io, N_EXPERTS), axis=0, keepdims=True)
    lt2 = jnp.where(e_io == i1, -jnp.inf, lt)
    m2 = jnp.max(lt2, axis=0, keepdims=True)
    i2 = jnp.min(jnp.where(lt2 == m2, e_io, N_EXPERTS), axis=0, keepdims=True)
    e2 = jnp.exp(m2 - m1)
    den = 1.0 + e2
    idx_ref[0:1, :] = i1
    idx_ref[1:2, :] = i2
    gate_ref[0:1, :] = 1.0 / den
    gate_ref[1:2, :] = e2 / den


def router(x, router_w_l, *, tm=512):
    M, D = x.shape
    return pl.pallas_call(
        _router_kernel,
        grid=(pl.cdiv(M, tm),),
        in_specs=[pl.BlockSpec((tm, D), lambda m: (m, 0)),
                  pl.BlockSpec((N_EXPERTS, D), lambda m: (0, 0))],
        out_specs=[pl.BlockSpec((TOP_K, tm), lambda m: (0, m)), pl.BlockSpec((TOP_K, tm), lambda m: (0, m))],
        out_shape=[jax.ShapeDtypeStruct((TOP_K, M), jnp.int32), jax.ShapeDtypeStruct((TOP_K, M), F32)],
        compiler_params=_cparams(1),
        name="router",
    )(x, router_w_l.T)


def _scatter_rows_kernel(pos_ref, x_ref, xs_in, xs_hbm, sem, *, rows, M):
    del xs_in
    base = pl.program_id(0) * rows
    n = jnp.minimum(rows, M - base)

    def row_copy(kk, r, dst_row):
        return pltpu.make_async_copy(x_ref.at[r], xs_hbm.at[dst_row], sem)

    def issue(r, c):
        for kk in range(TOP_K):
            row_copy(kk, r, pos_ref[kk, base + r]).start()
        return c

    lax.fori_loop(0, n, issue, 0)

    def drain(r, c):
        for kk in range(TOP_K):
            row_copy(kk, r, 0).wait()
        return c

    lax.fori_loop(0, n, drain, 0)


def scatter_rows(x, pos, P, *, rows=256):
    M, D = x.shape
    Mp = pl.cdiv(M, rows) * rows
    x3 = x.reshape(M, D // LANES, LANES)
    pos = jnp.pad(pos, ((0, 0), (0, Mp - M)))
    xs0 = jnp.zeros((P, D // LANES, LANES), x.dtype)
    out = pl.pallas_call(
        functools.partial(_scatter_rows_kernel, rows=rows, M=M),
        grid_spec=pltpu.PrefetchScalarGridSpec(
            num_scalar_prefetch=1,
            grid=(Mp // rows,),
            in_specs=[pl.BlockSpec((rows, D // LANES, LANES), lambda i, p: (i, 0, 0)),
                      pl.BlockSpec(memory_space=pl.ANY)],
            out_specs=pl.BlockSpec(memory_space=pl.ANY),
            scratch_shapes=[pltpu.SemaphoreType.DMA(())],
        ),
        out_shape=jax.ShapeDtypeStruct(xs0.shape, x.dtype),
        input_output_aliases={2: 0},
        compiler_params=_cparams(1),
        name="scatter_rows",
    )(pos, x3, xs0)
    return out.reshape(P, D)


def _moe_up_kernel(te_ref, tf_ref, tv_ref, x_ref, w1_ref, w3_ref, o_ref, w1b_ref, w3b_ref):
    t = pl.program_id(1)

    @pl.when(tf_ref[t] == 1)
    def _():
        w1b_ref[...] = w1_ref[...].astype(BF16)
        w3b_ref[...] = w3_ref[...].astype(BF16)

    @pl.when(tv_ref[t] == 1)
    def _():
        x = x_ref[...]
        a = jnp.dot(x, w1b_ref[...], preferred_element_type=F32)
        b = jnp.dot(x, w3b_ref[...], preferred_element_type=F32)
        o_ref[...] = (a * _sigmoid(a) * b).astype(o_ref.dtype)

    @pl.when(tv_ref[t] == 0)
    def _():
        o_ref[...] = jnp.zeros(o_ref.shape, o_ref.dtype)


def moe_up(xs, w1, w3, layer, tile_expert, tile_first, tile_valid, *, tn):
    P, K = xs.shape
    N = w1.shape[3]
    tm = MOE_TM
    nt = P // tm
    wspec = pl.BlockSpec((None, None, K, tn), lambda n, t, te, tf, tv: (layer, te[t], 0, n))
    return pl.pallas_call(
        _moe_up_kernel,
        grid_spec=pltpu.PrefetchScalarGridSpec(
            num_scalar_prefetch=3,
            grid=(N // tn, nt),
            in_specs=[pl.BlockSpec((tm, K), lambda n, t, te, tf, tv: (t, 0)), wspec, wspec],
            out_specs=pl.BlockSpec((tm, tn), lambda n, t, te, tf, tv: (t, n)),
            scratch_shapes=[pltpu.VMEM((K, tn), BF16), pltpu.VMEM((K, tn), BF16)],
        ),
        out_shape=jax.ShapeDtypeStruct((P, N), BF16),
        compiler_params=_cparams(2),
        name="moe_up",
    )(tile_expert, tile_first, tile_valid, xs, w1, w3)


def _moe_down_kernel(te_ref, tf_ref, tv_ref, h_ref, w_ref, o_ref, wb_ref, *, sub):
    t = pl.program_id(1)
    tt = t // sub

    @pl.when((tf_ref[tt] == 1) & (t % sub == 0))
    def _():
        wb_ref[...] = w_ref[...].astype(BF16)

    @pl.when(tv_ref[tt] == 1)
    def _():
        o_ref[...] = jnp.dot(h_ref[...], wb_ref[...], preferred_element_type=F32)

    @pl.when(tv_ref[tt] == 0)
    def _():
        o_ref[...] = jnp.zeros(o_ref.shape, o_ref.dtype)


def moe_down(h, w2, layer, tile_expert, tile_first, tile_valid, *, tm, tn):
    P, K = h.shape
    N = w2.shape[3]
    assert MOE_TM % tm == 0
    sub = MOE_TM // tm
    return pl.pallas_call(
        functools.partial(_moe_down_kernel, sub=sub),
        grid_spec=pltpu.PrefetchScalarGridSpec(
            num_scalar_prefetch=3,
            grid=(N // tn, P // tm),
            in_specs=[pl.BlockSpec((tm, K), lambda n, t, te, tf, tv: (t, 0)),
                      pl.BlockSpec((None, None, K, tn), lambda n, t, te, tf, tv: (layer, te[t // sub], 0, n))],
            out_specs=pl.BlockSpec((tm, tn), lambda n, t, te, tf, tv: (t, n)),
            scratch_shapes=[pltpu.VMEM((K, tn), BF16)],
        ),
        out_shape=jax.ShapeDtypeStruct((P, N), F32),
        compiler_params=_cparams(2),
        name="moe_down",
    )(tile_expert, tile_first, tile_valid, h, w2)


def _moe_combine_kernel(pos_ref, gate_ref, e_hbm, o_ref, buf_ref, sem, *, rows):
    base = pl.program_id(0) * rows

    def row_copy(kk, r, src_row):
        return pltpu.make_async_copy(e_hbm.at[src_row], buf_ref.at[kk, r], sem)

    def issue(r, c):
        for kk in range(TOP_K):
            row_copy(kk, r, pos_ref[kk, base + r]).start()
        return c

    lax.fori_loop(0, rows, issue, 0)

    def drain(r, c):
        for kk in range(TOP_K):
            row_copy(kk, r, 0).wait()
        return c

    lax.fori_loop(0, rows, drain, 0)
    o_ref[...] = gate_ref[:, 0] * buf_ref[0] + gate_ref[:, 1] * buf_ref[1]


def moe_combine(eout, pos, gate, *, rows=128):
    M = gate.shape[0]
    P, D = eout.shape
    Mp = pl.cdiv(M, rows) * rows
    e3 = eout.reshape(P, D // LANES, LANES)
    pos = jnp.pad(pos, ((0, 0), (0, Mp - M)))
    out = pl.pallas_call(
        functools.partial(_moe_combine_kernel, rows=rows),
        grid_spec=pltpu.PrefetchScalarGridSpec(
            num_scalar_prefetch=1,
            grid=(Mp // rows,),
            in_specs=[pl.BlockSpec((rows, TOP_K, 1, 1), lambda i, p: (i, 0, 0, 0)),
                      pl.BlockSpec(memory_space=pl.ANY)],
            out_specs=pl.BlockSpec((rows, D // LANES, LANES), lambda i, p: (i, 0, 0)),
            scratch_shapes=[pltpu.VMEM((TOP_K, rows, D // LANES, LANES), F32),
                            pltpu.SemaphoreType.DMA(())],
        ),
        out_shape=jax.ShapeDtypeStruct((M, D // LANES, LANES), F32),
        compiler_params=_cparams(1),
        name="moe_combine",
    )(pos, gate.reshape(M, TOP_K, 1, 1), e3)
    return out.reshape(M, D)


def _routing_tables(idx, tm):
    M = idx.shape[1]
    NP = TOP_K * M
    nb = pl.cdiv(NP, LANES)
    e_flat = jnp.pad(idx.reshape(-1), (0, nb * LANES - NP), constant_values=N_EXPERTS)
    oh = (e_flat[None, :] == jnp.arange(N_EXPERTS)[:, None]).astype(F32).reshape(N_EXPERTS, nb, LANES)
    tri = jnp.asarray(np.triu(np.ones((LANES, LANES), np.float32)))
    within = jnp.einsum("ebj,ji->ebi", oh, tri, precision=lax.Precision.HIGHEST)
    blk_tot = within[..., -1]
    blk_pre = jnp.cumsum(blk_tot, axis=1) - blk_tot
    counts = (blk_pre[:, -1] + blk_tot[:, -1]).astype(jnp.int32)
    tiles_per = (counts + tm - 1) // tm
    tile_start = jnp.cumsum(tiles_per) - tiles_per
    rank = within + blk_pre[..., None] - 1.0
    pos = jnp.sum(oh * (rank + (tile_start * tm).astype(F32)[:, None, None]), axis=0)
    pos = pos.reshape(-1)[:NP].astype(jnp.int32).reshape(TOP_K, M)
    n_tiles = NP // tm + N_EXPERTS
    t_io = jnp.arange(n_tiles)
    used = jnp.sum(tiles_per)
    tile_expert = jnp.clip(jnp.sum(t_io[:, None] >= tile_start[None, :], axis=1) - 1, 0, N_EXPERTS - 1)
    tile_valid = (t_io < used).astype(jnp.int32)
    last_used_expert = jnp.max(jnp.where(tile_valid == 1, tile_expert, 0))
    tile_expert = jnp.where(tile_valid == 1, tile_expert, last_used_expert).astype(jnp.int32)
    prev = jnp.concatenate([jnp.array([-1], jnp.int32), tile_expert[:-1]])
    tile_first = (tile_expert != prev).astype(jnp.int32)
    return pos, tile_expert, tile_first, tile_valid, n_tiles * tm


def moe_ffn(x, xb, router_w_l, w1, w3, w2, layer):
    idx, gate = router(x, router_w_l)
    pos, tile_expert, tile_first, tile_valid, P = _routing_tables(idx, MOE_TM)
    xs = scatter_rows(xb, pos, P)
    h = moe_up(xs, w1, w3, layer, tile_expert, tile_first, tile_valid, tn=1024)
    eout = moe_down(h, w2, layer, tile_expert, tile_first, tile_valid, tm=256, tn=512)
    return moe_combine(eout, pos, gate.T)


def kernel(x_prompt, x_sample, cache_k, cache_v, page_table, state_hgrn, rel_bias, w_in, w_pa, w_pb, w_out,
           lam_qk, subln_g, hgrn_g, lb_raw, ln_g, ln_b, ffn_w1, ffn_w3, ffn_w2, router_w, moe_w1, moe_w3, moe_w2):
    B, S, D = x_prompt.shape
    DB = x_sample.shape[0]
    MP = B * S
    lbs = jax.nn.softmax(lb_raw.astype(F32), axis=0)
    lbs = jnp.cumsum(lbs, axis=0) - lbs[0]
    bias_tiles = _bias_tiles_t(rel_bias, ATT_T)
    ck = cache_k.reshape(cache_k.shape[0], cache_k.shape[1], PAGE_ROWS, 2 * DK_A)
    cv = cache_v.reshape(cache_v.shape[0], cache_v.shape[1], PAGE_ROWS, DV_A)

    x = jnp.concatenate([x_prompt.reshape(MP, D), x_sample.reshape(DB, D)], axis=0)
    xb = x.astype(BF16)
    outs = {n: [] for n in ("kp", "vp", "sp", "ks", "vs", "ss")}
    for l in range(DEPTH):
        lam_init = 0.8 - 0.6 * math.exp(-0.3 * l)
        lq = lam_qk[l].astype(F32)
        lam = jnp.exp(jnp.sum(lq[0] * lq[1])) - jnp.exp(jnp.sum(lq[2] * lq[3])) + lam_init

        z = matmul(xb, w_in, l, tm=1024, tn=1024)
        zs = z[MP:]
        ya = prompt_attention(z, bias_tiles, lam, subln_g[l], lam_init, B, S)
        yb, s_p = hgrn_prompt(z, lbs[l], hgrn_g[l], B, S)
        head = lambda n: zs[:, COL[n]:COL[n] + W_A].reshape(DB, H_A, DV_A)
        q_s, k_s, v_s = head("q_a"), head("k_a"), head("v_a")
        ya_s = sample_attention(q_s, k_s, v_s, ck, cv, page_table, rel_bias, lam, subln_g[l], lam_init, l)
        colv = lambda n: zs[:, COL[n]:COL[n] + W_BK].reshape(DB, H_B, DK_B, 1)
        rowv = lambda n: zs[:, COL[n]:COL[n] + W_B].reshape(DB, H_B, 1, DV_B)
        yb_s, s_s = hgrn_step(colv("f_b"), colv("q_b"), rowv("i_b"), rowv("og_b"), lbs[l], hgrn_g[l],
                              state_hgrn[l])
        mg = branch_merge(ya, yb, ya_s.reshape(DB, W_A), yb_s.reshape(DB, W_B), z, w_pa, w_pb, l,
                          tm=1024, tn=512)
        mo = matmul(mg, w_out, l, tm=1024, tn=512)
        x, xb = residual_ln(x, mo, ln_g[l, 0], ln_b[l, 0])

        j = l // 2
        if l % 2 == 0:
            hh = swiglu_up(xb, ffn_w1, ffn_w3, j, tm=1024, tn=512)
            c = matmul(hh, ffn_w2, j, tm=512, tn=512)
        else:
            c = moe_ffn(x, xb, router_w[j], moe_w1, moe_w3, moe_w2, j)
        x, xb = residual_ln(x, c, ln_g[l, 1], ln_b[l, 1])

        outs["kp"].append(z[:MP, COL["k_a"]:COL["k_a"] + W_A].reshape(B, S, H_A, 2 * DK_A))
        outs["vp"].append(z[:MP, COL["v_a"]:COL["v_a"] + W_A].reshape(B, S, H_A, DV_A))
        outs["sp"].append(s_p)
        outs["ks"].append(k_s.reshape(DB, 1, H_A, 2 * DK_A))
        outs["vs"].append(v_s.reshape(DB, 1, H_A, DV_A))
        outs["ss"].append(s_s)
    return (x[:MP].reshape(B, S, D), x[MP:].reshape(DB, 1, D),
            jnp.stack(outs["kp"]), jnp.stack(outs["vp"]), jnp.stack(outs["sp"]),
            jnp.stack(outs["ks"]), jnp.stack(outs["vs"]), jnp.stack(outs["ss"]))
```

```python
import functools
import math

import numpy as np
import jax
import jax.numpy as jnp
from jax import lax
from jax.experimental import pallas as pl
from jax.experimental.pallas import tpu as pltpu

F32 = jnp.float32
BF16 = jnp.bfloat16

D_MODEL = 2048
DEPTH = 2
PAGE_SIZE = 128
H_A = 8
DK_A = 64
DV_A = 2 * DK_A
W_A = H_A * DV_A
H_B = 8
DK_B = 128
DV_B = 128
W_BK = H_B * DK_B
W_B = H_B * DV_B
CHUNK = 64
N_BUCKETS = 32
MAX_EXACT = N_BUCKETS // 2
MAX_DISTANCE = 128
N_EXPERTS = 8
TOP_K = 2
ALPHA = (2 * DEPTH) ** 0.25
LN_EPS = 1e-5
RMS_EPS = 1e-6
F_MIN = 1e-20
NEG_INF = -1e30
SPLITS = [H_A * 2 * DK_A, H_A * 2 * DK_A, W_A, W_BK, W_BK, W_B, W_B, D_MODEL, D_MODEL]
N_IN = sum(SPLITS)
_NAMES = ["q_a", "k_a", "v_a", "f_b", "q_b", "i_b", "og_b", "gt_a", "gt_b"]
COL = {}
COLZ = {}
_acc = _accz = 0
for _name, _w in zip(_NAMES, SPLITS):
    COL[_name] = _acc
    _acc += _w
    if _name not in ("k_a", "v_a"):
        COLZ[_name] = _accz
        _accz += _w
N_Z = _accz

LANES = 128
SUBLANES = 8
V7X_VMEM_BYTES = 64 * 1024 * 1024
VMEM_LIMIT = V7X_VMEM_BYTES - 4 * 1024 * 1024

ATT_T = 256
ATT_HALF = LANES
ATT_HEADS = 2
MOE_TM = 512
MOE_ROW_STEP = 128
PAGES_PER_STEP = 8
HGRN_UNROLL = 4


def _cparams(n_axes):
    return pltpu.CompilerParams(
        dimension_semantics=("arbitrary",) * n_axes, vmem_limit_bytes=VMEM_LIMIT)


def _sigmoid(x):
    return jax.nn.sigmoid(x)


def _dot_nt(a, b):
    return lax.dot_general(a, b, (((1,), (1,)), ((), ())), preferred_element_type=F32)


def _dot_tn(a, b):
    return lax.dot_general(a, b, (((0,), (0,)), ((), ())), preferred_element_type=F32)


def _for_rows(m, M, tm, fn):
    n_full, tail = M // tm, M % tm
    if n_full:
        @pl.when(m < n_full)
        def _():
            fn(slice(None))
    if tail:
        @pl.when(m == n_full)
        def _():
            fn(slice(0, tail))


def _mm_kernel(x_ref, w_ref, o_ref, wb_ref, *, M, tm):
    m = pl.program_id(1)

    @pl.when(m == 0)
    def _():
        wb_ref[...] = w_ref[...].astype(BF16)

    def body(rs):
        o_ref[rs, :] = jnp.dot(x_ref[rs, :], wb_ref[...], preferred_element_type=F32).astype(o_ref.dtype)

    _for_rows(m, M, tm, body)


def matmul(x, w, layer, *, tm, tn, out_dtype=F32):
    M, K = x.shape
    N = w.shape[2]
    assert N % tn == 0
    return pl.pallas_call(
        functools.partial(_mm_kernel, M=M, tm=tm),
        grid=(N // tn, pl.cdiv(M, tm)),
        in_specs=[
            pl.BlockSpec((tm, K), lambda n, m: (m, 0)),
            pl.BlockSpec((None, K, tn), lambda n, m: (layer, 0, n)),
        ],
        out_specs=pl.BlockSpec((tm, tn), lambda n, m: (m, n)),
        out_shape=jax.ShapeDtypeStruct((M, N), out_dtype),
        scratch_shapes=[pltpu.VMEM((K, tn), BF16)],
        compiler_params=_cparams(2),
        name="matmul",
    )(x, w)


def _in_proj_kernel(x_ref, w_ref, kp_in, vp_in, ks_in, vs_in, z_ref, kp_ref, vp_ref, ks_ref, vs_ref, wb_ref, *,
                    n_full, tail, nk, nv):
    del kp_in, vp_in, ks_in, vs_in
    n = pl.program_id(0)
    m = pl.program_id(1)

    @pl.when(m == 0)
    def _():
        wb_ref[...] = w_ref[...].astype(BF16)

    def emit(cond, full_ref, tail_ref, tail_rows):
        @pl.when(cond & (m < n_full))
        def _():
            full_ref[...] = jnp.dot(x_ref[...], wb_ref[...], preferred_element_type=F32)

        @pl.when(cond & (m == n_full))
        def _():
            tail_ref[tail_rows, :] = jnp.dot(x_ref[0:tail, :], wb_ref[...], preferred_element_type=F32)

    emit((n != nk) & (n != nv), z_ref, z_ref, slice(0, tail))
    emit(n == nk, kp_ref, ks_ref, slice(None))
    emit(n == nv, vp_ref, vs_ref, slice(None))


def in_proj(x, w_in, layer, kp, vp, ks, vs, *, tm, tn):
    M, K = x.shape
    MP, DB = kp.shape[1], ks.shape[1]
    assert tn == W_A and M == MP + DB and MP % tm == 0 and DB < tm and N_IN % tn == 0
    n_full = MP // tm
    nk, nv = COL["k_a"] // tn, COL["v_a"] // tn
    assert nv == nk + 1 and COL["q_a"] == 0

    def z_idx(n, m):
        hold = (n == nk) | (n == nv)
        return (jnp.where(hold, n_full, m), jnp.where(n < nk, n, jnp.where(hold, nk - 1, n - 2)))

    def p_idx(n_own):
        def idx(n, m):
            return (layer, jnp.where(n < n_own, 0, jnp.where(n == n_own, jnp.minimum(m, n_full - 1), n_full - 1)), 0)
        return idx

    s_idx = lambda n, m: (layer, 0, 0)
    anyspec = pl.BlockSpec(memory_space=pl.ANY)
    return pl.pallas_call(
        functools.partial(_in_proj_kernel, n_full=n_full, tail=DB, nk=nk, nv=nv),
        grid=(N_IN // tn, n_full + 1),
        in_specs=[
            pl.BlockSpec((tm, K), lambda n, m: (m, 0)),
            pl.BlockSpec((None, K, tn), lambda n, m: (layer, 0, n)),
            anyspec, anyspec, anyspec, anyspec,
        ],
        out_specs=[
            pl.BlockSpec((tm, tn), z_idx),
            pl.BlockSpec((None, tm, tn), p_idx(nk)),
            pl.BlockSpec((None, tm, tn), p_idx(nv)),
            pl.BlockSpec((None, DB, tn), s_idx),
            pl.BlockSpec((None, DB, tn), s_idx),
        ],
        out_shape=[jax.ShapeDtypeStruct((M, N_Z), F32)] + [jax.ShapeDtypeStruct(a.shape, a.dtype)
                                                            for a in (kp, vp, ks, vs)],
        input_output_aliases={2: 1, 3: 2, 4: 3, 5: 4},
        scratch_shapes=[pltpu.VMEM((K, tn), BF16)],
        compiler_params=_cparams(2),
        name="in_proj",
    )(x, w_in, kp, vp, ks, vs)


def _swiglu_kernel(x_ref, w1_ref, w3_ref, o_ref, w1b_ref, w3b_ref, *, M, tm):
    m = pl.program_id(1)

    @pl.when(m == 0)
    def _():
        w1b_ref[...] = w1_ref[...].astype(BF16)
        w3b_ref[...] = w3_ref[...].astype(BF16)

    def body(rs):
        x = x_ref[rs, :]
        a = jnp.dot(x, w1b_ref[...], preferred_element_type=F32)
        b = jnp.dot(x, w3b_ref[...], preferred_element_type=F32)
        o_ref[rs, :] = (a * _sigmoid(a) * b).astype(o_ref.dtype)

    _for_rows(m, M, tm, body)


def swiglu_up(x, w1, w3, layer, *, tm, tn):
    M, K = x.shape
    N = w1.shape[2]
    assert N % tn == 0
    return pl.pallas_call(
        functools.partial(_swiglu_kernel, M=M, tm=tm),
        grid=(N // tn, pl.cdiv(M, tm)),
        in_specs=[
            pl.BlockSpec((tm, K), lambda n, m: (m, 0)),
            pl.BlockSpec((None, K, tn), lambda n, m: (layer, 0, n)),
            pl.BlockSpec((None, K, tn), lambda n, m: (layer, 0, n)),
        ],
        out_specs=pl.BlockSpec((tm, tn), lambda n, m: (m, n)),
        out_shape=jax.ShapeDtypeStruct((M, N), BF16),
        scratch_shapes=[pltpu.VMEM((K, tn), BF16), pltpu.VMEM((K, tn), BF16)],
        compiler_params=_cparams(2),
        name="swiglu_up",
    )(x, w1, w3)


def _merge_kernel(ya_ref, yb_ref, yas_ref, ybs_ref, wa_ref, wb_ref, ga_ref, gb_ref, o_ref, wab_ref, wbb_ref, *,
                  n_full, tail):
    m = pl.program_id(1)

    @pl.when(m == 0)
    def _():
        wab_ref[...] = wa_ref[...].astype(BF16)
        wbb_ref[...] = wb_ref[...].astype(BF16)

    def merge(ya, yb, rs):
        pa = jnp.dot(ya, wab_ref[...], preferred_element_type=F32)
        pb = jnp.dot(yb, wbb_ref[...], preferred_element_type=F32)
        o_ref[rs, :] = (_sigmoid(ga_ref[rs, :]) * pa + _sigmoid(gb_ref[rs, :]) * pb).astype(o_ref.dtype)

    @pl.when(m < n_full)
    def _():
        merge(ya_ref[...], yb_ref[...], slice(None))

    @pl.when(m == n_full)
    def _():
        merge(yas_ref[...], ybs_ref[...], slice(0, tail))


def branch_merge(ya, yb, ya_s, yb_s, z, w_pa, w_pb, layer, *, tm, tn):
    MP, K = ya.shape
    DB = ya_s.shape[0]
    M = z.shape[0]
    N = w_pa.shape[2]
    assert N % tn == 0 and MP % tm == 0 and M == MP + DB and DB < tm
    n_full = MP // tm
    ga0, gb0 = COLZ["gt_a"] // tn, COLZ["gt_b"] // tn
    prow = pl.BlockSpec((tm, K), lambda n, m: (jnp.minimum(m, n_full - 1), 0))
    srow = pl.BlockSpec((DB, K), lambda n, m: (0, 0))
    return pl.pallas_call(
        functools.partial(_merge_kernel, n_full=n_full, tail=DB),
        grid=(N // tn, n_full + 1),
        in_specs=[
            prow, prow, srow, srow,
            pl.BlockSpec((None, K, tn), lambda n, m: (layer, 0, n)),
            pl.BlockSpec((None, K, tn), lambda n, m: (layer, 0, n)),
            pl.BlockSpec((tm, tn), lambda n, m: (m, ga0 + n)),
            pl.BlockSpec((tm, tn), lambda n, m: (m, gb0 + n)),
        ],
        out_specs=pl.BlockSpec((tm, tn), lambda n, m: (m, n)),
        out_shape=jax.ShapeDtypeStruct((M, N), BF16),
        scratch_shapes=[pltpu.VMEM((K, tn), BF16), pltpu.VMEM((K, tn), BF16)],
        compiler_params=_cparams(2),
        name="branch_merge",
    )(ya, yb, ya_s, yb_s, w_pa, w_pb, z, z)


def _add_ln(x, y, g, b):
    h = ALPHA * x + y
    mu = jnp.mean(h, axis=-1, keepdims=True)
    c = h - mu
    var = jnp.mean(c * c, axis=-1, keepdims=True)
    return c * lax.rsqrt(var + LN_EPS) * g + b


def _res_ln_kernel(*refs, n_full, tail, x_split, out_split):
    refs = list(refs)
    xp_ref = refs.pop(0)
    xs_ref = refs.pop(0) if x_split else None
    y_ref, g_ref, b_ref, o1_ref, o2_ref = refs
    m = pl.program_id(0)

    @pl.when(m < n_full)
    def _():
        o = _add_ln(xp_ref[...], y_ref[...], g_ref[...], b_ref[...])
        o1_ref[...] = o
        if not out_split:
            o2_ref[...] = o.astype(BF16)

    @pl.when(m == n_full)
    def _():
        xs = xs_ref[...] if x_split else xp_ref[0:tail, :]
        o = _add_ln(xs, y_ref[0:tail, :], g_ref[...], b_ref[...])
        if out_split:
            o2_ref[...] = o
        else:
            o1_ref[0:tail, :] = o
            o2_ref[0:tail, :] = o.astype(BF16)


def residual_ln(x, y, g, b, MP, *, tm=256, out_split=False):
    x_split = isinstance(x, tuple)
    M, D = y.shape
    DB = M - MP
    assert MP % tm == 0 and 0 < DB < tm
    n_full = MP // tm
    row = pl.BlockSpec((tm, D), lambda m: (m, 0))
    prow = pl.BlockSpec((tm, D), lambda m: (jnp.minimum(m, n_full - 1), 0))
    srow = pl.BlockSpec((DB, D), lambda m: (0, 0))
    vec = pl.BlockSpec((1, D), lambda m: (0, 0))
    x_args = list(x) if x_split else [x]
    x_specs = [prow, srow] if x_split else [row]
    if out_split:
        out_specs = [prow, srow]
        out_shape = [jax.ShapeDtypeStruct((MP, D), F32), jax.ShapeDtypeStruct((DB, D), F32)]
    else:
        out_specs = [row, row]
        out_shape = [jax.ShapeDtypeStruct((M, D), F32), jax.ShapeDtypeStruct((M, D), BF16)]
    return pl.pallas_call(
        functools.partial(_res_ln_kernel, n_full=n_full, tail=DB, x_split=x_split, out_split=out_split),
        grid=(n_full + 1,),
        in_specs=x_specs + [row, vec, vec],
        out_specs=out_specs,
        out_shape=out_shape,
        compiler_params=_cparams(1),
        name="residual_ln",
    )(*x_args, y, g.reshape(1, D), b.reshape(1, D))


def _t5_bucket(rel):
    n = jnp.maximum(rel, 0)
    large = MAX_EXACT + (jnp.log(jnp.maximum(n, 1).astype(F32) / MAX_EXACT)
                         / math.log(MAX_DISTANCE / MAX_EXACT) * (N_BUCKETS - MAX_EXACT)).astype(jnp.int32)
    large = jnp.clip(large, 0, N_BUCKETS - 1)
    return jnp.where(n < MAX_EXACT, n, large)


def _bucket_np(n):
    n = np.maximum(n, 0)
    large = MAX_EXACT + (np.log(np.maximum(n, 1) / MAX_EXACT)
                         / math.log(MAX_DISTANCE / MAX_EXACT) * (N_BUCKETS - MAX_EXACT)).astype(np.int64)
    return np.where(n < MAX_EXACT, n, np.clip(large, 0, N_BUCKETS - 1))


assert (_bucket_np(np.arange(min(ATT_T, PAGE_SIZE) + 1, 1 << 16)) == N_BUCKETS - 1).all()


def _bias_tiles_t(rel_bias, T):
    n = jnp.arange(-(T - 1), 3 * T)
    vec = jnp.where((n >= 0)[:, None], rel_bias[_t5_bucket(n)].astype(F32), NEG_INF).T
    L = 2 * T
    tiles = []
    for d in range(3):
        f0 = d * T + (T - 1)
        a = jnp.concatenate([vec[:, f0:f0 + T], vec[:, :1], vec[:, f0 - (T - 1):f0]], axis=1)
        skew = jnp.tile(a, (1, T))[:, :T * (L - 1)].reshape(H_A, T, L - 1)
        tiles.append(skew[:, :, :T])
    return jnp.stack(tiles, axis=1)


def _rms_head(o, g):
    return o * lax.rsqrt(jnp.mean(o * o, axis=-1, keepdims=True) + RMS_EPS) * g


def _prompt_attn_kernel(lam_ref, q_ref, k_ref, v_ref, bias_ref, g_ref, o_ref, kb_ref, vb_ref, s_ref, p_ref, *,
                        T, out_scale):
    qi = pl.program_id(2)
    n_half = T // ATT_HALF

    @pl.when(qi == 0)
    def _():
        kb_ref[...] = k_ref[...].astype(BF16)
        vb_ref[...] = v_ref[...].astype(BF16)

    lane = lax.broadcasted_iota(jnp.int32, (1, 2 * DK_A), 1)
    qm = []
    for hd in range(ATT_HEADS):
        q = q_ref[:, hd * DV_A:(hd + 1) * DV_A] * (DK_A ** -0.5)
        qm.append((jnp.where(lane < DK_A, q, 0.0).astype(BF16), jnp.where(lane >= DK_A, q, 0.0).astype(BF16)))
    blocks = [(hd, mi, hi) for hd in range(ATT_HEADS) for mi in range(2) for hi in range(n_half)]
    nb = len(blocks)

    def kv_rows(kj):
        return pl.ds(pl.multiple_of(kj * T, T), T)

    def head_cols(hd):
        return slice(hd * DV_A, (hd + 1) * DV_A)

    def scores(kj):
        tile = jnp.clip(qi - kj, 0, 2)
        ks = [kb_ref[kv_rows(kj), head_cols(hd)] for hd in range(ATT_HEADS)]
        return [_dot_nt(ks[hd], qm[hd][mi][hi * ATT_HALF:(hi + 1) * ATT_HALF, :])
                + bias_ref[hd, tile, :, hi * ATT_HALF:(hi + 1) * ATT_HALF] for hd, mi, hi in blocks]

    def pv(kj):
        vs = [vb_ref[kv_rows(kj), head_cols(hd)] for hd in range(ATT_HEADS)]
        return [_dot_tn(vs[blocks[i][0]], p_ref[i]) for i in range(nb)]

    for i, s in enumerate(scores(0)):
        s_ref[i] = s
    p_ref[...] = jnp.zeros(p_ref.shape, p_ref.dtype)

    def step(kj, carry):
        s_next = scores(jnp.minimum(kj + 1, qi))
        pv_prev = pv(jnp.maximum(kj - 1, 0))
        out = []
        for i in range(nb):
            m, l, b = carry[i]
            s = s_ref[i]
            m_new = jnp.maximum(m, jnp.max(s, axis=0, keepdims=True))
            p = jnp.exp(s - m_new)
            r = jnp.exp(m - m_new)
            l_new = r * l + jnp.sum(p, axis=0, keepdims=True)
            p_ref[i] = p.astype(BF16)
            out.append((m_new, l_new, r * (b + pv_prev[i])))
        for i in range(nb):
            s_ref[i] = s_next[i]
        return tuple(out)

    init = tuple((jnp.full((1, ATT_HALF), NEG_INF, F32), jnp.zeros((1, ATT_HALF), F32),
                  jnp.zeros((DV_A, ATT_HALF), F32)) for _ in blocks)
    res = lax.fori_loop(0, qi + 1, step, init)
    pv_last = pv(qi)
    lam = lam_ref[0]
    for hd in range(ATT_HEADS):
        for hi in range(n_half):
            i1 = blocks.index((hd, 0, hi))
            i2 = blocks.index((hd, 1, hi))
            ot = ((res[i1][2] + pv_last[i1]) / res[i1][1]
                  - lam * ((res[i2][2] + pv_last[i2]) / res[i2][1]))
            ot = ot * lax.rsqrt(jnp.mean(ot * ot, axis=0, keepdims=True) + RMS_EPS)
            o_ref[hi * ATT_HALF:(hi + 1) * ATT_HALF, head_cols(hd)] = (
                ot.T * g_ref[:, head_cols(hd)] * out_scale).astype(o_ref.dtype)


def prompt_attention(z, k_all, v_all, layer, bias_tiles, lam, subln_g_l, lam_init, B, S):
    T = ATT_T
    nq = S // T
    W = ATT_HEADS * DV_A
    assert COLZ["q_a"] == 0 and H_A % ATT_HEADS == 0
    nblk = 2 * ATT_HEADS * (T // ATT_HALF)
    kern = functools.partial(_prompt_attn_kernel, T=T, out_scale=1.0 - lam_init)
    return pl.pallas_call(
        kern,
        grid=(B, H_A // ATT_HEADS, nq),
        in_specs=[
            pl.BlockSpec(memory_space=pltpu.SMEM),
            pl.BlockSpec((T, W), lambda b, h, i: (b * nq + i, h)),
            pl.BlockSpec((None, S, W), lambda b, h, i: (layer, b, h)),
            pl.BlockSpec((None, S, W), lambda b, h, i: (layer, b, h)),
            pl.BlockSpec((ATT_HEADS, 3, T, T), lambda b, h, i: (h, 0, 0, 0)),
            pl.BlockSpec((1, W), lambda b, h, i: (0, h)),
        ],
        out_specs=pl.BlockSpec((T, W), lambda b, h, i: (b * nq + i, h)),
        out_shape=jax.ShapeDtypeStruct((B * S, W_A), BF16),
        scratch_shapes=[pltpu.VMEM((S, W), BF16), pltpu.VMEM((S, W), BF16),
                        pltpu.VMEM((nblk, T, ATT_HALF), F32),
                        pltpu.VMEM((nblk, T, ATT_HALF), BF16)],
        compiler_params=_cparams(3),
        name="prompt_attention",
    )(lam.reshape(1), z, k_all, v_all, bias_tiles, subln_g_l.reshape(1, W_A))


_LEVELS = [CHUNK >> (i + 1) for i in range(int(math.log2(CHUNK)))]


def _cumsum_and_anchor_matrix():
    t = np.arange(CHUNK)
    tri = (t[None, :] <= t[:, None]).astype(np.float32)
    mats = [tri]
    for w in _LEVELS:
        anchor_row = t - (t % (2 * w)) + w - 1
        mats.append(tri[anchor_row])
    return np.concatenate(mats, axis=0)


def _split3(x):
    x1 = x.astype(BF16)
    r1 = x - x1.astype(F32)
    x2 = r1.astype(BF16)
    x3 = (r1 - x2.astype(F32)).astype(BF16)
    return x1, x2, x3


def _hgrn_gates(fx, qx, lb):
    f = lb + (1.0 - lb) * _sigmoid(fx)
    g = jnp.log(jnp.maximum(f, F_MIN))
    k = (1.0 - lb) * _sigmoid(-fx)
    q = qx * _sigmoid(qx)
    return g, k, q


def _hgrn_prompt_kernel(f_ref, q_ref, i_ref, og_ref, lb_ref, gain_ref, pm_ref, y_ref, s_ref,
                        oi_ref, qe_ref, u_ref, dl_ref, *, S):
    n_chunks = S // CHUNK
    lb = lb_ref[...]
    gain = gain_ref[...]
    pm = pm_ref[...]
    t_io = lax.broadcasted_iota(jnp.int32, (CHUNK, CHUNK), 0)
    s_io = lax.broadcasted_iota(jnp.int32, (CHUNK, CHUNK), 1)
    masks = [(((t_io ^ s_io) < 2 * w) & ((t_io & w) != 0)) & ((s_io & w) == 0) for w in _LEVELS]
    diag = t_io == s_io

    def group(gi, carry):
        cs = [gi * HGRN_UNROLL + u for u in range(HGRN_UNROLL)]
        rows = [pl.ds(pl.multiple_of(c * CHUNK, CHUNK), CHUNK) for c in cs]
        gkq = [_hgrn_gates(f_ref[r, :], q_ref[r, :], lb) for r in rows]
        vbs = [i_ref[r, :].astype(BF16) for r in rows]
        balls = []
        for g, _, _ in gkq:
            g1, g2, g3 = _split3(g)
            balls.append(jnp.dot(pm, g1, preferred_element_type=F32)
                         + jnp.dot(pm, g2, preferred_element_type=F32)
                         + jnp.dot(pm, g3, preferred_element_type=F32))
        bs = [ball[0:CHUNK] for ball in balls]
        a_s = [jnp.where(diag, _dot_nt(q.astype(BF16), k.astype(BF16)), 0.0) for _, k, q in gkq]
        for li in range(len(_LEVELS)):
            for u, (_, k, q) in enumerate(gkq):
                anc = balls[u][(li + 1) * CHUNK:(li + 2) * CHUNK]
                qs = (q * jnp.exp(jnp.minimum(bs[u] - anc, 0.0))).astype(BF16)
                ks = (k * jnp.exp(jnp.minimum(anc - bs[u], 0.0))).astype(BF16)
                a_s[u] = jnp.where(masks[li], _dot_nt(qs, ks), a_s[u])
        for u, (_, k, q) in enumerate(gkq):
            b = bs[u]
            b_last = b[CHUNK - 1:CHUNK, :]
            oi_ref[rows[u], :] = jnp.dot(a_s[u].astype(BF16), vbs[u], preferred_element_type=F32)
            qe_ref[rows[u], :] = (q * jnp.exp(b)).astype(BF16)
            kd = (k * jnp.exp(b_last - b)).astype(BF16)
            u_ref[cs[u]] = _dot_tn(vbs[u], kd)
            dl_ref[cs[u]] = jnp.exp(b_last)
        return carry

    lax.fori_loop(0, n_chunks // HGRN_UNROLL, group, 0)

    def phase_b(c, st):
        rows = pl.ds(pl.multiple_of(c * CHUNK, CHUNK), CHUNK)
        o = oi_ref[rows, :] + _dot_nt(qe_ref[rows, :], st.astype(BF16))
        y = _rms_head(o, gain) * _sigmoid(og_ref[rows, :])
        y_ref[rows, :] = y.astype(y_ref.dtype)
        return st * dl_ref[c] + u_ref[c]

    st = lax.fori_loop(0, n_chunks, phase_b, jnp.zeros((DV_B, DK_B), F32), unroll=HGRN_UNROLL)
    s_ref[...] = st.T


def hgrn_prompt(z, lb_l, hgrn_g_l, B, S):
    assert (S // CHUNK) % HGRN_UNROLL == 0
    fc, qc, ic, oc = (COLZ[n] // DK_B for n in ("f_b", "q_b", "i_b", "og_b"))
    pm = jnp.asarray(_cumsum_and_anchor_matrix(), BF16)
    kern = functools.partial(_hgrn_prompt_kernel, S=S)
    blk = lambda c0: pl.BlockSpec((S, DK_B), lambda b, h: (b, c0 + h))
    n_chunks = S // CHUNK
    return pl.pallas_call(
        kern,
        grid=(B, H_B),
        in_specs=[
            blk(fc), blk(qc), blk(ic), blk(oc),
            pl.BlockSpec((None, 1, DK_B), lambda b, h: (h, 0, 0)),
            pl.BlockSpec((None, 1, DV_B), lambda b, h: (h, 0, 0)),
            pl.BlockSpec(pm.shape, lambda b, h: (0, 0)),
        ],
        out_specs=[
            pl.BlockSpec((S, DV_B), lambda b, h: (b, h)),
            pl.BlockSpec((None, None, DK_B, DV_B), lambda b, h: (b, h, 0, 0)),
        ],
        out_shape=[
            jax.ShapeDtypeStruct((B * S, W_B), BF16),
            jax.ShapeDtypeStruct((B, H_B, DK_B, DV_B), F32),
        ],
        scratch_shapes=[pltpu.VMEM((S, DV_B), F32), pltpu.VMEM((S, DK_B), BF16),
                        pltpu.VMEM((n_chunks, DV_B, DK_B), F32), pltpu.VMEM((n_chunks, 1, DK_B), F32)],
        compiler_params=_cparams(2),
        name="hgrn_prompt",
    )(z, z, z, z, lb_l.reshape(H_B, 1, DK_B), hgrn_g_l.reshape(H_B, 1, DV_B), pm)


PAGE_ROWS = PAGE_SIZE * H_A
assert H_A & (H_A - 1) == 0


def _paged_attn_kernel(pt_ref, lam_ref, q_ref, kn_ref, vn_ref, bfar_ref, blast_ref, bnew_ref, g_ref, *rest,
                       n_groups, out_scale):
    G = PAGES_PER_STEP
    R = 2 * H_A
    k_refs = rest[:G]
    v_refs = rest[G:2 * G]
    o_ref = rest[2 * G]
    m_ref, l_ref, a_ref = rest[2 * G + 1:]
    j = pl.program_id(1)
    lane = lax.broadcasted_iota(jnp.int32, (1, 2 * DK_A), 1)
    q = q_ref[...] * (DK_A ** -0.5)
    qf = jnp.concatenate([jnp.where(lane < DK_A, q, 0.0), jnp.where(lane >= DK_A, q, 0.0)], axis=0)
    qb = qf.astype(BF16)
    col = lax.broadcasted_iota(jnp.int32, (R, PAGE_ROWS), 1)
    row = lax.broadcasted_iota(jnp.int32, (R, PAGE_ROWS), 0)
    valid = (col & (H_A - 1)) == (row & (H_A - 1))

    @pl.when(j == 0)
    def _():
        m_ref[...] = jnp.full(m_ref.shape, NEG_INF, F32)
        l_ref[...] = jnp.zeros(l_ref.shape, F32)
        a_ref[...] = jnp.zeros(a_ref.shape, F32)

    def scores(i, bias):
        s = _dot_nt(qb, k_refs[i][...].astype(BF16)) + bias
        return jnp.where(valid, s, NEG_INF)

    def absorb(s_list, new_token):
        m_old = m_ref[...]
        m_new = m_old
        for s in s_list:
            m_new = jnp.maximum(m_new, jnp.max(s, axis=-1, keepdims=True))
        if new_token:
            kn2 = jnp.concatenate([kn_ref[...], kn_ref[...]], axis=0)
            vn2 = jnp.concatenate([vn_ref[...], vn_ref[...]], axis=0)
            s_new = jnp.sum(qf * kn2, axis=-1, keepdims=True) + bnew_ref[...]
            m_new = jnp.maximum(m_new, s_new)
        r = jnp.exp(m_old - m_new)
        l = r * l_ref[...]
        acc = r * a_ref[...]
        for i, s in enumerate(s_list):
            p = jnp.exp(s - m_new)
            l = l + jnp.sum(p, axis=-1, keepdims=True)
            acc = acc + jnp.dot(p.astype(BF16), v_refs[i][...].astype(BF16), preferred_element_type=F32)
        if new_token:
            p_new = jnp.exp(s_new - m_new)
            l = l + p_new
            acc = acc + p_new * vn2
        m_ref[...] = m_new
        l_ref[...] = l
        a_ref[...] = acc

    far = bfar_ref[...]

    @pl.when(j < n_groups - 1)
    def _():
        absorb([scores(i, far) for i in range(G)], False)

    @pl.when(j == n_groups - 1)
    def _():
        absorb([scores(i, far) for i in range(G - 1)] + [scores(G - 1, blast_ref[...])], True)
        lam = lam_ref[0]
        o = a_ref[0:H_A] / l_ref[0:H_A] - lam * (a_ref[H_A:R] / l_ref[H_A:R])
        o_ref[...] = (_rms_head(o, g_ref[...]) * out_scale).astype(o_ref.dtype)


def sample_attention(q, k_new, v_new, cache_k, cache_v, page_table, rel_bias, lam, subln_g_l,
                     lam_init, layer):
    DB = q.shape[0]
    n_pages = page_table.shape[1]
    G = PAGES_PER_STEP
    R = 2 * H_A
    assert n_pages % G == 0
    n_groups = n_pages // G
    hmap = jnp.tile(jnp.arange(H_A), 2)
    bias_of = lambda dist: rel_bias[_t5_bucket(dist)].astype(F32)
    bfar = bias_of(jnp.array(2 * PAGE_SIZE))[hmap][:, None]
    bnew = bias_of(jnp.array(0))[hmap][:, None]
    blast = jnp.broadcast_to(bias_of(PAGE_SIZE - jnp.arange(PAGE_SIZE)).reshape(1, PAGE_ROWS), (R, PAGE_ROWS))
    kern = functools.partial(_paged_attn_kernel, n_groups=n_groups, out_scale=1.0 - lam_init)

    def page_spec(i):
        return pl.BlockSpec((None, None, PAGE_ROWS, DV_A),
                            lambda b, j, pt: (layer, pt[b * n_pages + j * G + i], 0, 0))

    tok = pl.BlockSpec((None, H_A, DV_A), lambda b, j, pt: (b, 0, 0))
    const = lambda shape: pl.BlockSpec(shape, lambda b, j, pt: (0,) * len(shape))
    grid_spec = pltpu.PrefetchScalarGridSpec(
        num_scalar_prefetch=1,
        grid=(DB, n_groups),
        in_specs=[
            pl.BlockSpec(memory_space=pltpu.SMEM),
            tok, tok, tok,
            const((R, 1)), const((R, PAGE_ROWS)), const((R, 1)), const((H_A, DV_A)),
        ] + [page_spec(i) for i in range(G)] + [page_spec(i) for i in range(G)],
        out_specs=tok,
        scratch_shapes=[pltpu.VMEM((R, 1), F32), pltpu.VMEM((R, 1), F32), pltpu.VMEM((R, DV_A), F32)],
    )
    return pl.pallas_call(
        kern,
        grid_spec=grid_spec,
        out_shape=jax.ShapeDtypeStruct((DB, H_A, DV_A), BF16),
        compiler_params=_cparams(2),
        name="sample_attention",
    )(page_table.reshape(-1), lam.reshape(1), q, k_new, v_new, bfar, blast, bnew,
      subln_g_l.reshape(H_A, DV_A), *([cache_k] * G), *([cache_v] * G))


def _hgrn_step_kernel(fx_ref, qx_ref, v_ref, og_ref, lb_ref, gain_ref, s_ref, y_ref, so_ref):
    g, k, q = _hgrn_gates(fx_ref[...], qx_ref[...], lb_ref[...])
    v = v_ref[...]
    s_new = jnp.exp(g) * s_ref[...].astype(F32) + k * v
    so_ref[...] = s_new.astype(so_ref.dtype)
    o = jnp.sum(q * s_new, axis=1, keepdims=True)
    y = _rms_head(o, gain_ref[...]) * _sigmoid(og_ref[...])
    y_ref[...] = y.astype(y_ref.dtype)


def hgrn_step(fx, qx, v, og, lb_l, hgrn_g_l, state, layer):
    DB = fx.shape[0]
    col = pl.BlockSpec((None, H_B, DK_B, 1), lambda b: (b, 0, 0, 0))
    row = pl.BlockSpec((None, H_B, 1, DV_B), lambda b: (b, 0, 0, 0))
    st = pl.BlockSpec((None, H_B, DK_B, DV_B), lambda b: (b, 0, 0, 0))
    return pl.pallas_call(
        _hgrn_step_kernel,
        grid=(DB,),
        in_specs=[col, col, row, row,
                  pl.BlockSpec((H_B, DK_B, 1), lambda b: (0, 0, 0)),
                  pl.BlockSpec((H_B, 1, DV_B), lambda b: (0, 0, 0)),
                  pl.BlockSpec((None, None, H_B, DK_B, DV_B), lambda b: (layer, b, 0, 0, 0))],
        out_specs=[row, st],
        out_shape=[jax.ShapeDtypeStruct((DB, H_B, 1, DV_B), BF16),
                   jax.ShapeDtypeStruct(state.shape[1:], state.dtype)],
        compiler_params=_cparams(1),
        name="hgrn_step",
    )(fx, qx, v, og, lb_l.reshape(H_B, DK_B, 1), hgrn_g_l.reshape(H_B, 1, DV_B), state)


def _router_kernel(x_ref, w_ref, idx_ref, gate_ref):
    x1, x2, _ = _split3(x_ref[...])
    w1, w2, _ = _split3(w_ref[...])
    lt = _dot_nt(w1, x1) + (_dot_nt(w1, x2) + _dot_nt(w2, x1))
    e_io = lax.broadcasted_iota(jnp.int32, lt.shape, 0)
    m1 = jnp.max(lt, axis=0, keepdims=True)
    i1 = jnp.min(jnp.where(lt == m1, e_io, N_EXPERTS), axis=0, keepdims=True)
    lt2 = jnp.where(e_io == i1, -jnp.inf, lt)
    m2 = jnp.max(lt2, axis=0, keepdims=True)
    i2 = jnp.min(jnp.where(lt2 == m2, e_io, N_EXPERTS), axis=0, keepdims=True)
    e2 = jnp.exp(m2 - m1)
    den = 1.0 + e2
    idx_ref[0:1, :] = i1
    idx_ref[1:2, :] = i2
    gate_ref[0:1, :] = 1.0 / den
    gate_ref[1:2, :] = e2 / den


def router(x, router_w_l, *, tm=512):
    M, D = x.shape
    return pl.pallas_call(
        _router_kernel,
        grid=(pl.cdiv(M, tm),),
        in_specs=[pl.BlockSpec((tm, D), lambda m: (m, 0)),
                  pl.BlockSpec((N_EXPERTS, D), lambda m: (0, 0))],
        out_specs=[pl.BlockSpec((TOP_K, tm), lambda m: (0, m)), pl.BlockSpec((TOP_K, tm), lambda m: (0, m))],
        out_shape=[jax.ShapeDtypeStruct((TOP_K, M), jnp.int32), jax.ShapeDtypeStruct((TOP_K, M), F32)],
        compiler_params=_cparams(1),
        name="router",
    )(x, router_w_l.T)


def _scatter_rows_kernel(pos_ref, x_ref, xs_in, xs_hbm, sem, *, rows, M):
    del xs_in
    base = pl.program_id(0) * rows
    n = jnp.minimum(rows, M - base)

    def row_copy(kk, r, dst_row):
        return pltpu.make_async_copy(x_ref.at[r], xs_hbm.at[dst_row], sem)

    def issue(r, c):
        for kk in range(TOP_K):
            row_copy(kk, r, pos_ref[kk, base + r]).start()
        return c

    lax.fori_loop(0, n, issue, 0)

    def drain(r, c):
        for kk in range(TOP_K):
            row_copy(kk, r, 0).wait()
        return c

    lax.fori_loop(0, n, drain, 0)


def scatter_rows(x, pos, P, *, rows=256):
    M, D = x.shape
    Mp = pl.cdiv(M, rows) * rows
    x3 = x.reshape(M, D // LANES, LANES)
    pos = jnp.pad(pos, ((0, 0), (0, Mp - M)))
    xs0 = jnp.zeros((P, D // LANES, LANES), x.dtype)
    out = pl.pallas_call(
        functools.partial(_scatter_rows_kernel, rows=rows, M=M),
        grid_spec=pltpu.PrefetchScalarGridSpec(
            num_scalar_prefetch=1,
            grid=(Mp // rows,),
            in_specs=[pl.BlockSpec((rows, D // LANES, LANES), lambda i, p: (i, 0, 0)),
                      pl.BlockSpec(memory_space=pl.ANY)],
            out_specs=pl.BlockSpec(memory_space=pl.ANY),
            scratch_shapes=[pltpu.SemaphoreType.DMA(())],
        ),
        out_shape=jax.ShapeDtypeStruct(xs0.shape, x.dtype),
        input_output_aliases={2: 0},
        compiler_params=_cparams(1),
        name="scatter_rows",
    )(pos, x3, xs0)
    return out.reshape(P, D)


def _for_used_rows(n_rows, tm, o_ref, fn):
    n_steps = (n_rows + MOE_ROW_STEP - 1) // MOE_ROW_STEP
    for k in range(tm // MOE_ROW_STEP + 1):
        @pl.when(n_steps == k)
        def _(k=k):
            r = k * MOE_ROW_STEP
            if r:
                o_ref[0:r, :] = fn(slice(0, r)).astype(o_ref.dtype)
            if r < tm:
                o_ref[r:tm, :] = jnp.zeros((tm - r, o_ref.shape[1]), o_ref.dtype)


def _moe_up_kernel(te_ref, tf_ref, tr_ref, x_ref, w1_ref, w3_ref, o_ref, w1b_ref, w3b_ref):
    t = pl.program_id(1)

    @pl.when(tf_ref[t] == 1)
    def _():
        w1b_ref[...] = w1_ref[...].astype(BF16)
        w3b_ref[...] = w3_ref[...].astype(BF16)

    def swiglu(rs):
        x = x_ref[rs, :]
        a = jnp.dot(x, w1b_ref[...], preferred_element_type=F32)
        b = jnp.dot(x, w3b_ref[...], preferred_element_type=F32)
        return a * _sigmoid(a) * b

    _for_used_rows(tr_ref[t], MOE_TM, o_ref, swiglu)


def moe_up(xs, w1, w3, layer, tile_expert, tile_first, tile_rows, *, tn):
    P, K = xs.shape
    N = w1.shape[3]
    tm = MOE_TM
    nt = P // tm
    wspec = pl.BlockSpec((None, None, K, tn), lambda n, t, te, tf, tr: (layer, te[t], 0, n))
    return pl.pallas_call(
        _moe_up_kernel,
        grid_spec=pltpu.PrefetchScalarGridSpec(
            num_scalar_prefetch=3,
            grid=(N // tn, nt),
            in_specs=[pl.BlockSpec((tm, K), lambda n, t, te, tf, tr: (t, 0)), wspec, wspec],
            out_specs=pl.BlockSpec((tm, tn), lambda n, t, te, tf, tr: (t, n)),
            scratch_shapes=[pltpu.VMEM((K, tn), BF16), pltpu.VMEM((K, tn), BF16)],
        ),
        out_shape=jax.ShapeDtypeStruct((P, N), BF16),
        compiler_params=_cparams(2),
        name="moe_up",
    )(tile_expert, tile_first, tile_rows, xs, w1, w3)


def _moe_down_kernel(te_ref, tf_ref, tr_ref, h_ref, w_ref, o_ref, wb_ref, *, sub, tm):
    t = pl.program_id(1)
    tt = t // sub
    part = t % sub

    @pl.when((tf_ref[tt] == 1) & (part == 0))
    def _():
        wb_ref[...] = w_ref[...].astype(BF16)

    n_rows = jnp.clip(tr_ref[tt] - part * tm, 0, tm)
    _for_used_rows(n_rows, tm, o_ref,
                   lambda rs: jnp.dot(h_ref[rs, :], wb_ref[...], preferred_element_type=F32))


def moe_down(h, w2, layer, tile_expert, tile_first, tile_rows, *, tm, tn):
    P, K = h.shape
    N = w2.shape[3]
    assert MOE_TM % tm == 0 and tm % MOE_ROW_STEP == 0
    sub = MOE_TM // tm
    return pl.pallas_call(
        functools.partial(_moe_down_kernel, sub=sub, tm=tm),
        grid_spec=pltpu.PrefetchScalarGridSpec(
            num_scalar_prefetch=3,
            grid=(N // tn, P // tm),
            in_specs=[pl.BlockSpec((tm, K), lambda n, t, te, tf, tr: (t, 0)),
                      pl.BlockSpec((None, None, K, tn), lambda n, t, te, tf, tr: (layer, te[t // sub], 0, n))],
            out_specs=pl.BlockSpec((tm, tn), lambda n, t, te, tf, tr: (t, n)),
            scratch_shapes=[pltpu.VMEM((K, tn), BF16)],
        ),
        out_shape=jax.ShapeDtypeStruct((P, N), F32),
        compiler_params=_cparams(2),
        name="moe_down",
    )(tile_expert, tile_first, tile_rows, h, w2)


def _moe_combine_kernel(pos_ref, gate_ref, e_hbm, o_ref, buf_ref, sem, *, rows):
    base = pl.program_id(0) * rows

    def row_copy(kk, r, src_row):
        return pltpu.make_async_copy(e_hbm.at[src_row], buf_ref.at[kk, r], sem)

    def issue(r, c):
        for kk in range(TOP_K):
            row_copy(kk, r, pos_ref[kk, base + r]).start()
        return c

    lax.fori_loop(0, rows, issue, 0)

    def drain(r, c):
        for kk in range(TOP_K):
            row_copy(kk, r, 0).wait()
        return c

    lax.fori_loop(0, rows, drain, 0)
    o_ref[...] = gate_ref[:, 0] * buf_ref[0] + gate_ref[:, 1] * buf_ref[1]


def moe_combine(eout, pos, gate, *, rows=128):
    M = gate.shape[0]
    P, D = eout.shape
    Mp = pl.cdiv(M, rows) * rows
    e3 = eout.reshape(P, D // LANES, LANES)
    pos = jnp.pad(pos, ((0, 0), (0, Mp - M)))
    out = pl.pallas_call(
        functools.partial(_moe_combine_kernel, rows=rows),
        grid_spec=pltpu.PrefetchScalarGridSpec(
            num_scalar_prefetch=1,
            grid=(Mp // rows,),
            in_specs=[pl.BlockSpec((rows, TOP_K, 1, 1), lambda i, p: (i, 0, 0, 0)),
                      pl.BlockSpec(memory_space=pl.ANY)],
            out_specs=pl.BlockSpec((rows, D // LANES, LANES), lambda i, p: (i, 0, 0)),
            scratch_shapes=[pltpu.VMEM((TOP_K, rows, D // LANES, LANES), F32),
                            pltpu.SemaphoreType.DMA(())],
        ),
        out_shape=jax.ShapeDtypeStruct((M, D // LANES, LANES), F32),
        compiler_params=_cparams(1),
        name="moe_combine",
    )(pos, gate.reshape(M, TOP_K, 1, 1), e3)
    return out.reshape(M, D)


def _routing_tables(idx, tm):
    M = idx.shape[1]
    NP = TOP_K * M
    nb = pl.cdiv(NP, LANES)
    e_flat = jnp.pad(idx.reshape(-1), (0, nb * LANES - NP), constant_values=N_EXPERTS)
    oh = (e_flat[None, :] == jnp.arange(N_EXPERTS)[:, None]).astype(F32).reshape(N_EXPERTS, nb, LANES)
    tri = jnp.asarray(np.triu(np.ones((LANES, LANES), np.float32)))
    within = jnp.einsum("ebj,ji->ebi", oh, tri, precision=lax.Precision.HIGHEST)
    blk_tot = within[..., -1]
    blk_pre = jnp.cumsum(blk_tot, axis=1) - blk_tot
    counts = (blk_pre[:, -1] + blk_tot[:, -1]).astype(jnp.int32)
    tiles_per = (counts + tm - 1) // tm
    tile_start = jnp.cumsum(tiles_per) - tiles_per
    rank = within + blk_pre[..., None] - 1.0
    pos = jnp.sum(oh * (rank + (tile_start * tm).astype(F32)[:, None, None]), axis=0)
    pos = pos.reshape(-1)[:NP].astype(jnp.int32).reshape(TOP_K, M)
    n_tiles = NP // tm + N_EXPERTS
    t_io = jnp.arange(n_tiles)
    used = jnp.sum(tiles_per)
    tile_expert = jnp.clip(jnp.sum(t_io[:, None] >= tile_start[None, :], axis=1) - 1, 0, N_EXPERTS - 1)
    tile_valid = (t_io < used).astype(jnp.int32)
    last_used_expert = jnp.max(jnp.where(tile_valid == 1, tile_expert, 0))
    tile_expert = jnp.where(tile_valid == 1, tile_expert, last_used_expert).astype(jnp.int32)
    prev = jnp.concatenate([jnp.array([-1], jnp.int32), tile_expert[:-1]])
    tile_first = (tile_expert != prev).astype(jnp.int32)
    own = jnp.arange(N_EXPERTS)[None, :] == tile_expert[:, None]
    of_tile = lambda per_expert: jnp.sum(jnp.where(own, per_expert[None, :], 0), axis=1)
    rows_left = of_tile(counts) - (t_io - of_tile(tile_start)) * tm
    tile_rows = jnp.where(tile_valid == 1, jnp.clip(rows_left, 0, tm), 0).astype(jnp.int32)
    return pos, tile_expert, tile_first, tile_rows, n_tiles * tm


def moe_ffn(x, xb, router_w_l, w1, w3, w2, layer):
    idx, gate = router(x, router_w_l)
    pos, tile_expert, tile_first, tile_rows, P = _routing_tables(idx, MOE_TM)
    xs = scatter_rows(xb, pos, P)
    h = moe_up(xs, w1, w3, layer, tile_expert, tile_first, tile_rows, tn=1024)
    eout = moe_down(h, w2, layer, tile_expert, tile_first, tile_rows, tm=256, tn=512)
    return moe_combine(eout, pos, gate.T)


def kernel(x_prompt, x_sample, cache_k, cache_v, page_table, state_hgrn, rel_bias, w_in, w_pa, w_pb, w_out,
           lam_qk, subln_g, hgrn_g, lb_raw, ln_g, ln_b, ffn_w1, ffn_w3, ffn_w2, router_w, moe_w1, moe_w3, moe_w2):
    B, S, D = x_prompt.shape
    DB = x_sample.shape[0]
    MP = B * S
    lbs = jax.nn.softmax(lb_raw.astype(F32), axis=0)
    lbs = jnp.cumsum(lbs, axis=0) - lbs[0]
    bias_tiles = _bias_tiles_t(rel_bias, ATT_T)
    ck = cache_k.reshape(cache_k.shape[0], cache_k.shape[1], PAGE_ROWS, 2 * DK_A)
    cv = cache_v.reshape(cache_v.shape[0], cache_v.shape[1], PAGE_ROWS, DV_A)

    xp = x_prompt.reshape(MP, D)
    xs = x_sample.reshape(DB, D)
    x = (xp, xs)
    xb = jnp.concatenate([xp.astype(BF16), xs.astype(BF16)], axis=0)
    kp = jnp.zeros((DEPTH, MP, W_A), F32)
    vp = jnp.zeros((DEPTH, MP, W_A), F32)
    ks = jnp.zeros((DEPTH, DB, W_A), F32)
    vs = jnp.zeros((DEPTH, DB, W_A), F32)
    sp, ss = [], []
    for l in range(DEPTH):
        lam_init = 0.8 - 0.6 * math.exp(-0.3 * l)
        lq = lam_qk[l].astype(F32)
        lam = jnp.exp(jnp.sum(lq[0] * lq[1])) - jnp.exp(jnp.sum(lq[2] * lq[3])) + lam_init

        z, kp, vp, ks, vs = in_proj(xb, w_in, l, kp, vp, ks, vs, tm=1024, tn=W_A)
        zs = z[MP:]
        ya = prompt_attention(z, kp, vp, l, bias_tiles, lam, subln_g[l], lam_init, B, S)
        yb, s_p = hgrn_prompt(z, lbs[l], hgrn_g[l], B, S)
        heads = lambda a: a.reshape(DB, H_A, DV_A)
        ya_s = sample_attention(heads(zs[:, COLZ["q_a"]:COLZ["q_a"] + W_A]), heads(ks[l]), heads(vs[l]),
                                ck, cv, page_table, rel_bias, lam, subln_g[l], lam_init, l)
        colv = lambda n: zs[:, COLZ[n]:COLZ[n] + W_BK].reshape(DB, H_B, DK_B, 1)
        rowv = lambda n: zs[:, COLZ[n]:COLZ[n] + W_B].reshape(DB, H_B, 1, DV_B)
        yb_s, s_s = hgrn_step(colv("f_b"), colv("q_b"), rowv("i_b"), rowv("og_b"), lbs[l], hgrn_g[l],
                              state_hgrn, l)
        mg = branch_merge(ya, yb, ya_s.reshape(DB, W_A), yb_s.reshape(DB, W_B), z, w_pa, w_pb, l,
                          tm=1024, tn=512)
        mo = matmul(mg, w_out, l, tm=1024, tn=512)
        x, xb = residual_ln(x, mo, ln_g[l, 0], ln_b[l, 0], MP)

        j = l // 2
        if l % 2 == 0:
            hh = swiglu_up(xb, ffn_w1, ffn_w3, j, tm=1024, tn=512)
            c = matmul(hh, ffn_w2, j, tm=512, tn=512)
        else:
            c = moe_ffn(x, xb, router_w[j], moe_w1, moe_w3, moe_w2, j)
        last = l == DEPTH - 1
        x, xb = residual_ln(x, c, ln_g[l, 1], ln_b[l, 1], MP, out_split=last)
        sp.append(s_p)
        ss.append(s_s)
    y_prompt, y_sample = x, xb
    return (y_prompt.reshape(B, S, D), y_sample.reshape(DB, 1, D),
            kp.reshape(DEPTH, B, S, H_A, 2 * DK_A), vp.reshape(DEPTH, B, S, H_A, DV_A), jnp.stack(sp),
            ks.reshape(DEPTH, DB, 1, H_A, 2 * DK_A), vs.reshape(DEPTH, DB, 1, H_A, DV_A), jnp.stack(ss))
```

```python
import functools
import math

import numpy as np
import jax
import jax.numpy as jnp
from jax import lax
from jax.experimental import pallas as pl
from jax.experimental.pallas import tpu as pltpu

F32 = jnp.float32
BF16 = jnp.bfloat16

D_MODEL = 2048
DEPTH = 2
PAGE_SIZE = 128
H_A = 8
DK_A = 64
DV_A = 2 * DK_A
W_A = H_A * DV_A
H_B = 8
DK_B = 128
DV_B = 128
W_BK = H_B * DK_B
W_B = H_B * DV_B
CHUNK = 64
N_BUCKETS = 32
MAX_EXACT = N_BUCKETS // 2
MAX_DISTANCE = 128
N_EXPERTS = 8
TOP_K = 2
ALPHA = (2 * DEPTH) ** 0.25
LN_EPS = 1e-5
RMS_EPS = 1e-6
F_MIN = 1e-20
NEG_INF = -1e30
LOG2E = 1.0 / math.log(2.0)
SPLITS = [H_A * 2 * DK_A, H_A * 2 * DK_A, W_A, W_BK, W_BK, W_B, W_B, D_MODEL, D_MODEL]
N_IN = sum(SPLITS)
_NAMES = ["q_a", "k_a", "v_a", "f_b", "q_b", "i_b", "og_b", "gt_a", "gt_b"]
COL = {}
COLZ = {}
_acc = _accz = 0
for _name, _w in zip(_NAMES, SPLITS):
    COL[_name] = _acc
    _acc += _w
    if _name not in ("k_a", "v_a"):
        COLZ[_name] = _accz
        _accz += _w
N_Z = _accz

LANES = 128
SUBLANES = 8
V7X_VMEM_BYTES = 64 * 1024 * 1024
VMEM_LIMIT = V7X_VMEM_BYTES - 4 * 1024 * 1024

ATT_T = 256
ATT_HALF = LANES
ATT_HEADS = 2
MOE_TM = 1024
MOE_ROW_STEP = 128
PAGES_PER_STEP = 8
HGRN_UNROLL = 4


def _cparams(n_axes):
    return pltpu.CompilerParams(
        dimension_semantics=("arbitrary",) * n_axes, vmem_limit_bytes=VMEM_LIMIT)


def _sigmoid(x):
    return jax.nn.sigmoid(x)


def _dot_nt(a, b):
    return lax.dot_general(a, b, (((1,), (1,)), ((), ())), preferred_element_type=F32)


def _dot_tn(a, b):
    return lax.dot_general(a, b, (((0,), (0,)), ((), ())), preferred_element_type=F32)


def _for_rows(m, M, tm, fn):
    n_full, tail = M // tm, M % tm
    if n_full:
        @pl.when(m < n_full)
        def _():
            fn(slice(None))
    if tail:
        @pl.when(m == n_full)
        def _():
            fn(slice(0, tail))


def _mm_kernel(x_ref, w_ref, o_ref, wb_ref, *, M, tm):
    m = pl.program_id(1)

    @pl.when(m == 0)
    def _():
        wb_ref[...] = w_ref[...].astype(BF16)

    def body(rs):
        o_ref[rs, :] = jnp.dot(x_ref[rs, :], wb_ref[...], preferred_element_type=F32).astype(o_ref.dtype)

    _for_rows(m, M, tm, body)


def matmul(x, w, layer, *, tm, tn, out_dtype=F32):
    M, K = x.shape
    N = w.shape[2]
    assert N % tn == 0
    return pl.pallas_call(
        functools.partial(_mm_kernel, M=M, tm=tm),
        grid=(N // tn, pl.cdiv(M, tm)),
        in_specs=[
            pl.BlockSpec((tm, K), lambda n, m: (m, 0)),
            pl.BlockSpec((None, K, tn), lambda n, m: (layer, 0, n)),
        ],
        out_specs=pl.BlockSpec((tm, tn), lambda n, m: (m, n)),
        out_shape=jax.ShapeDtypeStruct((M, N), out_dtype),
        scratch_shapes=[pltpu.VMEM((K, tn), BF16)],
        compiler_params=_cparams(2),
        name="matmul",
    )(x, w)


def _in_proj_kernel(x_ref, w_ref, kp_in, vp_in, ks_in, vs_in, z_ref, kp_ref, vp_ref, ks_ref, vs_ref, wb_ref, *,
                    n_full, tail, nk, nv):
    del kp_in, vp_in, ks_in, vs_in
    n = pl.program_id(0)
    m = pl.program_id(1)

    @pl.when(m == 0)
    def _():
        wb_ref[...] = w_ref[...].astype(BF16)

    def emit(cond, full_ref, tail_ref, tail_rows):
        @pl.when(cond & (m < n_full))
        def _():
            full_ref[...] = jnp.dot(x_ref[...], wb_ref[...], preferred_element_type=F32)

        @pl.when(cond & (m == n_full))
        def _():
            tail_ref[tail_rows, :] = jnp.dot(x_ref[0:tail, :], wb_ref[...], preferred_element_type=F32)

    emit((n != nk) & (n != nv), z_ref, z_ref, slice(0, tail))
    emit(n == nk, kp_ref, ks_ref, slice(None))
    emit(n == nv, vp_ref, vs_ref, slice(None))


def in_proj(x, w_in, layer, kp, vp, ks, vs, *, tm, tn):
    M, K = x.shape
    MP, DB = kp.shape[1], ks.shape[1]
    assert tn == W_A and M == MP + DB and MP % tm == 0 and DB < tm and N_IN % tn == 0
    n_full = MP // tm
    nk, nv = COL["k_a"] // tn, COL["v_a"] // tn
    assert nv == nk + 1 and COL["q_a"] == 0

    def z_idx(n, m):
        hold = (n == nk) | (n == nv)
        return (jnp.where(hold, n_full, m), jnp.where(n < nk, n, jnp.where(hold, nk - 1, n - 2)))

    def p_idx(n_own):
        def idx(n, m):
            return (layer, jnp.where(n < n_own, 0, jnp.where(n == n_own, jnp.minimum(m, n_full - 1), n_full - 1)), 0)
        return idx

    s_idx = lambda n, m: (layer, 0, 0)
    anyspec = pl.BlockSpec(memory_space=pl.ANY)
    return pl.pallas_call(
        functools.partial(_in_proj_kernel, n_full=n_full, tail=DB, nk=nk, nv=nv),
        grid=(N_IN // tn, n_full + 1),
        in_specs=[
            pl.BlockSpec((tm, K), lambda n, m: (m, 0)),
            pl.BlockSpec((None, K, tn), lambda n, m: (layer, 0, n)),
            anyspec, anyspec, anyspec, anyspec,
        ],
        out_specs=[
            pl.BlockSpec((tm, tn), z_idx),
            pl.BlockSpec((None, tm, tn), p_idx(nk)),
            pl.BlockSpec((None, tm, tn), p_idx(nv)),
            pl.BlockSpec((None, DB, tn), s_idx),
            pl.BlockSpec((None, DB, tn), s_idx),
        ],
        out_shape=[jax.ShapeDtypeStruct((M, N_Z), F32)] + [jax.ShapeDtypeStruct(a.shape, a.dtype)
                                                            for a in (kp, vp, ks, vs)],
        input_output_aliases={2: 1, 3: 2, 4: 3, 5: 4},
        scratch_shapes=[pltpu.VMEM((K, tn), BF16)],
        compiler_params=_cparams(2),
        name="in_proj",
    )(x, w_in, kp, vp, ks, vs)


def _swiglu_kernel(x_ref, w1_ref, w3_ref, o_ref, w1b_ref, w3b_ref, *, M, tm):
    m = pl.program_id(1)

    @pl.when(m == 0)
    def _():
        w1b_ref[...] = w1_ref[...].astype(BF16)
        w3b_ref[...] = w3_ref[...].astype(BF16)

    def body(rs):
        x = x_ref[rs, :]
        a = jnp.dot(x, w1b_ref[...], preferred_element_type=F32)
        b = jnp.dot(x, w3b_ref[...], preferred_element_type=F32)
        o_ref[rs, :] = (a * _sigmoid(a) * b).astype(o_ref.dtype)

    _for_rows(m, M, tm, body)


def swiglu_up(x, w1, w3, layer, *, tm, tn):
    M, K = x.shape
    N = w1.shape[2]
    assert N % tn == 0
    return pl.pallas_call(
        functools.partial(_swiglu_kernel, M=M, tm=tm),
        grid=(N // tn, pl.cdiv(M, tm)),
        in_specs=[
            pl.BlockSpec((tm, K), lambda n, m: (m, 0)),
            pl.BlockSpec((None, K, tn), lambda n, m: (layer, 0, n)),
            pl.BlockSpec((None, K, tn), lambda n, m: (layer, 0, n)),
        ],
        out_specs=pl.BlockSpec((tm, tn), lambda n, m: (m, n)),
        out_shape=jax.ShapeDtypeStruct((M, N), BF16),
        scratch_shapes=[pltpu.VMEM((K, tn), BF16), pltpu.VMEM((K, tn), BF16)],
        compiler_params=_cparams(2),
        name="swiglu_up",
    )(x, w1, w3)


def _merge_kernel(ya_ref, yb_ref, yas_ref, ybs_ref, wa_ref, wb_ref, ga_ref, gb_ref, o_ref, wab_ref, wbb_ref, *,
                  n_full, tail):
    m = pl.program_id(1)

    @pl.when(m == 0)
    def _():
        wab_ref[...] = wa_ref[...].astype(BF16)
        wbb_ref[...] = wb_ref[...].astype(BF16)

    def merge(ya, yb, rs):
        pa = jnp.dot(ya, wab_ref[...], preferred_element_type=F32)
        pb = jnp.dot(yb, wbb_ref[...], preferred_element_type=F32)
        o_ref[rs, :] = (_sigmoid(ga_ref[rs, :]) * pa + _sigmoid(gb_ref[rs, :]) * pb).astype(o_ref.dtype)

    @pl.when(m < n_full)
    def _():
        merge(ya_ref[...], yb_ref[...], slice(None))

    @pl.when(m == n_full)
    def _():
        merge(yas_ref[...], ybs_ref[...], slice(0, tail))


def branch_merge(ya, yb, ya_s, yb_s, z, w_pa, w_pb, layer, *, tm, tn):
    MP, K = ya.shape
    DB = ya_s.shape[0]
    M = z.shape[0]
    N = w_pa.shape[2]
    assert N % tn == 0 and MP % tm == 0 and M == MP + DB and DB < tm
    n_full = MP // tm
    ga0, gb0 = COLZ["gt_a"] // tn, COLZ["gt_b"] // tn
    prow = pl.BlockSpec((tm, K), lambda n, m: (jnp.minimum(m, n_full - 1), 0))
    srow = pl.BlockSpec((DB, K), lambda n, m: (0, 0))
    return pl.pallas_call(
        functools.partial(_merge_kernel, n_full=n_full, tail=DB),
        grid=(N // tn, n_full + 1),
        in_specs=[
            prow, prow, srow, srow,
            pl.BlockSpec((None, K, tn), lambda n, m: (layer, 0, n)),
            pl.BlockSpec((None, K, tn), lambda n, m: (layer, 0, n)),
            pl.BlockSpec((tm, tn), lambda n, m: (m, ga0 + n)),
            pl.BlockSpec((tm, tn), lambda n, m: (m, gb0 + n)),
        ],
        out_specs=pl.BlockSpec((tm, tn), lambda n, m: (m, n)),
        out_shape=jax.ShapeDtypeStruct((M, N), BF16),
        scratch_shapes=[pltpu.VMEM((K, tn), BF16), pltpu.VMEM((K, tn), BF16)],
        compiler_params=_cparams(2),
        name="branch_merge",
    )(ya, yb, ya_s, yb_s, w_pa, w_pb, z, z)


def _add_ln(x, y, g, b):
    h = ALPHA * x + y
    mu = jnp.mean(h, axis=-1, keepdims=True)
    c = h - mu
    var = jnp.mean(c * c, axis=-1, keepdims=True)
    return c * lax.rsqrt(var + LN_EPS) * g + b


def _res_ln_kernel(*refs, n_full, tail, x_split):
    refs = list(refs)
    xp_ref = refs.pop(0)
    xs_ref = refs.pop(0) if x_split else None
    y_ref, g_ref, b_ref, o_ref, ob_ref = refs
    m = pl.program_id(0)

    @pl.when(m < n_full)
    def _():
        o = _add_ln(xp_ref[...], y_ref[...], g_ref[...], b_ref[...])
        o_ref[...] = o
        ob_ref[...] = o.astype(BF16)

    @pl.when(m == n_full)
    def _():
        xs = xs_ref[...] if x_split else xp_ref[0:tail, :]
        o = _add_ln(xs, y_ref[0:tail, :], g_ref[...], b_ref[...])
        o_ref[0:tail, :] = o
        ob_ref[0:tail, :] = o.astype(BF16)


def residual_ln(x, y, g, b, MP, *, tm=256):
    x_split = isinstance(x, tuple)
    M, D = y.shape
    DB = M - MP
    assert MP % tm == 0 and 0 < DB < tm
    n_full = MP // tm
    row = pl.BlockSpec((tm, D), lambda m: (m, 0))
    prow = pl.BlockSpec((tm, D), lambda m: (jnp.minimum(m, n_full - 1), 0))
    srow = pl.BlockSpec((DB, D), lambda m: (0, 0))
    vec = pl.BlockSpec((1, D), lambda m: (0, 0))
    x_args = list(x) if x_split else [x]
    x_specs = [prow, srow] if x_split else [row]
    return pl.pallas_call(
        functools.partial(_res_ln_kernel, n_full=n_full, tail=DB, x_split=x_split),
        grid=(n_full + 1,),
        in_specs=x_specs + [row, vec, vec],
        out_specs=[row, row],
        out_shape=[jax.ShapeDtypeStruct((M, D), F32), jax.ShapeDtypeStruct((M, D), BF16)],
        compiler_params=_cparams(1),
        name="residual_ln",
    )(*x_args, y, g.reshape(1, D), b.reshape(1, D))


def _t5_bucket(rel):
    n = jnp.maximum(rel, 0)
    large = MAX_EXACT + (jnp.log(jnp.maximum(n, 1).astype(F32) / MAX_EXACT)
                         / math.log(MAX_DISTANCE / MAX_EXACT) * (N_BUCKETS - MAX_EXACT)).astype(jnp.int32)
    large = jnp.clip(large, 0, N_BUCKETS - 1)
    return jnp.where(n < MAX_EXACT, n, large)


def _bucket_np(n):
    n = np.maximum(n, 0)
    large = MAX_EXACT + (np.log(np.maximum(n, 1) / MAX_EXACT)
                         / math.log(MAX_DISTANCE / MAX_EXACT) * (N_BUCKETS - MAX_EXACT)).astype(np.int64)
    return np.where(n < MAX_EXACT, n, np.clip(large, 0, N_BUCKETS - 1))


assert (_bucket_np(np.arange(min(ATT_T, PAGE_SIZE) + 1, 1 << 16)) == N_BUCKETS - 1).all()


def _bias_tiles_t(rel_bias, T):
    n = jnp.arange(-(T - 1), 3 * T)
    vec = jnp.where((n >= 0)[:, None], rel_bias[_t5_bucket(n)].astype(F32), NEG_INF).T
    L = 2 * T
    tiles = []
    for d in range(3):
        f0 = d * T + (T - 1)
        a = jnp.concatenate([vec[:, f0:f0 + T], vec[:, :1], vec[:, f0 - (T - 1):f0]], axis=1)
        skew = jnp.tile(a, (1, T))[:, :T * (L - 1)].reshape(H_A, T, L - 1)
        tiles.append(skew[:, :, :T])
    return jnp.stack(tiles, axis=1)


def _rms_head(o, g):
    return o * lax.rsqrt(jnp.mean(o * o, axis=-1, keepdims=True) + RMS_EPS) * g


def _prompt_attn_kernel(lam_ref, q_ref, k_ref, v_ref, bias_ref, g_ref, o_ref, kb_ref, vb_ref, s_ref, p_ref, *,
                        T, out_scale):
    qi = pl.program_id(2)
    n_half = T // ATT_HALF

    @pl.when(qi == 0)
    def _():
        kb_ref[...] = k_ref[...].astype(BF16)
        vb_ref[...] = v_ref[...].astype(BF16)

    lane = lax.broadcasted_iota(jnp.int32, (1, 2 * DK_A), 1)
    qm = []
    for hd in range(ATT_HEADS):
        q = q_ref[:, hd * DV_A:(hd + 1) * DV_A] * (DK_A ** -0.5 * LOG2E)
        qm.append((jnp.where(lane < DK_A, q, 0.0).astype(BF16), jnp.where(lane >= DK_A, q, 0.0).astype(BF16)))
    blocks = [(hd, mi, hi) for hd in range(ATT_HEADS) for mi in range(2) for hi in range(n_half)]
    nb = len(blocks)

    def kv_rows(kj):
        return pl.ds(pl.multiple_of(kj * T, T), T)

    def head_cols(hd):
        return slice(hd * DV_A, (hd + 1) * DV_A)

    def scores(kj):
        tile = jnp.clip(qi - kj, 0, 2)
        ks = [kb_ref[kv_rows(kj), head_cols(hd)] for hd in range(ATT_HEADS)]
        return [_dot_nt(ks[hd], qm[hd][mi][hi * ATT_HALF:(hi + 1) * ATT_HALF, :])
                + bias_ref[hd, tile, :, hi * ATT_HALF:(hi + 1) * ATT_HALF] for hd, mi, hi in blocks]

    def pv(kj):
        vs = [vb_ref[kv_rows(kj), head_cols(hd)] for hd in range(ATT_HEADS)]
        return [_dot_tn(vs[blocks[i][0]], p_ref[i]) for i in range(nb)]

    for i, s in enumerate(scores(0)):
        s_ref[i] = s
    p_ref[...] = jnp.zeros(p_ref.shape, p_ref.dtype)

    def step(kj, carry):
        s_next = scores(jnp.minimum(kj + 1, qi))
        pv_prev = pv(jnp.maximum(kj - 1, 0))
        out = []
        for i in range(nb):
            m, l, b = carry[i]
            s = s_ref[i]
            m_new = jnp.maximum(m, jnp.max(s, axis=0, keepdims=True))
            p = jnp.exp2(s - m_new)
            r = jnp.exp2(m - m_new)
            l_new = r * l + jnp.sum(p, axis=0, keepdims=True)
            p_ref[i] = p.astype(BF16)
            out.append((m_new, l_new, r * (b + pv_prev[i])))
        for i in range(nb):
            s_ref[i] = s_next[i]
        return tuple(out)

    init = tuple((jnp.full((1, ATT_HALF), NEG_INF, F32), jnp.zeros((1, ATT_HALF), F32),
                  jnp.zeros((DV_A, ATT_HALF), F32)) for _ in blocks)
    res = lax.fori_loop(0, qi + 1, step, init)
    pv_last = pv(qi)
    lam = lam_ref[0]
    for hd in range(ATT_HEADS):
        for hi in range(n_half):
            i1 = blocks.index((hd, 0, hi))
            i2 = blocks.index((hd, 1, hi))
            ot = ((res[i1][2] + pv_last[i1]) / res[i1][1]
                  - lam * ((res[i2][2] + pv_last[i2]) / res[i2][1]))
            ot = ot * lax.rsqrt(jnp.mean(ot * ot, axis=0, keepdims=True) + RMS_EPS)
            o_ref[hi * ATT_HALF:(hi + 1) * ATT_HALF, head_cols(hd)] = (
                ot.T * g_ref[:, head_cols(hd)] * out_scale).astype(o_ref.dtype)


def prompt_attention(z, k_all, v_all, layer, bias_tiles, lam, subln_g_l, lam_init, B, S):
    T = ATT_T
    nq = S // T
    W = ATT_HEADS * DV_A
    assert COLZ["q_a"] == 0 and H_A % ATT_HEADS == 0
    nblk = 2 * ATT_HEADS * (T // ATT_HALF)
    kern = functools.partial(_prompt_attn_kernel, T=T, out_scale=1.0 - lam_init)
    return pl.pallas_call(
        kern,
        grid=(B, H_A // ATT_HEADS, nq),
        in_specs=[
            pl.BlockSpec(memory_space=pltpu.SMEM),
            pl.BlockSpec((T, W), lambda b, h, i: (b * nq + i, h)),
            pl.BlockSpec((None, S, W), lambda b, h, i: (layer, b, h)),
            pl.BlockSpec((None, S, W), lambda b, h, i: (layer, b, h)),
            pl.BlockSpec((ATT_HEADS, 3, T, T), lambda b, h, i: (h, 0, 0, 0)),
            pl.BlockSpec((1, W), lambda b, h, i: (0, h)),
        ],
        out_specs=pl.BlockSpec((T, W), lambda b, h, i: (b * nq + i, h)),
        out_shape=jax.ShapeDtypeStruct((B * S, W_A), BF16),
        scratch_shapes=[pltpu.VMEM((S, W), BF16), pltpu.VMEM((S, W), BF16),
                        pltpu.VMEM((nblk, T, ATT_HALF), F32),
                        pltpu.VMEM((nblk, T, ATT_HALF), BF16)],
        compiler_params=_cparams(3),
        name="prompt_attention",
    )(lam.reshape(1), z, k_all, v_all, bias_tiles, subln_g_l.reshape(1, W_A))


_LEVELS = [CHUNK >> (i + 1) for i in range(int(math.log2(CHUNK)))]


def _cumsum_matrix():
    t = np.arange(CHUNK)
    return (t[None, :] <= t[:, None]).astype(np.float32)


def _anchor_rows(b, w):
    if 2 * w >= SUBLANES:
        g = b.reshape(CHUNK // (2 * w), 2 * w, b.shape[-1])
        return jnp.broadcast_to(g[:, w - 1:w, :], g.shape).reshape(b.shape)
    t = lax.broadcasted_iota(jnp.int32, b.shape, 0)
    off = (t & (2 * w - 1)) - (w - 1)
    out = b
    for d in range(-(w - 1), w + 1):
        if d != 0:
            out = jnp.where(off == d, pltpu.roll(b, d % CHUNK, 0), out)
    return out


def _split3(x):
    x1 = x.astype(BF16)
    r1 = x - x1.astype(F32)
    x2 = r1.astype(BF16)
    x3 = (r1 - x2.astype(F32)).astype(BF16)
    return x1, x2, x3


def _hgrn_gates(fx, qx, lb):
    f = lb + (1.0 - lb) * _sigmoid(fx)
    g = jnp.log(jnp.maximum(f, F_MIN))
    k = (1.0 - lb) * _sigmoid(-fx)
    q = qx * _sigmoid(qx)
    return g, k, q


def _hgrn_prompt_kernel(f_ref, q_ref, i_ref, og_ref, lb_ref, gain_ref, pm_ref, y_ref, s_ref,
                        oi_ref, qe_ref, u_ref, dl_ref, *, S):
    n_chunks = S // CHUNK
    lb = lb_ref[...]
    gain = gain_ref[...]
    pm = pm_ref[...]
    t_io = lax.broadcasted_iota(jnp.int32, (CHUNK, CHUNK), 0)
    s_io = lax.broadcasted_iota(jnp.int32, (CHUNK, CHUNK), 1)
    masks = [(((t_io ^ s_io) < 2 * w) & ((t_io & w) != 0)) & ((s_io & w) == 0) for w in _LEVELS]
    diag = t_io == s_io

    def group(gi, carry):
        cs = [gi * HGRN_UNROLL + u for u in range(HGRN_UNROLL)]
        rows = [pl.ds(pl.multiple_of(c * CHUNK, CHUNK), CHUNK) for c in cs]
        gkq = [_hgrn_gates(f_ref[r, :], q_ref[r, :], lb) for r in rows]
        vbs = [i_ref[r, :].astype(BF16) for r in rows]
        bs = []
        for g, _, _ in gkq:
            g1, g2, g3 = _split3(g)
            bs.append(jnp.dot(pm, g1, preferred_element_type=F32)
                      + jnp.dot(pm, g2, preferred_element_type=F32)
                      + jnp.dot(pm, g3, preferred_element_type=F32))
        a_s = [jnp.where(diag, _dot_nt(q.astype(BF16), k.astype(BF16)), 0.0) for _, k, q in gkq]
        for li, w in enumerate(_LEVELS):
            for u, (_, k, q) in enumerate(gkq):
                anc = _anchor_rows(bs[u], w)
                qs = (q * jnp.exp(jnp.minimum(bs[u] - anc, 0.0))).astype(BF16)
                ks = (k * jnp.exp(jnp.minimum(anc - bs[u], 0.0))).astype(BF16)
                a_s[u] = jnp.where(masks[li], _dot_nt(qs, ks), a_s[u])
        for u, (_, k, q) in enumerate(gkq):
            b = bs[u]
            b_last = b[CHUNK - 1:CHUNK, :]
            oi_ref[rows[u], :] = jnp.dot(a_s[u].astype(BF16), vbs[u], preferred_element_type=F32)
            qe_ref[rows[u], :] = (q * jnp.exp(b)).astype(BF16)
            kd = (k * jnp.exp(b_last - b)).astype(BF16)
            u_ref[cs[u]] = _dot_tn(vbs[u], kd)
            dl_ref[cs[u]] = jnp.exp(b_last)
        return carry

    lax.fori_loop(0, n_chunks // HGRN_UNROLL, group, 0)

    def phase_b(c, st):
        rows = pl.ds(pl.multiple_of(c * CHUNK, CHUNK), CHUNK)
        o = oi_ref[rows, :] + _dot_nt(qe_ref[rows, :], st.astype(BF16))
        y = _rms_head(o, gain) * _sigmoid(og_ref[rows, :])
        y_ref[rows, :] = y.astype(y_ref.dtype)
        return st * dl_ref[c] + u_ref[c]

    st = lax.fori_loop(0, n_chunks, phase_b, jnp.zeros((DV_B, DK_B), F32), unroll=HGRN_UNROLL)
    s_ref[...] = st.T


def hgrn_prompt(z, lb_l, hgrn_g_l, B, S):
    assert (S // CHUNK) % HGRN_UNROLL == 0
    fc, qc, ic, oc = (COLZ[n] // DK_B for n in ("f_b", "q_b", "i_b", "og_b"))
    pm = jnp.asarray(_cumsum_matrix(), BF16)
    kern = functools.partial(_hgrn_prompt_kernel, S=S)
    blk = lambda c0: pl.BlockSpec((S, DK_B), lambda b, h: (b, c0 + h))
    n_chunks = S // CHUNK
    return pl.pallas_call(
        kern,
        grid=(B, H_B),
        in_specs=[
            blk(fc), blk(qc), blk(ic), blk(oc),
            pl.BlockSpec((None, 1, DK_B), lambda b, h: (h, 0, 0)),
            pl.BlockSpec((None, 1, DV_B), lambda b, h: (h, 0, 0)),
            pl.BlockSpec(pm.shape, lambda b, h: (0, 0)),
        ],
        out_specs=[
            pl.BlockSpec((S, DV_B), lambda b, h: (b, h)),
            pl.BlockSpec((None, None, DK_B, DV_B), lambda b, h: (b, h, 0, 0)),
        ],
        out_shape=[
            jax.ShapeDtypeStruct((B * S, W_B), BF16),
            jax.ShapeDtypeStruct((B, H_B, DK_B, DV_B), F32),
        ],
        scratch_shapes=[pltpu.VMEM((S, DV_B), F32), pltpu.VMEM((S, DK_B), BF16),
                        pltpu.VMEM((n_chunks, DV_B, DK_B), F32), pltpu.VMEM((n_chunks, 1, DK_B), F32)],
        compiler_params=_cparams(2),
        name="hgrn_prompt",
    )(z, z, z, z, lb_l.reshape(H_B, 1, DK_B), hgrn_g_l.reshape(H_B, 1, DV_B), pm)


PAGE_ROWS = PAGE_SIZE * H_A
assert H_A & (H_A - 1) == 0


def _paged_attn_kernel(pt_ref, lam_ref, q_ref, kn_ref, vn_ref, bfar_ref, blast_ref, bnew_ref, g_ref, *rest,
                       n_groups, out_scale):
    G = PAGES_PER_STEP
    R = 2 * H_A
    k_refs = rest[:G]
    v_refs = rest[G:2 * G]
    o_ref = rest[2 * G]
    m_ref, l_ref, a_ref = rest[2 * G + 1:]
    j = pl.program_id(1)
    lane = lax.broadcasted_iota(jnp.int32, (1, 2 * DK_A), 1)
    q = q_ref[...] * (DK_A ** -0.5)
    qf = jnp.concatenate([jnp.where(lane < DK_A, q, 0.0), jnp.where(lane >= DK_A, q, 0.0)], axis=0)
    qb = qf.astype(BF16)
    col = lax.broadcasted_iota(jnp.int32, (R, PAGE_ROWS), 1)
    row = lax.broadcasted_iota(jnp.int32, (R, PAGE_ROWS), 0)
    valid = (col & (H_A - 1)) == (row & (H_A - 1))

    @pl.when(j == 0)
    def _():
        m_ref[...] = jnp.full(m_ref.shape, NEG_INF, F32)
        l_ref[...] = jnp.zeros(l_ref.shape, F32)
        a_ref[...] = jnp.zeros(a_ref.shape, F32)

    def scores(i, bias):
        s = _dot_nt(qb, k_refs[i][...].astype(BF16)) + bias
        return jnp.where(valid, s, NEG_INF)

    def absorb(s_list, new_token):
        m_old = m_ref[...]
        m_new = m_old
        for s in s_list:
            m_new = jnp.maximum(m_new, jnp.max(s, axis=-1, keepdims=True))
        if new_token:
            kn2 = jnp.concatenate([kn_ref[...], kn_ref[...]], axis=0)
            vn2 = jnp.concatenate([vn_ref[...], vn_ref[...]], axis=0)
            s_new = jnp.sum(qf * kn2, axis=-1, keepdims=True) + bnew_ref[...]
            m_new = jnp.maximum(m_new, s_new)
        r = jnp.exp(m_old - m_new)
        l = r * l_ref[...]
        acc = r * a_ref[...]
        for i, s in enumerate(s_list):
            p = jnp.exp(s - m_new)
            l = l + jnp.sum(p, axis=-1, keepdims=True)
            acc = acc + jnp.dot(p.astype(BF16), v_refs[i][...].astype(BF16), preferred_element_type=F32)
        if new_token:
            p_new = jnp.exp(s_new - m_new)
            l = l + p_new
            acc = acc + p_new * vn2
        m_ref[...] = m_new
        l_ref[...] = l
        a_ref[...] = acc

    far = bfar_ref[...]

    @pl.when(j < n_groups - 1)
    def _():
        absorb([scores(i, far) for i in range(G)], False)

    @pl.when(j == n_groups - 1)
    def _():
        absorb([scores(i, far) for i in range(G - 1)] + [scores(G - 1, blast_ref[...])], True)
        lam = lam_ref[0]
        o = a_ref[0:H_A] / l_ref[0:H_A] - lam * (a_ref[H_A:R] / l_ref[H_A:R])
        o_ref[...] = (_rms_head(o, g_ref[...]) * out_scale).astype(o_ref.dtype)


def sample_attention(q, k_new, v_new, cache_k, cache_v, page_table, rel_bias, lam, subln_g_l,
                     lam_init, layer):
    DB = q.shape[0]
    n_pages = page_table.shape[1]
    G = PAGES_PER_STEP
    R = 2 * H_A
    assert n_pages % G == 0
    n_groups = n_pages // G
    hmap = jnp.tile(jnp.arange(H_A), 2)
    bias_of = lambda dist: rel_bias[_t5_bucket(dist)].astype(F32)
    bfar = bias_of(jnp.array(2 * PAGE_SIZE))[hmap][:, None]
    bnew = bias_of(jnp.array(0))[hmap][:, None]
    blast = jnp.broadcast_to(bias_of(PAGE_SIZE - jnp.arange(PAGE_SIZE)).reshape(1, PAGE_ROWS), (R, PAGE_ROWS))
    kern = functools.partial(_paged_attn_kernel, n_groups=n_groups, out_scale=1.0 - lam_init)

    def page_spec(i):
        return pl.BlockSpec((None, None, PAGE_ROWS, DV_A),
                            lambda b, j, pt: (layer, pt[b * n_pages + j * G + i], 0, 0))

    tok = pl.BlockSpec((None, H_A, DV_A), lambda b, j, pt: (b, 0, 0))
    const = lambda shape: pl.BlockSpec(shape, lambda b, j, pt: (0,) * len(shape))
    grid_spec = pltpu.PrefetchScalarGridSpec(
        num_scalar_prefetch=1,
        grid=(DB, n_groups),
        in_specs=[
            pl.BlockSpec(memory_space=pltpu.SMEM),
            tok, tok, tok,
            const((R, 1)), const((R, PAGE_ROWS)), const((R, 1)), const((H_A, DV_A)),
        ] + [page_spec(i) for i in range(G)] + [page_spec(i) for i in range(G)],
        out_specs=tok,
        scratch_shapes=[pltpu.VMEM((R, 1), F32), pltpu.VMEM((R, 1), F32), pltpu.VMEM((R, DV_A), F32)],
    )
    return pl.pallas_call(
        kern,
        grid_spec=grid_spec,
        out_shape=jax.ShapeDtypeStruct((DB, H_A, DV_A), BF16),
        compiler_params=_cparams(2),
        name="sample_attention",
    )(page_table.reshape(-1), lam.reshape(1), q, k_new, v_new, bfar, blast, bnew,
      subln_g_l.reshape(H_A, DV_A), *([cache_k] * G), *([cache_v] * G))


def _hgrn_step_kernel(fx_ref, qx_ref, v_ref, og_ref, lb_ref, gain_ref, s_ref, y_ref, so_ref):
    g, k, q = _hgrn_gates(fx_ref[...], qx_ref[...], lb_ref[...])
    v = v_ref[...]
    s_new = jnp.exp(g) * s_ref[...].astype(F32) + k * v
    so_ref[...] = s_new.astype(so_ref.dtype)
    o = jnp.sum(q * s_new, axis=1, keepdims=True)
    y = _rms_head(o, gain_ref[...]) * _sigmoid(og_ref[...])
    y_ref[...] = y.astype(y_ref.dtype)


def hgrn_step(fx, qx, v, og, lb_l, hgrn_g_l, state, layer):
    DB = fx.shape[0]
    col = pl.BlockSpec((None, H_B, DK_B, 1), lambda b: (b, 0, 0, 0))
    row = pl.BlockSpec((None, H_B, 1, DV_B), lambda b: (b, 0, 0, 0))
    st = pl.BlockSpec((None, H_B, DK_B, DV_B), lambda b: (b, 0, 0, 0))
    return pl.pallas_call(
        _hgrn_step_kernel,
        grid=(DB,),
        in_specs=[col, col, row, row,
                  pl.BlockSpec((H_B, DK_B, 1), lambda b: (0, 0, 0)),
                  pl.BlockSpec((H_B, 1, DV_B), lambda b: (0, 0, 0)),
                  pl.BlockSpec((None, None, H_B, DK_B, DV_B), lambda b: (layer, b, 0, 0, 0))],
        out_specs=[row, st],
        out_shape=[jax.ShapeDtypeStruct((DB, H_B, 1, DV_B), BF16),
                   jax.ShapeDtypeStruct(state.shape[1:], state.dtype)],
        compiler_params=_cparams(1),
        name="hgrn_step",
    )(fx, qx, v, og, lb_l.reshape(H_B, DK_B, 1), hgrn_g_l.reshape(H_B, 1, DV_B), state)


def _router_kernel(x_ref, w_ref, idx_ref, gate_ref):
    x1, x2, _ = _split3(x_ref[...])
    w1, w2, _ = _split3(w_ref[...])
    lt = _dot_nt(w1, x1) + (_dot_nt(w1, x2) + _dot_nt(w2, x1))
    e_io = lax.broadcasted_iota(jnp.int32, lt.shape, 0)
    m1 = jnp.max(lt, axis=0, keepdims=True)
    i1 = jnp.min(jnp.where(lt == m1, e_io, N_EXPERTS), axis=0, keepdims=True)
    lt2 = jnp.where(e_io == i1, -jnp.inf, lt)
    m2 = jnp.max(lt2, axis=0, keepdims=True)
    i2 = jnp.min(jnp.where(lt2 == m2, e_io, N_EXPERTS), axis=0, keepdims=True)
    e2 = jnp.exp(m2 - m1)
    den = 1.0 + e2
    idx_ref[0:1, :] = i1
    idx_ref[1:2, :] = i2
    gate_ref[0:1, :] = 1.0 / den
    gate_ref[1:2, :] = e2 / den


def router(x, router_w_l, *, tm=512):
    M, D = x.shape
    return pl.pallas_call(
        _router_kernel,
        grid=(pl.cdiv(M, tm),),
        in_specs=[pl.BlockSpec((tm, D), lambda m: (m, 0)),
                  pl.BlockSpec((N_EXPERTS, D), lambda m: (0, 0))],
        out_specs=[pl.BlockSpec((TOP_K, tm), lambda m: (0, m)), pl.BlockSpec((TOP_K, tm), lambda m: (0, m))],
        out_shape=[jax.ShapeDtypeStruct((TOP_K, M), jnp.int32), jax.ShapeDtypeStruct((TOP_K, M), F32)],
        compiler_params=_cparams(1),
        name="router",
    )(x, router_w_l.T)


def _scatter_rows_kernel(pos_ref, x_ref, xs_in, xs_hbm, sem, *, rows, M):
    del xs_in
    base = pl.program_id(0) * rows
    n = jnp.minimum(rows, M - base)

    def row_copy(kk, r, dst_row):
        return pltpu.make_async_copy(x_ref.at[r], xs_hbm.at[dst_row], sem)

    def issue(r, c):
        for kk in range(TOP_K):
            row_copy(kk, r, pos_ref[kk, base + r]).start()
        return c

    lax.fori_loop(0, n, issue, 0)

    def drain(r, c):
        for kk in range(TOP_K):
            row_copy(kk, r, 0).wait()
        return c

    lax.fori_loop(0, n, drain, 0)


def scatter_rows(x, pos, P, *, rows=256):
    M, D = x.shape
    Mp = pl.cdiv(M, rows) * rows
    x3 = x.reshape(M, D // LANES, LANES)
    pos = jnp.pad(pos, ((0, 0), (0, Mp - M)))
    xs0 = jnp.zeros((P, D // LANES, LANES), x.dtype)
    out = pl.pallas_call(
        functools.partial(_scatter_rows_kernel, rows=rows, M=M),
        grid_spec=pltpu.PrefetchScalarGridSpec(
            num_scalar_prefetch=1,
            grid=(Mp // rows,),
            in_specs=[pl.BlockSpec((rows, D // LANES, LANES), lambda i, p: (i, 0, 0)),
                      pl.BlockSpec(memory_space=pl.ANY)],
            out_specs=pl.BlockSpec(memory_space=pl.ANY),
            scratch_shapes=[pltpu.SemaphoreType.DMA(())],
        ),
        out_shape=jax.ShapeDtypeStruct(xs0.shape, x.dtype),
        input_output_aliases={2: 0},
        compiler_params=_cparams(1),
        name="scatter_rows",
    )(pos, x3, xs0)
    return out.reshape(P, D)


def _for_used_rows(n_rows, tm, o_ref, fn):
    n_steps = (n_rows + MOE_ROW_STEP - 1) // MOE_ROW_STEP
    for k in range(tm // MOE_ROW_STEP + 1):
        @pl.when(n_steps == k)
        def _(k=k):
            r = k * MOE_ROW_STEP
            if r:
                o_ref[0:r, :] = fn(slice(0, r)).astype(o_ref.dtype)
            if r < tm:
                o_ref[r:tm, :] = jnp.zeros((tm - r, o_ref.shape[1]), o_ref.dtype)


def _moe_up_kernel(te_ref, tf_ref, tr_ref, x_ref, w1_ref, w3_ref, o_ref, w1b_ref, w3b_ref):
    t = pl.program_id(1)

    @pl.when(tf_ref[t] == 1)
    def _():
        w1b_ref[...] = w1_ref[...].astype(BF16)
        w3b_ref[...] = w3_ref[...].astype(BF16)

    def swiglu(rs):
        x = x_ref[rs, :]
        a = jnp.dot(x, w1b_ref[...], preferred_element_type=F32)
        b = jnp.dot(x, w3b_ref[...], preferred_element_type=F32)
        return a * _sigmoid(a) * b

    _for_used_rows(tr_ref[t], MOE_TM, o_ref, swiglu)


def moe_up(xs, w1, w3, layer, tile_expert, tile_first, tile_rows, *, tn):
    P, K = xs.shape
    N = w1.shape[3]
    tm = MOE_TM
    nt = P // tm
    wspec = pl.BlockSpec((None, None, K, tn), lambda n, t, te, tf, tr: (layer, te[t], 0, n))
    return pl.pallas_call(
        _moe_up_kernel,
        grid_spec=pltpu.PrefetchScalarGridSpec(
            num_scalar_prefetch=3,
            grid=(N // tn, nt),
            in_specs=[pl.BlockSpec((tm, K), lambda n, t, te, tf, tr: (t, 0)), wspec, wspec],
            out_specs=pl.BlockSpec((tm, tn), lambda n, t, te, tf, tr: (t, n)),
            scratch_shapes=[pltpu.VMEM((K, tn), BF16), pltpu.VMEM((K, tn), BF16)],
        ),
        out_shape=jax.ShapeDtypeStruct((P, N), BF16),
        compiler_params=_cparams(2),
        name="moe_up",
    )(tile_expert, tile_first, tile_rows, xs, w1, w3)


def _moe_down_kernel(te_ref, tf_ref, tr_ref, h_ref, w_ref, o_ref, wb_ref, *, sub, tm):
    t = pl.program_id(1)
    tt = t // sub
    part = t % sub

    @pl.when((tf_ref[tt] == 1) & (part == 0))
    def _():
        wb_ref[...] = w_ref[...].astype(BF16)

    n_rows = jnp.clip(tr_ref[tt] - part * tm, 0, tm)
    _for_used_rows(n_rows, tm, o_ref,
                   lambda rs: jnp.dot(h_ref[rs, :], wb_ref[...], preferred_element_type=F32))


def moe_down(h, w2, layer, tile_expert, tile_first, tile_rows, *, tm, tn):
    P, K = h.shape
    N = w2.shape[3]
    assert MOE_TM % tm == 0 and tm % MOE_ROW_STEP == 0
    sub = MOE_TM // tm
    return pl.pallas_call(
        functools.partial(_moe_down_kernel, sub=sub, tm=tm),
        grid_spec=pltpu.PrefetchScalarGridSpec(
            num_scalar_prefetch=3,
            grid=(N // tn, P // tm),
            in_specs=[pl.BlockSpec((tm, K), lambda n, t, te, tf, tr: (t, 0)),
                      pl.BlockSpec((None, None, K, tn), lambda n, t, te, tf, tr: (layer, te[t // sub], 0, n))],
            out_specs=pl.BlockSpec((tm, tn), lambda n, t, te, tf, tr: (t, n)),
            scratch_shapes=[pltpu.VMEM((K, tn), BF16)],
        ),
        out_shape=jax.ShapeDtypeStruct((P, N), F32),
        compiler_params=_cparams(2),
        name="moe_down",
    )(tile_expert, tile_first, tile_rows, h, w2)


def _moe_combine_ln_kernel(pos_ref, gate_ref, x_ref, g_ref, b_ref, e_hbm, op_ref, os_ref, buf_ref, sem, *,
                           rows, n_full, tail):
    i = pl.program_id(0)
    base = i * rows
    n = jnp.where(i < n_full, rows, tail)

    def row_copy(kk, r, src_row):
        return pltpu.make_async_copy(e_hbm.at[src_row], buf_ref.at[kk, r], sem)

    def issue(r, c):
        for kk in range(TOP_K):
            row_copy(kk, r, pos_ref[kk, base + r]).start()
        return c

    lax.fori_loop(0, n, issue, 0)

    def drain(r, c):
        for kk in range(TOP_K):
            row_copy(kk, r, 0).wait()
        return c

    lax.fori_loop(0, n, drain, 0)

    def finish(nr, out_ref):
        g0 = gate_ref[0:nr, 0:1]
        g1 = gate_ref[0:nr, 1:2]
        y = jnp.concatenate([g0 * buf_ref[0, 0:nr, j, :] + g1 * buf_ref[1, 0:nr, j, :]
                             for j in range(buf_ref.shape[2])], axis=-1)
        out_ref[...] = _add_ln(x_ref[0:nr, :], y, g_ref[...], b_ref[...])

    @pl.when(i < n_full)
    def _():
        finish(rows, op_ref)

    @pl.when(i == n_full)
    def _():
        finish(tail, os_ref)


def moe_combine_ln(eout, pos, gate, x, g, b, MP, *, rows=128):
    M, D = x.shape
    P = eout.shape[0]
    DB = M - MP
    assert MP % rows == 0 and 0 < DB < rows
    n_full = MP // rows
    e3 = eout.reshape(P, D // LANES, LANES)
    pos = jnp.pad(pos, ((0, 0), (0, (n_full + 1) * rows - M)))
    vec = pl.BlockSpec((1, D), lambda i, p: (0, 0))
    return pl.pallas_call(
        functools.partial(_moe_combine_ln_kernel, rows=rows, n_full=n_full, tail=DB),
        grid_spec=pltpu.PrefetchScalarGridSpec(
            num_scalar_prefetch=1,
            grid=(n_full + 1,),
            in_specs=[pl.BlockSpec((rows, TOP_K), lambda i, p: (i, 0)),
                      pl.BlockSpec((rows, D), lambda i, p: (i, 0)),
                      vec, vec,
                      pl.BlockSpec(memory_space=pl.ANY)],
            out_specs=[pl.BlockSpec((rows, D), lambda i, p: (jnp.minimum(i, n_full - 1), 0)),
                       pl.BlockSpec((DB, D), lambda i, p: (0, 0))],
            scratch_shapes=[pltpu.VMEM((TOP_K, rows, D // LANES, LANES), F32),
                            pltpu.SemaphoreType.DMA(())],
        ),
        out_shape=[jax.ShapeDtypeStruct((MP, D), F32), jax.ShapeDtypeStruct((DB, D), F32)],
        compiler_params=_cparams(1),
        name="moe_combine_ln",
    )(pos, gate, x, g.reshape(1, D), b.reshape(1, D), e3)


def _routing_tables(idx, tm):
    M = idx.shape[1]
    NP = TOP_K * M
    nb = pl.cdiv(NP, LANES)
    e_flat = jnp.pad(idx.reshape(-1), (0, nb * LANES - NP), constant_values=N_EXPERTS)
    oh = (e_flat[None, :] == jnp.arange(N_EXPERTS)[:, None]).astype(F32).reshape(N_EXPERTS, nb, LANES)
    tri = jnp.asarray(np.triu(np.ones((LANES, LANES), np.float32)))
    within = jnp.einsum("ebj,ji->ebi", oh, tri, precision=lax.Precision.HIGHEST)
    blk_tot = within[..., -1]
    blk_pre = jnp.cumsum(blk_tot, axis=1) - blk_tot
    counts = (blk_pre[:, -1] + blk_tot[:, -1]).astype(jnp.int32)
    tiles_per = (counts + tm - 1) // tm
    tile_start = jnp.cumsum(tiles_per) - tiles_per
    rank = within + blk_pre[..., None] - 1.0
    pos = jnp.sum(oh * (rank + (tile_start * tm).astype(F32)[:, None, None]), axis=0)
    pos = pos.reshape(-1)[:NP].astype(jnp.int32).reshape(TOP_K, M)
    n_tiles = NP // tm + N_EXPERTS
    t_io = jnp.arange(n_tiles)
    used = jnp.sum(tiles_per)
    tile_expert = jnp.clip(jnp.sum(t_io[:, None] >= tile_start[None, :], axis=1) - 1, 0, N_EXPERTS - 1)
    tile_valid = (t_io < used).astype(jnp.int32)
    last_used_expert = jnp.max(jnp.where(tile_valid == 1, tile_expert, 0))
    tile_expert = jnp.where(tile_valid == 1, tile_expert, last_used_expert).astype(jnp.int32)
    prev = jnp.concatenate([jnp.array([-1], jnp.int32), tile_expert[:-1]])
    tile_first = (tile_expert != prev).astype(jnp.int32)
    own = jnp.arange(N_EXPERTS)[None, :] == tile_expert[:, None]
    of_tile = lambda per_expert: jnp.sum(jnp.where(own, per_expert[None, :], 0), axis=1)
    rows_left = of_tile(counts) - (t_io - of_tile(tile_start)) * tm
    tile_rows = jnp.where(tile_valid == 1, jnp.clip(rows_left, 0, tm), 0).astype(jnp.int32)
    return pos, tile_expert, tile_first, tile_rows, n_tiles * tm


def moe_ffn_ln(x, xb, router_w_l, w1, w3, w2, layer, g, b, MP):
    idx, gate = router(x, router_w_l)
    pos, tile_expert, tile_first, tile_rows, P = _routing_tables(idx, MOE_TM)
    xs = scatter_rows(xb, pos, P)
    h = moe_up(xs, w1, w3, layer, tile_expert, tile_first, tile_rows, tn=512)
    eout = moe_down(h, w2, layer, tile_expert, tile_first, tile_rows, tm=512, tn=512)
    return moe_combine_ln(eout, pos, gate.T, x, g, b, MP)


def kernel(x_prompt, x_sample, cache_k, cache_v, page_table, state_hgrn, rel_bias, w_in, w_pa, w_pb, w_out,
           lam_qk, subln_g, hgrn_g, lb_raw, ln_g, ln_b, ffn_w1, ffn_w3, ffn_w2, router_w, moe_w1, moe_w3, moe_w2):
    B, S, D = x_prompt.shape
    DB = x_sample.shape[0]
    MP = B * S
    lbs = jax.nn.softmax(lb_raw.astype(F32), axis=0)
    lbs = jnp.cumsum(lbs, axis=0) - lbs[0]
    bias_tiles = _bias_tiles_t(rel_bias, ATT_T) * LOG2E
    ck = cache_k.reshape(cache_k.shape[0], cache_k.shape[1], PAGE_ROWS, 2 * DK_A)
    cv = cache_v.reshape(cache_v.shape[0], cache_v.shape[1], PAGE_ROWS, DV_A)

    xp = x_prompt.reshape(MP, D)
    xs = x_sample.reshape(DB, D)
    x = (xp, xs)
    xb = jnp.concatenate([xp.astype(BF16), xs.astype(BF16)], axis=0)
    kp = jnp.zeros((DEPTH, MP, W_A), F32)
    vp = jnp.zeros((DEPTH, MP, W_A), F32)
    ks = jnp.zeros((DEPTH, DB, W_A), F32)
    vs = jnp.zeros((DEPTH, DB, W_A), F32)
    sp, ss = [], []
    for l in range(DEPTH):
        lam_init = 0.8 - 0.6 * math.exp(-0.3 * l)
        lq = lam_qk[l].astype(F32)
        lam = jnp.exp(jnp.sum(lq[0] * lq[1])) - jnp.exp(jnp.sum(lq[2] * lq[3])) + lam_init

        z, kp, vp, ks, vs = in_proj(xb, w_in, l, kp, vp, ks, vs, tm=1024, tn=W_A)
        zs = z[MP:]
        ya = prompt_attention(z, kp, vp, l, bias_tiles, lam, subln_g[l], lam_init, B, S)
        yb, s_p = hgrn_prompt(z, lbs[l], hgrn_g[l], B, S)
        heads = lambda a: a.reshape(DB, H_A, DV_A)
        ya_s = sample_attention(heads(zs[:, COLZ["q_a"]:COLZ["q_a"] + W_A]), heads(ks[l]), heads(vs[l]),
                                ck, cv, page_table, rel_bias, lam, subln_g[l], lam_init, l)
        colv = lambda n: zs[:, COLZ[n]:COLZ[n] + W_BK].reshape(DB, H_B, DK_B, 1)
        rowv = lambda n: zs[:, COLZ[n]:COLZ[n] + W_B].reshape(DB, H_B, 1, DV_B)
        yb_s, s_s = hgrn_step(colv("f_b"), colv("q_b"), rowv("i_b"), rowv("og_b"), lbs[l], hgrn_g[l],
                              state_hgrn, l)
        mg = branch_merge(ya, yb, ya_s.reshape(DB, W_A), yb_s.reshape(DB, W_B), z, w_pa, w_pb, l,
                          tm=1024, tn=512)
        mo = matmul(mg, w_out, l, tm=1024, tn=512)
        x, xb = residual_ln(x, mo, ln_g[l, 0], ln_b[l, 0], MP)

        j = l // 2
        if l % 2 == 0:
            hh = swiglu_up(xb, ffn_w1, ffn_w3, j, tm=1024, tn=512)
            c = matmul(hh, ffn_w2, j, tm=512, tn=512)
            x, xb = residual_ln(x, c, ln_g[l, 1], ln_b[l, 1], MP)
        else:
            assert l == DEPTH - 1
            y_prompt, y_sample = moe_ffn_ln(x, xb, router_w[j], moe_w1, moe_w3, moe_w2, j,
                                            ln_g[l, 1], ln_b[l, 1], MP)
        sp.append(s_p)
        ss.append(s_s)
    return (y_prompt.reshape(B, S, D), y_sample.reshape(DB, 1, D),
            kp.reshape(DEPTH, B, S, H_A, 2 * DK_A), vp.reshape(DEPTH, B, S, H_A, DV_A), jnp.stack(sp),
            ks.reshape(DEPTH, DB, 1, H_A, 2 * DK_A), vs.reshape(DEPTH, DB, 1, H_A, DV_A), jnp.stack(ss))
```

```python
import functools
import math

import numpy as np
import jax
import jax.numpy as jnp
from jax import lax
from jax.experimental import pallas as pl
from jax.experimental.pallas import tpu as pltpu

F32 = jnp.float32
BF16 = jnp.bfloat16

D_MODEL = 2048
DEPTH = 2
PAGE_SIZE = 128
H_A = 8
DK_A = 64
DV_A = 2 * DK_A
W_A = H_A * DV_A
H_B = 8
DK_B = 128
DV_B = 128
W_BK = H_B * DK_B
W_B = H_B * DV_B
CHUNK = 64
N_BUCKETS = 32
MAX_EXACT = N_BUCKETS // 2
MAX_DISTANCE = 128
N_EXPERTS = 8
TOP_K = 2
ALPHA = (2 * DEPTH) ** 0.25
LN_EPS = 1e-5
RMS_EPS = 1e-6
F_MIN = 1e-20
NEG_INF = -1e30
LOG2E = 1.0 / math.log(2.0)
SPLITS = [H_A * 2 * DK_A, H_A * 2 * DK_A, W_A, W_BK, W_BK, W_B, W_B, D_MODEL, D_MODEL]
N_IN = sum(SPLITS)
_NAMES = ["q_a", "k_a", "v_a", "f_b", "q_b", "i_b", "og_b", "gt_a", "gt_b"]
COL = {}
COLZ = {}
_acc = _accz = 0
for _name, _w in zip(_NAMES, SPLITS):
    COL[_name] = _acc
    _acc += _w
    if _name not in ("k_a", "v_a"):
        COLZ[_name] = _accz
        _accz += _w
N_Z = _accz

LANES = 128
SUBLANES = 8
V7X_VMEM_BYTES = 64 * 1024 * 1024
VMEM_LIMIT = V7X_VMEM_BYTES - 4 * 1024 * 1024

ATT_T = 256
ATT_HALF = LANES
ATT_HEADS = 2
MOE_TM = 1024
MOE_ROW_STEP = 128
PAGES_PER_STEP = 8
HGRN_UNROLL = 4


def _cparams(n_axes):
    return pltpu.CompilerParams(
        dimension_semantics=("arbitrary",) * n_axes, vmem_limit_bytes=VMEM_LIMIT)


def _sigmoid(x):
    return jax.nn.sigmoid(x)


def _dot_nt(a, b):
    return lax.dot_general(a, b, (((1,), (1,)), ((), ())), preferred_element_type=F32)


def _dot_tn(a, b):
    return lax.dot_general(a, b, (((0,), (0,)), ((), ())), preferred_element_type=F32)


def _for_rows(m, M, tm, fn):
    n_full, tail = M // tm, M % tm
    if n_full:
        @pl.when(m < n_full)
        def _():
            fn(slice(None))
    if tail:
        @pl.when(m == n_full)
        def _():
            fn(slice(0, tail))


def _mm_kernel(x_ref, w_ref, o_ref, wb_ref, *, M, tm):
    m = pl.program_id(1)

    @pl.when(m == 0)
    def _():
        wb_ref[...] = w_ref[...].astype(BF16)

    def body(rs):
        o_ref[rs, :] = jnp.dot(x_ref[rs, :], wb_ref[...], preferred_element_type=F32).astype(o_ref.dtype)

    _for_rows(m, M, tm, body)


def matmul(x, w, layer, *, tm, tn, out_dtype=F32):
    M, K = x.shape
    N = w.shape[2]
    assert N % tn == 0
    return pl.pallas_call(
        functools.partial(_mm_kernel, M=M, tm=tm),
        grid=(N // tn, pl.cdiv(M, tm)),
        in_specs=[
            pl.BlockSpec((tm, K), lambda n, m: (m, 0)),
            pl.BlockSpec((None, K, tn), lambda n, m: (layer, 0, n)),
        ],
        out_specs=pl.BlockSpec((tm, tn), lambda n, m: (m, n)),
        out_shape=jax.ShapeDtypeStruct((M, N), out_dtype),
        scratch_shapes=[pltpu.VMEM((K, tn), BF16)],
        compiler_params=_cparams(2),
        name="matmul",
    )(x, w)


def _in_proj_kernel(x_ref, w_ref, kp_in, vp_in, ks_in, vs_in, z_ref, kp_ref, vp_ref, ks_ref, vs_ref, wb_ref, *,
                    n_full, tail, nk, nv):
    del kp_in, vp_in, ks_in, vs_in
    n = pl.program_id(0)
    m = pl.program_id(1)

    @pl.when(m == 0)
    def _():
        wb_ref[...] = w_ref[...].astype(BF16)

    def emit(cond, full_ref, tail_ref, tail_rows):
        @pl.when(cond & (m < n_full))
        def _():
            full_ref[...] = jnp.dot(x_ref[...], wb_ref[...], preferred_element_type=F32)

        @pl.when(cond & (m == n_full))
        def _():
            tail_ref[tail_rows, :] = jnp.dot(x_ref[0:tail, :], wb_ref[...], preferred_element_type=F32)

    emit((n != nk) & (n != nv), z_ref, z_ref, slice(0, tail))
    emit(n == nk, kp_ref, ks_ref, slice(None))
    emit(n == nv, vp_ref, vs_ref, slice(None))


def in_proj(x, w_in, layer, kp, vp, ks, vs, *, tm, tn):
    M, K = x.shape
    MP, DB = kp.shape[1], ks.shape[1]
    assert tn == W_A and M == MP + DB and MP % tm == 0 and DB < tm and N_IN % tn == 0
    n_full = MP // tm
    nk, nv = COL["k_a"] // tn, COL["v_a"] // tn
    assert nv == nk + 1 and COL["q_a"] == 0

    def z_idx(n, m):
        hold = (n == nk) | (n == nv)
        return (jnp.where(hold, n_full, m), jnp.where(n < nk, n, jnp.where(hold, nk - 1, n - 2)))

    def p_idx(n_own):
        def idx(n, m):
            return (layer, jnp.where(n < n_own, 0, jnp.where(n == n_own, jnp.minimum(m, n_full - 1), n_full - 1)), 0)
        return idx

    s_idx = lambda n, m: (layer, 0, 0)
    anyspec = pl.BlockSpec(memory_space=pl.ANY)
    return pl.pallas_call(
        functools.partial(_in_proj_kernel, n_full=n_full, tail=DB, nk=nk, nv=nv),
        grid=(N_IN // tn, n_full + 1),
        in_specs=[
            pl.BlockSpec((tm, K), lambda n, m: (m, 0)),
            pl.BlockSpec((None, K, tn), lambda n, m: (layer, 0, n)),
            anyspec, anyspec, anyspec, anyspec,
        ],
        out_specs=[
            pl.BlockSpec((tm, tn), z_idx),
            pl.BlockSpec((None, tm, tn), p_idx(nk)),
            pl.BlockSpec((None, tm, tn), p_idx(nv)),
            pl.BlockSpec((None, DB, tn), s_idx),
            pl.BlockSpec((None, DB, tn), s_idx),
        ],
        out_shape=[jax.ShapeDtypeStruct((M, N_Z), F32)] + [jax.ShapeDtypeStruct(a.shape, a.dtype)
                                                            for a in (kp, vp, ks, vs)],
        input_output_aliases={2: 1, 3: 2, 4: 3, 5: 4},
        scratch_shapes=[pltpu.VMEM((K, tn), BF16)],
        compiler_params=_cparams(2),
        name="in_proj",
    )(x, w_in, kp, vp, ks, vs)


def _swiglu_kernel(x_ref, w1_ref, w3_ref, o_ref, w1b_ref, w3b_ref, *, M, tm):
    m = pl.program_id(1)

    @pl.when(m == 0)
    def _():
        w1b_ref[...] = w1_ref[...].astype(BF16)
        w3b_ref[...] = w3_ref[...].astype(BF16)

    def body(rs):
        x = x_ref[rs, :]
        a = jnp.dot(x, w1b_ref[...], preferred_element_type=F32)
        b = jnp.dot(x, w3b_ref[...], preferred_element_type=F32)
        o_ref[rs, :] = (a * _sigmoid(a) * b).astype(o_ref.dtype)

    _for_rows(m, M, tm, body)


def swiglu_up(x, w1, w3, layer, *, tm, tn):
    M, K = x.shape
    N = w1.shape[2]
    assert N % tn == 0
    return pl.pallas_call(
        functools.partial(_swiglu_kernel, M=M, tm=tm),
        grid=(N // tn, pl.cdiv(M, tm)),
        in_specs=[
            pl.BlockSpec((tm, K), lambda n, m: (m, 0)),
            pl.BlockSpec((None, K, tn), lambda n, m: (layer, 0, n)),
            pl.BlockSpec((None, K, tn), lambda n, m: (layer, 0, n)),
        ],
        out_specs=pl.BlockSpec((tm, tn), lambda n, m: (m, n)),
        out_shape=jax.ShapeDtypeStruct((M, N), BF16),
        scratch_shapes=[pltpu.VMEM((K, tn), BF16), pltpu.VMEM((K, tn), BF16)],
        compiler_params=_cparams(2),
        name="swiglu_up",
    )(x, w1, w3)


def _merge_kernel(ya_ref, yb_ref, yas_ref, ybs_ref, wa_ref, wb_ref, ga_ref, gb_ref, o_ref, wab_ref, wbb_ref, *,
                  n_full, tail):
    m = pl.program_id(1)

    @pl.when(m == 0)
    def _():
        wab_ref[...] = wa_ref[...].astype(BF16)
        wbb_ref[...] = wb_ref[...].astype(BF16)

    def merge(ya, yb, rs):
        pa = jnp.dot(ya, wab_ref[...], preferred_element_type=F32)
        pb = jnp.dot(yb, wbb_ref[...], preferred_element_type=F32)
        o_ref[rs, :] = (_sigmoid(ga_ref[rs, :]) * pa + _sigmoid(gb_ref[rs, :]) * pb).astype(o_ref.dtype)

    @pl.when(m < n_full)
    def _():
        merge(ya_ref[...], yb_ref[...], slice(None))

    @pl.when(m == n_full)
    def _():
        merge(yas_ref[...], ybs_ref[...], slice(0, tail))


def branch_merge(ya, yb, ya_s, yb_s, z, w_pa, w_pb, layer, *, tm, tn):
    MP, K = ya.shape
    DB = ya_s.shape[0]
    M = z.shape[0]
    N = w_pa.shape[2]
    assert N % tn == 0 and MP % tm == 0 and M == MP + DB and DB < tm
    n_full = MP // tm
    ga0, gb0 = COLZ["gt_a"] // tn, COLZ["gt_b"] // tn
    prow = pl.BlockSpec((tm, K), lambda n, m: (jnp.minimum(m, n_full - 1), 0))
    srow = pl.BlockSpec((DB, K), lambda n, m: (0, 0))
    return pl.pallas_call(
        functools.partial(_merge_kernel, n_full=n_full, tail=DB),
        grid=(N // tn, n_full + 1),
        in_specs=[
            prow, prow, srow, srow,
            pl.BlockSpec((None, K, tn), lambda n, m: (layer, 0, n)),
            pl.BlockSpec((None, K, tn), lambda n, m: (layer, 0, n)),
            pl.BlockSpec((tm, tn), lambda n, m: (m, ga0 + n)),
            pl.BlockSpec((tm, tn), lambda n, m: (m, gb0 + n)),
        ],
        out_specs=pl.BlockSpec((tm, tn), lambda n, m: (m, n)),
        out_shape=jax.ShapeDtypeStruct((M, N), BF16),
        scratch_shapes=[pltpu.VMEM((K, tn), BF16), pltpu.VMEM((K, tn), BF16)],
        compiler_params=_cparams(2),
        name="branch_merge",
    )(ya, yb, ya_s, yb_s, w_pa, w_pb, z, z)


def _add_ln(x, y, g, b):
    h = ALPHA * x + y
    mu = jnp.mean(h, axis=-1, keepdims=True)
    c = h - mu
    var = jnp.mean(c * c, axis=-1, keepdims=True)
    return c * lax.rsqrt(var + LN_EPS) * g + b


def _proj_ln_kernel(*refs, n_full, tail, x_split):
    refs = list(refs)
    a_ref, w_ref, xp_ref = refs[:3]
    del refs[:3]
    xs_ref = refs.pop(0) if x_split else None
    g_ref, b_ref, o_ref, ob_ref, wb_ref = refs
    m = pl.program_id(0)

    @pl.when(m == 0)
    def _():
        wb_ref[...] = w_ref[...].astype(BF16)

    def emit(a, x, rs):
        y = jnp.dot(a, wb_ref[...], preferred_element_type=F32)
        o = _add_ln(x, y, g_ref[...], b_ref[...])
        o_ref[rs, :] = o
        ob_ref[rs, :] = o.astype(BF16)

    @pl.when(m < n_full)
    def _():
        emit(a_ref[...], xp_ref[...], slice(None))

    @pl.when(m == n_full)
    def _():
        emit(a_ref[0:tail, :], xs_ref[...] if x_split else xp_ref[0:tail, :], slice(0, tail))


def proj_residual_ln(a, w, layer, x, g, b, MP, *, tm=256):
    x_split = isinstance(x, tuple)
    M, K = a.shape
    N = w.shape[2]
    DB = M - MP
    assert MP % tm == 0 and 0 < DB < tm
    n_full = MP // tm
    row = pl.BlockSpec((tm, N), lambda m: (m, 0))
    prow = pl.BlockSpec((tm, N), lambda m: (jnp.minimum(m, n_full - 1), 0))
    srow = pl.BlockSpec((DB, N), lambda m: (0, 0))
    vec = pl.BlockSpec((1, N), lambda m: (0, 0))
    x_args = list(x) if x_split else [x]
    x_specs = [prow, srow] if x_split else [row]
    return pl.pallas_call(
        functools.partial(_proj_ln_kernel, n_full=n_full, tail=DB, x_split=x_split),
        grid=(n_full + 1,),
        in_specs=[pl.BlockSpec((tm, K), lambda m: (m, 0)),
                  pl.BlockSpec((None, K, N), lambda m: (layer, 0, 0))] + x_specs + [vec, vec],
        out_specs=[row, row],
        out_shape=[jax.ShapeDtypeStruct((M, N), F32), jax.ShapeDtypeStruct((M, N), BF16)],
        scratch_shapes=[pltpu.VMEM((K, N), BF16)],
        compiler_params=_cparams(1),
        name="proj_residual_ln",
    )(a, w, *x_args, g.reshape(1, N), b.reshape(1, N))


def _res_ln_kernel(x_ref, y_ref, g_ref, b_ref, o_ref, ob_ref, *, M, tm):
    def body(rs):
        o = _add_ln(x_ref[rs, :], y_ref[rs, :], g_ref[...], b_ref[...])
        o_ref[rs, :] = o
        ob_ref[rs, :] = o.astype(BF16)

    _for_rows(pl.program_id(0), M, tm, body)


def residual_ln(x, y, g, b, *, tm=256):
    M, D = y.shape
    row = pl.BlockSpec((tm, D), lambda m: (m, 0))
    vec = pl.BlockSpec((1, D), lambda m: (0, 0))
    return pl.pallas_call(
        functools.partial(_res_ln_kernel, M=M, tm=tm),
        grid=(pl.cdiv(M, tm),),
        in_specs=[row, row, vec, vec],
        out_specs=[row, row],
        out_shape=[jax.ShapeDtypeStruct((M, D), F32), jax.ShapeDtypeStruct((M, D), BF16)],
        compiler_params=_cparams(1),
        name="residual_ln",
    )(x, y, g.reshape(1, D), b.reshape(1, D))


def _t5_bucket(rel):
    n = jnp.maximum(rel, 0)
    large = MAX_EXACT + (jnp.log(jnp.maximum(n, 1).astype(F32) / MAX_EXACT)
                         / math.log(MAX_DISTANCE / MAX_EXACT) * (N_BUCKETS - MAX_EXACT)).astype(jnp.int32)
    large = jnp.clip(large, 0, N_BUCKETS - 1)
    return jnp.where(n < MAX_EXACT, n, large)


def _bucket_np(n):
    n = np.maximum(n, 0)
    large = MAX_EXACT + (np.log(np.maximum(n, 1) / MAX_EXACT)
                         / math.log(MAX_DISTANCE / MAX_EXACT) * (N_BUCKETS - MAX_EXACT)).astype(np.int64)
    return np.where(n < MAX_EXACT, n, np.clip(large, 0, N_BUCKETS - 1))


assert (_bucket_np(np.arange(min(ATT_T, PAGE_SIZE) + 1, 1 << 16)) == N_BUCKETS - 1).all()


def _bias_tiles_t(rel_bias, T):
    n = jnp.arange(-(T - 1), 3 * T)
    vec = jnp.where((n >= 0)[:, None], rel_bias[_t5_bucket(n)].astype(F32), NEG_INF).T
    L = 2 * T
    tiles = []
    for d in range(3):
        f0 = d * T + (T - 1)
        a = jnp.concatenate([vec[:, f0:f0 + T], vec[:, :1], vec[:, f0 - (T - 1):f0]], axis=1)
        skew = jnp.tile(a, (1, T))[:, :T * (L - 1)].reshape(H_A, T, L - 1)
        tiles.append(skew[:, :, :T])
    return jnp.stack(tiles, axis=1)


def _rms_head(o, g):
    return o * lax.rsqrt(jnp.mean(o * o, axis=-1, keepdims=True) + RMS_EPS) * g


def _prompt_attn_kernel(lam_ref, q_ref, k_ref, v_ref, bias_ref, g_ref, o_ref, kb_ref, vb_ref, s_ref, p_ref, *,
                        T, out_scale):
    qi = pl.program_id(2)
    n_half = T // ATT_HALF

    @pl.when(qi == 0)
    def _():
        kb_ref[...] = k_ref[...].astype(BF16)
        vb_ref[...] = v_ref[...].astype(BF16)

    lane = lax.broadcasted_iota(jnp.int32, (1, 2 * DK_A), 1)
    qm = []
    for hd in range(ATT_HEADS):
        q = q_ref[:, hd * DV_A:(hd + 1) * DV_A] * (DK_A ** -0.5 * LOG2E)
        qm.append((jnp.where(lane < DK_A, q, 0.0).astype(BF16), jnp.where(lane >= DK_A, q, 0.0).astype(BF16)))
    pairs = [(hd, mi) for hd in range(ATT_HEADS) for mi in range(2)]
    blocks = [(pi, hi) for pi in range(len(pairs)) for hi in range(n_half)]

    def kv_rows(kj):
        return pl.ds(pl.multiple_of(kj * T, T), T)

    def head_cols(hd):
        return slice(hd * DV_A, (hd + 1) * DV_A)

    def half_cols(hi):
        return slice(hi * ATT_HALF, (hi + 1) * ATT_HALF)

    def scores(kj):
        tile = jnp.clip(qi - kj, 0, 2)
        ks = [kb_ref[kv_rows(kj), head_cols(hd)] for hd in range(ATT_HEADS)]
        return [_dot_nt(ks[hd], qm[hd][mi]) + bias_ref[hd, tile] for hd, mi in pairs]

    def pv(kj):
        vs = [vb_ref[kv_rows(kj), head_cols(hd)] for hd in range(ATT_HEADS)]
        return [_dot_tn(vs[hd], p_ref[pi]) for pi, (hd, _) in enumerate(pairs)]

    for pi, s in enumerate(scores(0)):
        s_ref[pi] = s
    p_ref[...] = jnp.zeros(p_ref.shape, p_ref.dtype)

    def step(kj, carry):
        s_next = scores(jnp.minimum(kj + 1, qi))
        pv_prev = pv(jnp.maximum(kj - 1, 0))
        out = []
        for (pi, hi), (m, l, b) in zip(blocks, carry):
            s = s_ref[pi, :, half_cols(hi)]
            m_new = jnp.maximum(m, jnp.max(s, axis=0, keepdims=True))
            p = jnp.exp2(s - m_new)
            r = jnp.exp2(m - m_new)
            l_new = r * l + jnp.sum(p, axis=0, keepdims=True)
            p_ref[pi, :, half_cols(hi)] = p.astype(BF16)
            out.append((m_new, l_new, r * (b + pv_prev[pi][:, half_cols(hi)])))
        for pi in range(len(pairs)):
            s_ref[pi] = s_next[pi]
        return tuple(out)

    init = tuple((jnp.full((1, ATT_HALF), NEG_INF, F32), jnp.zeros((1, ATT_HALF), F32),
                  jnp.zeros((DV_A, ATT_HALF), F32)) for _ in blocks)
    res = lax.fori_loop(0, qi + 1, step, init)
    pv_last = pv(qi)
    lam = lam_ref[0]

    def normalised(hd, mi, hi):
        pi = pairs.index((hd, mi))
        _, l, b = res[blocks.index((pi, hi))]
        return (b + pv_last[pi][:, half_cols(hi)]) / l

    for hd in range(ATT_HEADS):
        for hi in range(n_half):
            ot = normalised(hd, 0, hi) - lam * normalised(hd, 1, hi)
            ot = ot * lax.rsqrt(jnp.mean(ot * ot, axis=0, keepdims=True) + RMS_EPS)
            o_ref[half_cols(hi), head_cols(hd)] = (
                ot.T * g_ref[:, head_cols(hd)] * out_scale).astype(o_ref.dtype)


def prompt_attention(z, k_all, v_all, layer, bias_tiles, lam, subln_g_l, lam_init, B, S):
    T = ATT_T
    nq = S // T
    W = ATT_HEADS * DV_A
    assert COLZ["q_a"] == 0 and H_A % ATT_HEADS == 0
    n_pairs = 2 * ATT_HEADS
    kern = functools.partial(_prompt_attn_kernel, T=T, out_scale=1.0 - lam_init)
    return pl.pallas_call(
        kern,
        grid=(B, H_A // ATT_HEADS, nq),
        in_specs=[
            pl.BlockSpec(memory_space=pltpu.SMEM),
            pl.BlockSpec((T, W), lambda b, h, i: (b * nq + i, h)),
            pl.BlockSpec((None, S, W), lambda b, h, i: (layer, b, h)),
            pl.BlockSpec((None, S, W), lambda b, h, i: (layer, b, h)),
            pl.BlockSpec((ATT_HEADS, 3, T, T), lambda b, h, i: (h, 0, 0, 0)),
            pl.BlockSpec((1, W), lambda b, h, i: (0, h)),
        ],
        out_specs=pl.BlockSpec((T, W), lambda b, h, i: (b * nq + i, h)),
        out_shape=jax.ShapeDtypeStruct((B * S, W_A), BF16),
        scratch_shapes=[pltpu.VMEM((S, W), BF16), pltpu.VMEM((S, W), BF16),
                        pltpu.VMEM((n_pairs, T, T), F32),
                        pltpu.VMEM((n_pairs, T, T), BF16)],
        compiler_params=_cparams(3),
        name="prompt_attention",
    )(lam.reshape(1), z, k_all, v_all, bias_tiles, subln_g_l.reshape(1, W_A))


_LEVELS = [CHUNK >> (i + 1) for i in range(int(math.log2(CHUNK)))]


def _cumsum_matrix():
    t = np.arange(CHUNK)
    return (t[None, :] <= t[:, None]).astype(np.float32)


def _anchor_rows(b, w):
    if 2 * w >= SUBLANES:
        g = b.reshape(CHUNK // (2 * w), 2 * w, b.shape[-1])
        return jnp.broadcast_to(g[:, w - 1:w, :], g.shape).reshape(b.shape)
    t = lax.broadcasted_iota(jnp.int32, b.shape, 0)
    off = (t & (2 * w - 1)) - (w - 1)
    out = b
    for d in range(-(w - 1), w + 1):
        if d != 0:
            out = jnp.where(off == d, pltpu.roll(b, d % CHUNK, 0), out)
    return out


def _split3(x):
    x1 = x.astype(BF16)
    r1 = x - x1.astype(F32)
    x2 = r1.astype(BF16)
    x3 = (r1 - x2.astype(F32)).astype(BF16)
    return x1, x2, x3


def _hgrn_gates(fx, qx, lb):
    f = lb + (1.0 - lb) * _sigmoid(fx)
    g = jnp.log(jnp.maximum(f, F_MIN))
    k = (1.0 - lb) * _sigmoid(-fx)
    q = qx * _sigmoid(qx)
    return g, k, q


def _hgrn_prompt_kernel(f_ref, q_ref, i_ref, og_ref, lb_ref, gain_ref, pm_ref, y_ref, s_ref,
                        oi_ref, qe_ref, u_ref, dl_ref, *, S):
    n_chunks = S // CHUNK
    lb = lb_ref[...]
    gain = gain_ref[...]
    pm = pm_ref[...]
    t_io = lax.broadcasted_iota(jnp.int32, (CHUNK, CHUNK), 0)
    s_io = lax.broadcasted_iota(jnp.int32, (CHUNK, CHUNK), 1)
    masks = [(((t_io ^ s_io) < 2 * w) & ((t_io & w) != 0)) & ((s_io & w) == 0) for w in _LEVELS]
    diag = t_io == s_io

    def group(gi, carry):
        cs = [gi * HGRN_UNROLL + u for u in range(HGRN_UNROLL)]
        rows = [pl.ds(pl.multiple_of(c * CHUNK, CHUNK), CHUNK) for c in cs]
        gkq = [_hgrn_gates(f_ref[r, :], q_ref[r, :], lb) for r in rows]
        vbs = [i_ref[r, :].astype(BF16) for r in rows]
        bs = []
        for g, _, _ in gkq:
            g1, g2, g3 = _split3(g)
            bs.append(jnp.dot(pm, g1, preferred_element_type=F32)
                      + jnp.dot(pm, g2, preferred_element_type=F32)
                      + jnp.dot(pm, g3, preferred_element_type=F32))
        a_s = [jnp.where(diag, _dot_nt(q.astype(BF16), k.astype(BF16)), 0.0) for _, k, q in gkq]
        for li, w in enumerate(_LEVELS):
            for u, (_, k, q) in enumerate(gkq):
                anc = _anchor_rows(bs[u], w)
                qs = (q * jnp.exp(jnp.minimum(bs[u] - anc, 0.0))).astype(BF16)
                ks = (k * jnp.exp(jnp.minimum(anc - bs[u], 0.0))).astype(BF16)
                a_s[u] = jnp.where(masks[li], _dot_nt(qs, ks), a_s[u])
        for u, (_, k, q) in enumerate(gkq):
            b = bs[u]
            b_last = b[CHUNK - 1:CHUNK, :]
            oi_ref[rows[u], :] = jnp.dot(a_s[u].astype(BF16), vbs[u], preferred_element_type=F32)
            qe_ref[rows[u], :] = (q * jnp.exp(b)).astype(BF16)
            kd = (k * jnp.exp(b_last - b)).astype(BF16)
            u_ref[cs[u]] = _dot_tn(vbs[u], kd)
            dl_ref[cs[u]] = jnp.exp(b_last)
        return carry

    lax.fori_loop(0, n_chunks // HGRN_UNROLL, group, 0)

    def phase_b(c, st):
        rows = pl.ds(pl.multiple_of(c * CHUNK, CHUNK), CHUNK)
        o = oi_ref[rows, :] + _dot_nt(qe_ref[rows, :], st.astype(BF16))
        y = _rms_head(o, gain) * _sigmoid(og_ref[rows, :])
        y_ref[rows, :] = y.astype(y_ref.dtype)
        return st * dl_ref[c] + u_ref[c]

    st = lax.fori_loop(0, n_chunks, phase_b, jnp.zeros((DV_B, DK_B), F32), unroll=HGRN_UNROLL)
    s_ref[...] = st.T


def hgrn_prompt(z, lb_l, hgrn_g_l, B, S):
    assert (S // CHUNK) % HGRN_UNROLL == 0
    fc, qc, ic, oc = (COLZ[n] // DK_B for n in ("f_b", "q_b", "i_b", "og_b"))
    pm = jnp.asarray(_cumsum_matrix(), BF16)
    kern = functools.partial(_hgrn_prompt_kernel, S=S)
    blk = lambda c0: pl.BlockSpec((S, DK_B), lambda b, h: (b, c0 + h))
    n_chunks = S // CHUNK
    return pl.pallas_call(
        kern,
        grid=(B, H_B),
        in_specs=[
            blk(fc), blk(qc), blk(ic), blk(oc),
            pl.BlockSpec((None, 1, DK_B), lambda b, h: (h, 0, 0)),
            pl.BlockSpec((None, 1, DV_B), lambda b, h: (h, 0, 0)),
            pl.BlockSpec(pm.shape, lambda b, h: (0, 0)),
        ],
        out_specs=[
            pl.BlockSpec((S, DV_B), lambda b, h: (b, h)),
            pl.BlockSpec((None, None, DK_B, DV_B), lambda b, h: (b, h, 0, 0)),
        ],
        out_shape=[
            jax.ShapeDtypeStruct((B * S, W_B), BF16),
            jax.ShapeDtypeStruct((B, H_B, DK_B, DV_B), F32),
        ],
        scratch_shapes=[pltpu.VMEM((S, DV_B), F32), pltpu.VMEM((S, DK_B), BF16),
                        pltpu.VMEM((n_chunks, DV_B, DK_B), F32), pltpu.VMEM((n_chunks, 1, DK_B), F32)],
        compiler_params=_cparams(2),
        name="hgrn_prompt",
    )(z, z, z, z, lb_l.reshape(H_B, 1, DK_B), hgrn_g_l.reshape(H_B, 1, DV_B), pm)


PAGE_ROWS = PAGE_SIZE * H_A
assert H_A & (H_A - 1) == 0


def _paged_attn_kernel(pt_ref, lam_ref, q_ref, kn_ref, vn_ref, bfar_ref, blast_ref, bnew_ref, g_ref, *rest,
                       n_groups, out_scale):
    G = PAGES_PER_STEP
    R = 2 * H_A
    k_refs = rest[:G]
    v_refs = rest[G:2 * G]
    o_ref = rest[2 * G]
    m_ref, l_ref, a_ref = rest[2 * G + 1:]
    j = pl.program_id(1)
    lane = lax.broadcasted_iota(jnp.int32, (1, 2 * DK_A), 1)
    q = q_ref[...] * (DK_A ** -0.5)
    qf = jnp.concatenate([jnp.where(lane < DK_A, q, 0.0), jnp.where(lane >= DK_A, q, 0.0)], axis=0)
    qb = qf.astype(BF16)
    col = lax.broadcasted_iota(jnp.int32, (R, PAGE_ROWS), 1)
    row = lax.broadcasted_iota(jnp.int32, (R, PAGE_ROWS), 0)
    valid = (col & (H_A - 1)) == (row & (H_A - 1))

    @pl.when(j == 0)
    def _():
        m_ref[...] = jnp.full(m_ref.shape, NEG_INF, F32)
        l_ref[...] = jnp.zeros(l_ref.shape, F32)
        a_ref[...] = jnp.zeros(a_ref.shape, F32)

    def scores(i, bias):
        s = _dot_nt(qb, k_refs[i][...].astype(BF16)) + bias
        return jnp.where(valid, s, NEG_INF)

    def absorb(s_list, new_token):
        m_old = m_ref[...]
        m_new = m_old
        for s in s_list:
            m_new = jnp.maximum(m_new, jnp.max(s, axis=-1, keepdims=True))
        if new_token:
            kn2 = jnp.concatenate([kn_ref[...], kn_ref[...]], axis=0)
            vn2 = jnp.concatenate([vn_ref[...], vn_ref[...]], axis=0)
            s_new = jnp.sum(qf * kn2, axis=-1, keepdims=True) + bnew_ref[...]
            m_new = jnp.maximum(m_new, s_new)
        r = jnp.exp(m_old - m_new)
        l = r * l_ref[...]
        acc = r * a_ref[...]
        for i, s in enumerate(s_list):
            p = jnp.exp(s - m_new)
            l = l + jnp.sum(p, axis=-1, keepdims=True)
            acc = acc + jnp.dot(p.astype(BF16), v_refs[i][...].astype(BF16), preferred_element_type=F32)
        if new_token:
            p_new = jnp.exp(s_new - m_new)
            l = l + p_new
            acc = acc + p_new * vn2
        m_ref[...] = m_new
        l_ref[...] = l
        a_ref[...] = acc

    far = bfar_ref[...]

    @pl.when(j < n_groups - 1)
    def _():
        absorb([scores(i, far) for i in range(G)], False)

    @pl.when(j == n_groups - 1)
    def _():
        absorb([scores(i, far) for i in range(G - 1)] + [scores(G - 1, blast_ref[...])], True)
        lam = lam_ref[0]
        o = a_ref[0:H_A] / l_ref[0:H_A] - lam * (a_ref[H_A:R] / l_ref[H_A:R])
        o_ref[...] = (_rms_head(o, g_ref[...]) * out_scale).astype(o_ref.dtype)


def sample_attention(q, k_new, v_new, cache_k, cache_v, page_table, rel_bias, lam, subln_g_l,
                     lam_init, layer):
    DB = q.shape[0]
    n_pages = page_table.shape[1]
    G = PAGES_PER_STEP
    R = 2 * H_A
    assert n_pages % G == 0
    n_groups = n_pages // G
    hmap = jnp.tile(jnp.arange(H_A), 2)
    bias_of = lambda dist: rel_bias[_t5_bucket(dist)].astype(F32)
    bfar = bias_of(jnp.array(2 * PAGE_SIZE))[hmap][:, None]
    bnew = bias_of(jnp.array(0))[hmap][:, None]
    blast = jnp.broadcast_to(bias_of(PAGE_SIZE - jnp.arange(PAGE_SIZE)).reshape(1, PAGE_ROWS), (R, PAGE_ROWS))
    kern = functools.partial(_paged_attn_kernel, n_groups=n_groups, out_scale=1.0 - lam_init)

    def page_spec(i):
        return pl.BlockSpec((None, None, PAGE_ROWS, DV_A),
                            lambda b, j, pt: (layer, pt[b * n_pages + j * G + i], 0, 0))

    tok = pl.BlockSpec((None, H_A, DV_A), lambda b, j, pt: (b, 0, 0))
    const = lambda shape: pl.BlockSpec(shape, lambda b, j, pt: (0,) * len(shape))
    grid_spec = pltpu.PrefetchScalarGridSpec(
        num_scalar_prefetch=1,
        grid=(DB, n_groups),
        in_specs=[
            pl.BlockSpec(memory_space=pltpu.SMEM),
            tok, tok, tok,
            const((R, 1)), const((R, PAGE_ROWS)), const((R, 1)), const((H_A, DV_A)),
        ] + [page_spec(i) for i in range(G)] + [page_spec(i) for i in range(G)],
        out_specs=tok,
        scratch_shapes=[pltpu.VMEM((R, 1), F32), pltpu.VMEM((R, 1), F32), pltpu.VMEM((R, DV_A), F32)],
    )
    return pl.pallas_call(
        kern,
        grid_spec=grid_spec,
        out_shape=jax.ShapeDtypeStruct((DB, H_A, DV_A), BF16),
        compiler_params=_cparams(2),
        name="sample_attention",
    )(page_table.reshape(-1), lam.reshape(1), q, k_new, v_new, bfar, blast, bnew,
      subln_g_l.reshape(H_A, DV_A), *([cache_k] * G), *([cache_v] * G))


def _hgrn_step_kernel(fx_ref, qx_ref, v_ref, og_ref, lb_ref, gain_ref, s_ref, y_ref, so_ref):
    g, k, q = _hgrn_gates(fx_ref[...], qx_ref[...], lb_ref[...])
    v = v_ref[...]
    s_new = jnp.exp(g) * s_ref[...].astype(F32) + k * v
    so_ref[...] = s_new.astype(so_ref.dtype)
    o = jnp.sum(q * s_new, axis=1, keepdims=True)
    y = _rms_head(o, gain_ref[...]) * _sigmoid(og_ref[...])
    y_ref[...] = y.astype(y_ref.dtype)


def hgrn_step(fx, qx, v, og, lb_l, hgrn_g_l, state, layer):
    DB = fx.shape[0]
    col = pl.BlockSpec((None, H_B, DK_B, 1), lambda b: (b, 0, 0, 0))
    row = pl.BlockSpec((None, H_B, 1, DV_B), lambda b: (b, 0, 0, 0))
    st = pl.BlockSpec((None, H_B, DK_B, DV_B), lambda b: (b, 0, 0, 0))
    return pl.pallas_call(
        _hgrn_step_kernel,
        grid=(DB,),
        in_specs=[col, col, row, row,
                  pl.BlockSpec((H_B, DK_B, 1), lambda b: (0, 0, 0)),
                  pl.BlockSpec((H_B, 1, DV_B), lambda b: (0, 0, 0)),
                  pl.BlockSpec((None, None, H_B, DK_B, DV_B), lambda b: (layer, b, 0, 0, 0))],
        out_specs=[row, st],
        out_shape=[jax.ShapeDtypeStruct((DB, H_B, 1, DV_B), BF16),
                   jax.ShapeDtypeStruct(state.shape[1:], state.dtype)],
        compiler_params=_cparams(1),
        name="hgrn_step",
    )(fx, qx, v, og, lb_l.reshape(H_B, DK_B, 1), hgrn_g_l.reshape(H_B, 1, DV_B), state)


def _router_kernel(x_ref, w_ref, idx_ref, gate_ref):
    x1, x2, _ = _split3(x_ref[...])
    w1, w2, _ = _split3(w_ref[...])
    lt = _dot_nt(w1, x1) + (_dot_nt(w1, x2) + _dot_nt(w2, x1))
    e_io = lax.broadcasted_iota(jnp.int32, lt.shape, 0)
    m1 = jnp.max(lt, axis=0, keepdims=True)
    i1 = jnp.min(jnp.where(lt == m1, e_io, N_EXPERTS), axis=0, keepdims=True)
    lt2 = jnp.where(e_io == i1, -jnp.inf, lt)
    m2 = jnp.max(lt2, axis=0, keepdims=True)
    i2 = jnp.min(jnp.where(lt2 == m2, e_io, N_EXPERTS), axis=0, keepdims=True)
    e2 = jnp.exp(m2 - m1)
    den = 1.0 + e2
    idx_ref[0:1, :] = i1
    idx_ref[1:2, :] = i2
    gate_ref[0:1, :] = 1.0 / den
    gate_ref[1:2, :] = e2 / den


def router(x, router_w_l, *, tm=512):
    M, D = x.shape
    return pl.pallas_call(
        _router_kernel,
        grid=(pl.cdiv(M, tm),),
        in_specs=[pl.BlockSpec((tm, D), lambda m: (m, 0)),
                  pl.BlockSpec((N_EXPERTS, D), lambda m: (0, 0))],
        out_specs=[pl.BlockSpec((TOP_K, tm), lambda m: (0, m)), pl.BlockSpec((TOP_K, tm), lambda m: (0, m))],
        out_shape=[jax.ShapeDtypeStruct((TOP_K, M), jnp.int32), jax.ShapeDtypeStruct((TOP_K, M), F32)],
        compiler_params=_cparams(1),
        name="router",
    )(x, router_w_l.T)


def _scatter_rows_kernel(pos_ref, x_ref, xs_in, xs_hbm, sem, *, rows, M):
    del xs_in
    base = pl.program_id(0) * rows
    n = jnp.minimum(rows, M - base)

    def row_copy(kk, r, dst_row):
        return pltpu.make_async_copy(x_ref.at[r], xs_hbm.at[dst_row], sem)

    def issue(r, c):
        for kk in range(TOP_K):
            row_copy(kk, r, pos_ref[kk, base + r]).start()
        return c

    lax.fori_loop(0, n, issue, 0)

    def drain(r, c):
        for kk in range(TOP_K):
            row_copy(kk, r, 0).wait()
        return c

    lax.fori_loop(0, n, drain, 0)


def scatter_rows(x, pos, P, *, rows=256):
    M, D = x.shape
    Mp = pl.cdiv(M, rows) * rows
    x3 = x.reshape(M, D // LANES, LANES)
    pos = jnp.pad(pos, ((0, 0), (0, Mp - M)))
    xs0 = jnp.zeros((P, D // LANES, LANES), x.dtype)
    out = pl.pallas_call(
        functools.partial(_scatter_rows_kernel, rows=rows, M=M),
        grid_spec=pltpu.PrefetchScalarGridSpec(
            num_scalar_prefetch=1,
            grid=(Mp // rows,),
            in_specs=[pl.BlockSpec((rows, D // LANES, LANES), lambda i, p: (i, 0, 0)),
                      pl.BlockSpec(memory_space=pl.ANY)],
            out_specs=pl.BlockSpec(memory_space=pl.ANY),
            scratch_shapes=[pltpu.SemaphoreType.DMA(())],
        ),
        out_shape=jax.ShapeDtypeStruct(xs0.shape, x.dtype),
        input_output_aliases={2: 0},
        compiler_params=_cparams(1),
        name="scatter_rows",
    )(pos, x3, xs0)
    return out.reshape(P, D)


def _for_used_rows(n_rows, tm, o_ref, fn):
    n_steps = (n_rows + MOE_ROW_STEP - 1) // MOE_ROW_STEP
    for k in range(tm // MOE_ROW_STEP + 1):
        @pl.when(n_steps == k)
        def _(k=k):
            r = k * MOE_ROW_STEP
            if r:
                o_ref[0:r, :] = fn(slice(0, r)).astype(o_ref.dtype)
            if r < tm:
                o_ref[r:tm, :] = jnp.zeros((tm - r, o_ref.shape[1]), o_ref.dtype)


def _moe_up_kernel(te_ref, tf_ref, tr_ref, x_ref, w1_ref, w3_ref, o_ref, w1b_ref, w3b_ref):
    t = pl.program_id(1)

    @pl.when(tf_ref[t] == 1)
    def _():
        w1b_ref[...] = w1_ref[...].astype(BF16)
        w3b_ref[...] = w3_ref[...].astype(BF16)

    def swiglu(rs):
        x = x_ref[rs, :]
        a = jnp.dot(x, w1b_ref[...], preferred_element_type=F32)
        b = jnp.dot(x, w3b_ref[...], preferred_element_type=F32)
        return a * _sigmoid(a) * b

    _for_used_rows(tr_ref[t], MOE_TM, o_ref, swiglu)


def moe_up(xs, w1, w3, layer, tile_expert, tile_first, tile_rows, *, tn):
    P, K = xs.shape
    N = w1.shape[3]
    tm = MOE_TM
    nt = P // tm
    wspec = pl.BlockSpec((None, None, K, tn), lambda n, t, te, tf, tr: (layer, te[t], 0, n))
    return pl.pallas_call(
        _moe_up_kernel,
        grid_spec=pltpu.PrefetchScalarGridSpec(
            num_scalar_prefetch=3,
            grid=(N // tn, nt),
            in_specs=[pl.BlockSpec((tm, K), lambda n, t, te, tf, tr: (t, 0)), wspec, wspec],
            out_specs=pl.BlockSpec((tm, tn), lambda n, t, te, tf, tr: (t, n)),
            scratch_shapes=[pltpu.VMEM((K, tn), BF16), pltpu.VMEM((K, tn), BF16)],
        ),
        out_shape=jax.ShapeDtypeStruct((P, N), BF16),
        compiler_params=_cparams(2),
        name="moe_up",
    )(tile_expert, tile_first, tile_rows, xs, w1, w3)


def _moe_down_kernel(te_ref, tf_ref, tr_ref, h_ref, w_ref, o_ref, wb_ref, *, sub, tm):
    t = pl.program_id(1)
    tt = t // sub
    part = t % sub

    @pl.when((tf_ref[tt] == 1) & (part == 0))
    def _():
        wb_ref[...] = w_ref[...].astype(BF16)

    n_rows = jnp.clip(tr_ref[tt] - part * tm, 0, tm)
    _for_used_rows(n_rows, tm, o_ref,
                   lambda rs: jnp.dot(h_ref[rs, :], wb_ref[...], preferred_element_type=F32))


def moe_down(h, w2, layer, tile_expert, tile_first, tile_rows, *, tm, tn):
    P, K = h.shape
    N = w2.shape[3]
    assert MOE_TM % tm == 0 and tm % MOE_ROW_STEP == 0
    sub = MOE_TM // tm
    return pl.pallas_call(
        functools.partial(_moe_down_kernel, sub=sub, tm=tm),
        grid_spec=pltpu.PrefetchScalarGridSpec(
            num_scalar_prefetch=3,
            grid=(N // tn, P // tm),
            in_specs=[pl.BlockSpec((tm, K), lambda n, t, te, tf, tr: (t, 0)),
                      pl.BlockSpec((None, None, K, tn), lambda n, t, te, tf, tr: (layer, te[t // sub], 0, n))],
            out_specs=pl.BlockSpec((tm, tn), lambda n, t, te, tf, tr: (t, n)),
            scratch_shapes=[pltpu.VMEM((K, tn), BF16)],
        ),
        out_shape=jax.ShapeDtypeStruct((P, N), F32),
        compiler_params=_cparams(2),
        name="moe_down",
    )(tile_expert, tile_first, tile_rows, h, w2)


def _moe_combine_ln_kernel(pos_ref, gate_ref, x_ref, g_ref, b_ref, e_hbm, op_ref, os_ref, buf_ref, sem, *,
                           rows, n_full, tail):
    i = pl.program_id(0)
    base = i * rows
    n = jnp.where(i < n_full, rows, tail)

    def row_copy(kk, r, src_row):
        return pltpu.make_async_copy(e_hbm.at[src_row], buf_ref.at[kk, r], sem)

    def issue(r, c):
        for kk in range(TOP_K):
            row_copy(kk, r, pos_ref[kk, base + r]).start()
        return c

    lax.fori_loop(0, n, issue, 0)

    def drain(r, c):
        for kk in range(TOP_K):
            row_copy(kk, r, 0).wait()
        return c

    lax.fori_loop(0, n, drain, 0)

    def finish(nr, out_ref):
        g0 = gate_ref[0:nr, 0:1]
        g1 = gate_ref[0:nr, 1:2]
        y = jnp.concatenate([g0 * buf_ref[0, 0:nr, j, :] + g1 * buf_ref[1, 0:nr, j, :]
                             for j in range(buf_ref.shape[2])], axis=-1)
        out_ref[...] = _add_ln(x_ref[0:nr, :], y, g_ref[...], b_ref[...])

    @pl.when(i < n_full)
    def _():
        finish(rows, op_ref)

    @pl.when(i == n_full)
    def _():
        finish(tail, os_ref)


def moe_combine_ln(eout, pos, gate, x, g, b, MP, *, rows=128):
    M, D = x.shape
    P = eout.shape[0]
    DB = M - MP
    assert MP % rows == 0 and 0 < DB < rows
    n_full = MP // rows
    e3 = eout.reshape(P, D // LANES, LANES)
    pos = jnp.pad(pos, ((0, 0), (0, (n_full + 1) * rows - M)))
    vec = pl.BlockSpec((1, D), lambda i, p: (0, 0))
    return pl.pallas_call(
        functools.partial(_moe_combine_ln_kernel, rows=rows, n_full=n_full, tail=DB),
        grid_spec=pltpu.PrefetchScalarGridSpec(
            num_scalar_prefetch=1,
            grid=(n_full + 1,),
            in_specs=[pl.BlockSpec((rows, TOP_K), lambda i, p: (i, 0)),
                      pl.BlockSpec((rows, D), lambda i, p: (i, 0)),
                      vec, vec,
                      pl.BlockSpec(memory_space=pl.ANY)],
            out_specs=[pl.BlockSpec((rows, D), lambda i, p: (jnp.minimum(i, n_full - 1), 0)),
                       pl.BlockSpec((DB, D), lambda i, p: (0, 0))],
            scratch_shapes=[pltpu.VMEM((TOP_K, rows, D // LANES, LANES), F32),
                            pltpu.SemaphoreType.DMA(())],
        ),
        out_shape=[jax.ShapeDtypeStruct((MP, D), F32), jax.ShapeDtypeStruct((DB, D), F32)],
        compiler_params=_cparams(1),
        name="moe_combine_ln",
    )(pos, gate, x, g.reshape(1, D), b.reshape(1, D), e3)


def _routing_tables(idx, tm):
    M = idx.shape[1]
    NP = TOP_K * M
    nb = pl.cdiv(NP, LANES)
    e_flat = jnp.pad(idx.reshape(-1), (0, nb * LANES - NP), constant_values=N_EXPERTS)
    oh = (e_flat[None, :] == jnp.arange(N_EXPERTS)[:, None]).astype(F32).reshape(N_EXPERTS, nb, LANES)
    tri = jnp.asarray(np.triu(np.ones((LANES, LANES), np.float32)))
    within = jnp.einsum("ebj,ji->ebi", oh, tri, precision=lax.Precision.HIGHEST)
    blk_tot = within[..., -1]
    blk_pre = jnp.cumsum(blk_tot, axis=1) - blk_tot
    counts = (blk_pre[:, -1] + blk_tot[:, -1]).astype(jnp.int32)
    tiles_per = (counts + tm - 1) // tm
    tile_start = jnp.cumsum(tiles_per) - tiles_per
    rem = counts - (tiles_per - 1) * tm
    rank = within + blk_pre[..., None] - 1.0
    remf = rem.astype(F32)[:, None, None]
    row_in_expert = jnp.where(rank < remf, rank, rank - remf + tm)
    pos = jnp.sum(oh * (row_in_expert + (tile_start * tm).astype(F32)[:, None, None]), axis=0)
    pos = pos.reshape(-1)[:NP].astype(jnp.int32).reshape(TOP_K, M)
    n_tiles = NP // tm + N_EXPERTS
    t_io = jnp.arange(n_tiles)
    used = jnp.sum(tiles_per)
    tile_expert = jnp.clip(jnp.sum(t_io[:, None] >= tile_start[None, :], axis=1) - 1, 0, N_EXPERTS - 1)
    tile_valid = (t_io < used).astype(jnp.int32)
    last_used_expert = jnp.max(jnp.where(tile_valid == 1, tile_expert, 0))
    tile_expert = jnp.where(tile_valid == 1, tile_expert, last_used_expert).astype(jnp.int32)
    prev = jnp.concatenate([jnp.array([-1], jnp.int32), tile_expert[:-1]])
    tile_first = (tile_expert != prev).astype(jnp.int32)
    own = jnp.arange(N_EXPERTS)[None, :] == tile_expert[:, None]
    of_tile = lambda per_expert: jnp.sum(jnp.where(own, per_expert[None, :], 0), axis=1)
    is_first_of_expert = t_io == of_tile(tile_start)
    tile_rows = jnp.where(tile_valid == 1, jnp.where(is_first_of_expert, of_tile(rem), tm), 0).astype(jnp.int32)
    return pos, tile_expert, tile_first, tile_rows, n_tiles * tm


def moe_ffn_ln(x, xb, router_w_l, w1, w3, w2, layer, g, b, MP):
    idx, gate = router(x, router_w_l)
    pos, tile_expert, tile_first, tile_rows, P = _routing_tables(idx, MOE_TM)
    xs = scatter_rows(xb, pos, P)
    h = moe_up(xs, w1, w3, layer, tile_expert, tile_first, tile_rows, tn=512)
    eout = moe_down(h, w2, layer, tile_expert, tile_first, tile_rows, tm=512, tn=512)
    return moe_combine_ln(eout, pos, gate.T, x, g, b, MP)


def kernel(x_prompt, x_sample, cache_k, cache_v, page_table, state_hgrn, rel_bias, w_in, w_pa, w_pb, w_out,
           lam_qk, subln_g, hgrn_g, lb_raw, ln_g, ln_b, ffn_w1, ffn_w3, ffn_w2, router_w, moe_w1, moe_w3, moe_w2):
    B, S, D = x_prompt.shape
    DB = x_sample.shape[0]
    MP = B * S
    lbs = jax.nn.softmax(lb_raw.astype(F32), axis=0)
    lbs = jnp.cumsum(lbs, axis=0) - lbs[0]
    bias_tiles = _bias_tiles_t(rel_bias, ATT_T) * LOG2E
    ck = cache_k.reshape(cache_k.shape[0], cache_k.shape[1], PAGE_ROWS, 2 * DK_A)
    cv = cache_v.reshape(cache_v.shape[0], cache_v.shape[1], PAGE_ROWS, DV_A)

    xp = x_prompt.reshape(MP, D)
    xs = x_sample.reshape(DB, D)
    x = (xp, xs)
    xb = jnp.concatenate([xp.astype(BF16), xs.astype(BF16)], axis=0)
    kp = jnp.zeros((DEPTH, MP, W_A), F32)
    vp = jnp.zeros((DEPTH, MP, W_A), F32)
    ks = jnp.zeros((DEPTH, DB, W_A), F32)
    vs = jnp.zeros((DEPTH, DB, W_A), F32)
    sp, ss = [], []
    for l in range(DEPTH):
        lam_init = 0.8 - 0.6 * math.exp(-0.3 * l)
        lq = lam_qk[l].astype(F32)
        lam = jnp.exp(jnp.sum(lq[0] * lq[1])) - jnp.exp(jnp.sum(lq[2] * lq[3])) + lam_init

        z, kp, vp, ks, vs = in_proj(xb, w_in, l, kp, vp, ks, vs, tm=1024, tn=W_A)
        zs = z[MP:]
        ya = prompt_attention(z, kp, vp, l, bias_tiles, lam, subln_g[l], lam_init, B, S)
        yb, s_p = hgrn_prompt(z, lbs[l], hgrn_g[l], B, S)
        heads = lambda a: a.reshape(DB, H_A, DV_A)
        ya_s = sample_attention(heads(zs[:, COLZ["q_a"]:COLZ["q_a"] + W_A]), heads(ks[l]), heads(vs[l]),
                                ck, cv, page_table, rel_bias, lam, subln_g[l], lam_init, l)
        colv = lambda n: zs[:, COLZ[n]:COLZ[n] + W_BK].reshape(DB, H_B, DK_B, 1)
        rowv = lambda n: zs[:, COLZ[n]:COLZ[n] + W_B].reshape(DB, H_B, 1, DV_B)
        yb_s, s_s = hgrn_step(colv("f_b"), colv("q_b"), rowv("i_b"), rowv("og_b"), lbs[l], hgrn_g[l],
                              state_hgrn, l)
        mg = branch_merge(ya, yb, ya_s.reshape(DB, W_A), yb_s.reshape(DB, W_B), z, w_pa, w_pb, l,
                          tm=1024, tn=512)
        x, xb = proj_residual_ln(mg, w_out, l, x, ln_g[l, 0], ln_b[l, 0], MP)

        j = l // 2
        if l % 2 == 0:
            hh = swiglu_up(xb, ffn_w1, ffn_w3, j, tm=1024, tn=512)
            c = matmul(hh, ffn_w2, j, tm=512, tn=512)
            x, xb = residual_ln(x, c, ln_g[l, 1], ln_b[l, 1])
        else:
            assert l == DEPTH - 1
            y_prompt, y_sample = moe_ffn_ln(x, xb, router_w[j], moe_w1, moe_w3, moe_w2, j,
                                            ln_g[l, 1], ln_b[l, 1], MP)
        sp.append(s_p)
        ss.append(s_s)
    return (y_prompt.reshape(B, S, D), y_sample.reshape(DB, 1, D),
            kp.reshape(DEPTH, B, S, H_A, 2 * DK_A), vp.reshape(DEPTH, B, S, H_A, DV_A), jnp.stack(sp),
            ks.reshape(DEPTH, DB, 1, H_A, 2 * DK_A), vs.reshape(DEPTH, DB, 1, H_A, DV_A), jnp.stack(ss))
```

```python
import functools
import math

import numpy as np
import jax
import jax.numpy as jnp
from jax import lax
from jax.experimental import pallas as pl
from jax.experimental.pallas import tpu as pltpu

F32 = jnp.float32
BF16 = jnp.bfloat16

D_MODEL = 2048
DEPTH = 2
PAGE_SIZE = 128
H_A = 8
DK_A = 64
DV_A = 2 * DK_A
W_A = H_A * DV_A
H_B = 8
DK_B = 128
DV_B = 128
W_BK = H_B * DK_B
W_B = H_B * DV_B
CHUNK = 64
N_BUCKETS = 32
MAX_EXACT = N_BUCKETS // 2
MAX_DISTANCE = 128
N_EXPERTS = 8
TOP_K = 2
ALPHA = (2 * DEPTH) ** 0.25
LN_EPS = 1e-5
RMS_EPS = 1e-6
F_MIN = 1e-20
NEG_INF = -1e30
LOG2E = 1.0 / math.log(2.0)
SPLITS = [H_A * 2 * DK_A, H_A * 2 * DK_A, W_A, W_BK, W_BK, W_B, W_B, D_MODEL, D_MODEL]
N_IN = sum(SPLITS)
_NAMES = ["q_a", "k_a", "v_a", "f_b", "q_b", "i_b", "og_b", "gt_a", "gt_b"]
COL = {}
COLZ = {}
_acc = _accz = 0
for _name, _w in zip(_NAMES, SPLITS):
    COL[_name] = _acc
    _acc += _w
    if _name not in ("k_a", "v_a"):
        COLZ[_name] = _accz
        _accz += _w
N_Z = _accz

LANES = 128
SUBLANES = 8
V7X_VMEM_BYTES = 64 * 1024 * 1024
VMEM_LIMIT = V7X_VMEM_BYTES - 4 * 1024 * 1024

ATT_T = 256
ATT_HALF = LANES
ATT_HEADS = 2
MOE_TM = 1024
MOE_ROW_STEP = 128
PAGES_PER_STEP = 8
HGRN_UNROLL = 4


def _cparams(n_axes):
    return pltpu.CompilerParams(
        dimension_semantics=("arbitrary",) * n_axes, vmem_limit_bytes=VMEM_LIMIT)


def _sigmoid(x):
    return jax.nn.sigmoid(x)


def _dot_nt(a, b):
    return lax.dot_general(a, b, (((1,), (1,)), ((), ())), preferred_element_type=F32)


def _dot_tn(a, b):
    return lax.dot_general(a, b, (((0,), (0,)), ((), ())), preferred_element_type=F32)


def _for_rows(m, M, tm, fn):
    n_full, tail = M // tm, M % tm
    if n_full:
        @pl.when(m < n_full)
        def _():
            fn(slice(None))
    if tail:
        @pl.when(m == n_full)
        def _():
            fn(slice(0, tail))


def _mm_kernel(x_ref, w_ref, o_ref, wb_ref, *, M, tm):
    m = pl.program_id(1)

    @pl.when(m == 0)
    def _():
        wb_ref[...] = w_ref[...].astype(BF16)

    def body(rs):
        o_ref[rs, :] = jnp.dot(x_ref[rs, :], wb_ref[...], preferred_element_type=F32).astype(o_ref.dtype)

    _for_rows(m, M, tm, body)


def matmul(x, w, layer, *, tm, tn, out_dtype=F32):
    M, K = x.shape
    N = w.shape[2]
    assert N % tn == 0
    return pl.pallas_call(
        functools.partial(_mm_kernel, M=M, tm=tm),
        grid=(N // tn, pl.cdiv(M, tm)),
        in_specs=[
            pl.BlockSpec((tm, K), lambda n, m: (m, 0)),
            pl.BlockSpec((None, K, tn), lambda n, m: (layer, 0, n)),
        ],
        out_specs=pl.BlockSpec((tm, tn), lambda n, m: (m, n)),
        out_shape=jax.ShapeDtypeStruct((M, N), out_dtype),
        scratch_shapes=[pltpu.VMEM((K, tn), BF16)],
        compiler_params=_cparams(2),
        name="matmul",
    )(x, w)


def _in_proj_kernel(x_ref, w_ref, kp_in, vp_in, ks_in, vs_in, z_ref, kp_ref, vp_ref, ks_ref, vs_ref, wb_ref, *,
                    n_full, tail, nk, nv):
    del kp_in, vp_in, ks_in, vs_in
    n = pl.program_id(0)
    m = pl.program_id(1)

    @pl.when(m == 0)
    def _():
        wb_ref[...] = w_ref[...].astype(BF16)

    def emit(cond, full_ref, tail_ref, tail_rows):
        @pl.when(cond & (m < n_full))
        def _():
            full_ref[...] = jnp.dot(x_ref[...], wb_ref[...], preferred_element_type=F32)

        @pl.when(cond & (m == n_full))
        def _():
            tail_ref[tail_rows, :] = jnp.dot(x_ref[0:tail, :], wb_ref[...], preferred_element_type=F32)

    emit((n != nk) & (n != nv), z_ref, z_ref, slice(0, tail))
    emit(n == nk, kp_ref, ks_ref, slice(None))
    emit(n == nv, vp_ref, vs_ref, slice(None))


def in_proj(x, w_in, layer, kp, vp, ks, vs, *, tm, tn):
    M, K = x.shape
    MP, DB = kp.shape[1], ks.shape[1]
    assert tn == W_A and M == MP + DB and MP % tm == 0 and DB < tm and N_IN % tn == 0
    n_full = MP // tm
    nk, nv = COL["k_a"] // tn, COL["v_a"] // tn
    assert nv == nk + 1 and COL["q_a"] == 0

    def z_idx(n, m):
        hold = (n == nk) | (n == nv)
        return (jnp.where(hold, n_full, m), jnp.where(n < nk, n, jnp.where(hold, nk - 1, n - 2)))

    def p_idx(n_own):
        def idx(n, m):
            return (layer, jnp.where(n < n_own, 0, jnp.where(n == n_own, jnp.minimum(m, n_full - 1), n_full - 1)), 0)
        return idx

    s_idx = lambda n, m: (layer, 0, 0)
    anyspec = pl.BlockSpec(memory_space=pl.ANY)
    return pl.pallas_call(
        functools.partial(_in_proj_kernel, n_full=n_full, tail=DB, nk=nk, nv=nv),
        grid=(N_IN // tn, n_full + 1),
        in_specs=[
            pl.BlockSpec((tm, K), lambda n, m: (m, 0)),
            pl.BlockSpec((None, K, tn), lambda n, m: (layer, 0, n)),
            anyspec, anyspec, anyspec, anyspec,
        ],
        out_specs=[
            pl.BlockSpec((tm, tn), z_idx),
            pl.BlockSpec((None, tm, tn), p_idx(nk)),
            pl.BlockSpec((None, tm, tn), p_idx(nv)),
            pl.BlockSpec((None, DB, tn), s_idx),
            pl.BlockSpec((None, DB, tn), s_idx),
        ],
        out_shape=[jax.ShapeDtypeStruct((M, N_Z), F32)] + [jax.ShapeDtypeStruct(a.shape, a.dtype)
                                                            for a in (kp, vp, ks, vs)],
        input_output_aliases={2: 1, 3: 2, 4: 3, 5: 4},
        scratch_shapes=[pltpu.VMEM((K, tn), BF16)],
        compiler_params=_cparams(2),
        name="in_proj",
    )(x, w_in, kp, vp, ks, vs)


def _swiglu_kernel(x_ref, w1_ref, w3_ref, o_ref, w1b_ref, w3b_ref, *, M, tm):
    m = pl.program_id(1)

    @pl.when(m == 0)
    def _():
        w1b_ref[...] = w1_ref[...].astype(BF16)
        w3b_ref[...] = w3_ref[...].astype(BF16)

    def body(rs):
        x = x_ref[rs, :]
        a = jnp.dot(x, w1b_ref[...], preferred_element_type=F32)
        b = jnp.dot(x, w3b_ref[...], preferred_element_type=F32)
        o_ref[rs, :] = (a * _sigmoid(a) * b).astype(o_ref.dtype)

    _for_rows(m, M, tm, body)


def swiglu_up(x, w1, w3, layer, *, tm, tn):
    M, K = x.shape
    N = w1.shape[2]
    assert N % tn == 0
    return pl.pallas_call(
        functools.partial(_swiglu_kernel, M=M, tm=tm),
        grid=(N // tn, pl.cdiv(M, tm)),
        in_specs=[
            pl.BlockSpec((tm, K), lambda n, m: (m, 0)),
            pl.BlockSpec((None, K, tn), lambda n, m: (layer, 0, n)),
            pl.BlockSpec((None, K, tn), lambda n, m: (layer, 0, n)),
        ],
        out_specs=pl.BlockSpec((tm, tn), lambda n, m: (m, n)),
        out_shape=jax.ShapeDtypeStruct((M, N), BF16),
        scratch_shapes=[pltpu.VMEM((K, tn), BF16), pltpu.VMEM((K, tn), BF16)],
        compiler_params=_cparams(2),
        name="swiglu_up",
    )(x, w1, w3)


def _merge_kernel(ya_ref, yb_ref, yas_ref, ybs_ref, wa_ref, wb_ref, ga_ref, gb_ref, o_ref, wab_ref, wbb_ref, *,
                  n_full, tail):
    m = pl.program_id(1)

    @pl.when(m == 0)
    def _():
        wab_ref[...] = wa_ref[...].astype(BF16)
        wbb_ref[...] = wb_ref[...].astype(BF16)

    def merge(ya, yb, rs):
        pa = jnp.dot(ya, wab_ref[...], preferred_element_type=F32)
        pb = jnp.dot(yb, wbb_ref[...], preferred_element_type=F32)
        o_ref[rs, :] = (_sigmoid(ga_ref[rs, :]) * pa + _sigmoid(gb_ref[rs, :]) * pb).astype(o_ref.dtype)

    @pl.when(m < n_full)
    def _():
        merge(ya_ref[...], yb_ref[...], slice(None))

    @pl.when(m == n_full)
    def _():
        merge(yas_ref[...], ybs_ref[...], slice(0, tail))


def branch_merge(ya, yb, ya_s, yb_s, z, w_pa, w_pb, layer, *, tm, tn):
    MP, K = ya.shape
    DB = ya_s.shape[0]
    M = z.shape[0]
    N = w_pa.shape[2]
    assert N % tn == 0 and MP % tm == 0 and M == MP + DB and DB < tm
    n_full = MP // tm
    ga0, gb0 = COLZ["gt_a"] // tn, COLZ["gt_b"] // tn
    prow = pl.BlockSpec((tm, K), lambda n, m: (jnp.minimum(m, n_full - 1), 0))
    srow = pl.BlockSpec((DB, K), lambda n, m: (0, 0))
    return pl.pallas_call(
        functools.partial(_merge_kernel, n_full=n_full, tail=DB),
        grid=(N // tn, n_full + 1),
        in_specs=[
            prow, prow, srow, srow,
            pl.BlockSpec((None, K, tn), lambda n, m: (layer, 0, n)),
            pl.BlockSpec((None, K, tn), lambda n, m: (layer, 0, n)),
            pl.BlockSpec((tm, tn), lambda n, m: (m, ga0 + n)),
            pl.BlockSpec((tm, tn), lambda n, m: (m, gb0 + n)),
        ],
        out_specs=pl.BlockSpec((tm, tn), lambda n, m: (m, n)),
        out_shape=jax.ShapeDtypeStruct((M, N), BF16),
        scratch_shapes=[pltpu.VMEM((K, tn), BF16), pltpu.VMEM((K, tn), BF16)],
        compiler_params=_cparams(2),
        name="branch_merge",
    )(ya, yb, ya_s, yb_s, w_pa, w_pb, z, z)


def _add_ln(x, y, g, b):
    h = ALPHA * x + y
    mu = jnp.mean(h, axis=-1, keepdims=True)
    c = h - mu
    var = jnp.mean(c * c, axis=-1, keepdims=True)
    return c * lax.rsqrt(var + LN_EPS) * g + b


def _stack_bf16_kernel(xp_ref, xs_ref, o_ref, *, n_full, tail):
    m = pl.program_id(0)

    @pl.when(m < n_full)
    def _():
        o_ref[...] = xp_ref[...].astype(BF16)

    @pl.when(m == n_full)
    def _():
        o_ref[0:tail, :] = xs_ref[...].astype(BF16)


def stack_rows_bf16(xp, xs, *, tm=512):
    MP, D = xp.shape
    DB = xs.shape[0]
    assert MP % tm == 0 and 0 < DB < tm
    n_full = MP // tm
    return pl.pallas_call(
        functools.partial(_stack_bf16_kernel, n_full=n_full, tail=DB),
        grid=(n_full + 1,),
        in_specs=[pl.BlockSpec((tm, D), lambda m: (jnp.minimum(m, n_full - 1), 0)),
                  pl.BlockSpec((DB, D), lambda m: (0, 0))],
        out_specs=pl.BlockSpec((tm, D), lambda m: (m, 0)),
        out_shape=jax.ShapeDtypeStruct((MP + DB, D), BF16),
        compiler_params=_cparams(1),
        name="stack_rows_bf16",
    )(xp, xs)


def _proj_ln_kernel(*refs, n_full, tail, x_split):
    refs = list(refs)
    a_ref, w_ref, xp_ref = refs[:3]
    del refs[:3]
    xs_ref = refs.pop(0) if x_split else None
    g_ref, b_ref, o_ref, ob_ref, wb_ref = refs
    m = pl.program_id(0)

    @pl.when(m == 0)
    def _():
        wb_ref[...] = w_ref[...].astype(BF16)

    def emit(a, x, rs):
        y = jnp.dot(a, wb_ref[...], preferred_element_type=F32)
        o = _add_ln(x, y, g_ref[...], b_ref[...])
        o_ref[rs, :] = o
        ob_ref[rs, :] = o.astype(BF16)

    @pl.when(m < n_full)
    def _():
        emit(a_ref[...], xp_ref[...], slice(None))

    @pl.when(m == n_full)
    def _():
        emit(a_ref[0:tail, :], xs_ref[...] if x_split else xp_ref[0:tail, :], slice(0, tail))


def proj_residual_ln(a, w, layer, x, g, b, MP, *, tm=256):
    x_split = isinstance(x, tuple)
    M, K = a.shape
    N = w.shape[2]
    DB = M - MP
    assert MP % tm == 0 and 0 < DB < tm
    n_full = MP // tm
    row = pl.BlockSpec((tm, N), lambda m: (m, 0))
    prow = pl.BlockSpec((tm, N), lambda m: (jnp.minimum(m, n_full - 1), 0))
    srow = pl.BlockSpec((DB, N), lambda m: (0, 0))
    vec = pl.BlockSpec((1, N), lambda m: (0, 0))
    x_args = list(x) if x_split else [x]
    x_specs = [prow, srow] if x_split else [row]
    return pl.pallas_call(
        functools.partial(_proj_ln_kernel, n_full=n_full, tail=DB, x_split=x_split),
        grid=(n_full + 1,),
        in_specs=[pl.BlockSpec((tm, K), lambda m: (m, 0)),
                  pl.BlockSpec((None, K, N), lambda m: (layer, 0, 0))] + x_specs + [vec, vec],
        out_specs=[row, row],
        out_shape=[jax.ShapeDtypeStruct((M, N), F32), jax.ShapeDtypeStruct((M, N), BF16)],
        scratch_shapes=[pltpu.VMEM((K, N), BF16)],
        compiler_params=_cparams(1),
        name="proj_residual_ln",
    )(a, w, *x_args, g.reshape(1, N), b.reshape(1, N))


def _res_ln_kernel(x_ref, y_ref, g_ref, b_ref, o_ref, ob_ref, *, M, tm):
    def body(rs):
        o = _add_ln(x_ref[rs, :], y_ref[rs, :], g_ref[...], b_ref[...])
        o_ref[rs, :] = o
        ob_ref[rs, :] = o.astype(BF16)

    _for_rows(pl.program_id(0), M, tm, body)


def residual_ln(x, y, g, b, *, tm=256):
    M, D = y.shape
    row = pl.BlockSpec((tm, D), lambda m: (m, 0))
    vec = pl.BlockSpec((1, D), lambda m: (0, 0))
    return pl.pallas_call(
        functools.partial(_res_ln_kernel, M=M, tm=tm),
        grid=(pl.cdiv(M, tm),),
        in_specs=[row, row, vec, vec],
        out_specs=[row, row],
        out_shape=[jax.ShapeDtypeStruct((M, D), F32), jax.ShapeDtypeStruct((M, D), BF16)],
        compiler_params=_cparams(1),
        name="residual_ln",
    )(x, y, g.reshape(1, D), b.reshape(1, D))


def _t5_bucket(rel):
    n = jnp.maximum(rel, 0)
    large = MAX_EXACT + (jnp.log(jnp.maximum(n, 1).astype(F32) / MAX_EXACT)
                         / math.log(MAX_DISTANCE / MAX_EXACT) * (N_BUCKETS - MAX_EXACT)).astype(jnp.int32)
    large = jnp.clip(large, 0, N_BUCKETS - 1)
    return jnp.where(n < MAX_EXACT, n, large)


def _bucket_np(n):
    n = np.maximum(n, 0)
    large = MAX_EXACT + (np.log(np.maximum(n, 1) / MAX_EXACT)
                         / math.log(MAX_DISTANCE / MAX_EXACT) * (N_BUCKETS - MAX_EXACT)).astype(np.int64)
    return np.where(n < MAX_EXACT, n, np.clip(large, 0, N_BUCKETS - 1))


assert (_bucket_np(np.arange(min(ATT_T, PAGE_SIZE) + 1, 1 << 16)) == N_BUCKETS - 1).all()


def _bias_tiles_t(rel_bias, T):
    n = jnp.arange(-(T - 1), 3 * T)
    vec = jnp.where((n >= 0)[:, None], rel_bias[_t5_bucket(n)].astype(F32), NEG_INF).T
    L = 2 * T
    tiles = []
    for d in range(3):
        f0 = d * T + (T - 1)
        a = jnp.concatenate([vec[:, f0:f0 + T], vec[:, :1], vec[:, f0 - (T - 1):f0]], axis=1)
        skew = jnp.tile(a, (1, T))[:, :T * (L - 1)].reshape(H_A, T, L - 1)
        tiles.append(skew[:, :, :T])
    return jnp.stack(tiles, axis=1)


def _rms_head(o, g):
    return o * lax.rsqrt(jnp.mean(o * o, axis=-1, keepdims=True) + RMS_EPS) * g


def _prompt_attn_kernel(lam_ref, q_ref, k_ref, v_ref, bias_ref, g_ref, o_ref, kb_ref, vb_ref, s_ref, p_ref, *,
                        T, out_scale):
    qi = pl.program_id(2)
    n_half = T // ATT_HALF

    @pl.when(qi == 0)
    def _():
        kb_ref[...] = k_ref[...].astype(BF16)
        vb_ref[...] = v_ref[...].astype(BF16)

    lane = lax.broadcasted_iota(jnp.int32, (1, 2 * DK_A), 1)
    qm = []
    for hd in range(ATT_HEADS):
        q = q_ref[:, hd * DV_A:(hd + 1) * DV_A] * (DK_A ** -0.5 * LOG2E)
        qm.append((jnp.where(lane < DK_A, q, 0.0).astype(BF16), jnp.where(lane >= DK_A, q, 0.0).astype(BF16)))
    pairs = [(hd, mi) for hd in range(ATT_HEADS) for mi in range(2)]
    blocks = [(pi, hi) for pi in range(len(pairs)) for hi in range(n_half)]

    def kv_rows(kj):
        return pl.ds(pl.multiple_of(kj * T, T), T)

    def head_cols(hd):
        return slice(hd * DV_A, (hd + 1) * DV_A)

    def half_cols(hi):
        return slice(hi * ATT_HALF, (hi + 1) * ATT_HALF)

    def scores(kj):
        tile = jnp.clip(qi - kj, 0, 2)
        ks = [kb_ref[kv_rows(kj), head_cols(hd)] for hd in range(ATT_HEADS)]
        return [_dot_nt(ks[hd], qm[hd][mi]) + bias_ref[hd, tile] for hd, mi in pairs]

    def pv(kj):
        vs = [vb_ref[kv_rows(kj), head_cols(hd)] for hd in range(ATT_HEADS)]
        return [_dot_tn(vs[hd], p_ref[pi]) for pi, (hd, _) in enumerate(pairs)]

    for pi, s in enumerate(scores(0)):
        s_ref[pi] = s
    p_ref[...] = jnp.zeros(p_ref.shape, p_ref.dtype)

    def step(kj, carry):
        s_next = scores(jnp.minimum(kj + 1, qi))
        pv_prev = pv(jnp.maximum(kj - 1, 0))
        out = []
        for (pi, hi), (m, l, b) in zip(blocks, carry):
            s = s_ref[pi, :, half_cols(hi)]
            m_new = jnp.maximum(m, jnp.max(s, axis=0, keepdims=True))
            p = jnp.exp2(s - m_new)
            r = jnp.exp2(m - m_new)
            l_new = r * l + jnp.sum(p, axis=0, keepdims=True)
            p_ref[pi, :, half_cols(hi)] = p.astype(BF16)
            out.append((m_new, l_new, r * (b + pv_prev[pi][:, half_cols(hi)])))
        for pi in range(len(pairs)):
            s_ref[pi] = s_next[pi]
        return tuple(out)

    init = tuple((jnp.full((1, ATT_HALF), NEG_INF, F32), jnp.zeros((1, ATT_HALF), F32),
                  jnp.zeros((DV_A, ATT_HALF), F32)) for _ in blocks)
    res = lax.fori_loop(0, qi + 1, step, init)
    pv_last = pv(qi)
    lam = lam_ref[0]

    def normalised(hd, mi, hi):
        pi = pairs.index((hd, mi))
        _, l, b = res[blocks.index((pi, hi))]
        return (b + pv_last[pi][:, half_cols(hi)]) / l

    for hd in range(ATT_HEADS):
        for hi in range(n_half):
            ot = normalised(hd, 0, hi) - lam * normalised(hd, 1, hi)
            ot = ot * lax.rsqrt(jnp.mean(ot * ot, axis=0, keepdims=True) + RMS_EPS)
            o_ref[half_cols(hi), head_cols(hd)] = (
                ot.T * g_ref[:, head_cols(hd)] * out_scale).astype(o_ref.dtype)


def prompt_attention(z, k_all, v_all, layer, bias_tiles, lam, subln_g_l, lam_init, B, S):
    T = ATT_T
    nq = S // T
    W = ATT_HEADS * DV_A
    assert COLZ["q_a"] == 0 and H_A % ATT_HEADS == 0
    n_pairs = 2 * ATT_HEADS
    kern = functools.partial(_prompt_attn_kernel, T=T, out_scale=1.0 - lam_init)
    return pl.pallas_call(
        kern,
        grid=(B, H_A // ATT_HEADS, nq),
        in_specs=[
            pl.BlockSpec(memory_space=pltpu.SMEM),
            pl.BlockSpec((T, W), lambda b, h, i: (b * nq + i, h)),
            pl.BlockSpec((None, S, W), lambda b, h, i: (layer, b, h)),
            pl.BlockSpec((None, S, W), lambda b, h, i: (layer, b, h)),
            pl.BlockSpec((ATT_HEADS, 3, T, T), lambda b, h, i: (h, 0, 0, 0)),
            pl.BlockSpec((1, W), lambda b, h, i: (0, h)),
        ],
        out_specs=pl.BlockSpec((T, W), lambda b, h, i: (b * nq + i, h)),
        out_shape=jax.ShapeDtypeStruct((B * S, W_A), BF16),
        scratch_shapes=[pltpu.VMEM((S, W), BF16), pltpu.VMEM((S, W), BF16),
                        pltpu.VMEM((n_pairs, T, T), F32),
                        pltpu.VMEM((n_pairs, T, T), BF16)],
        compiler_params=_cparams(3),
        name="prompt_attention",
    )(lam.reshape(1), z, k_all, v_all, bias_tiles, subln_g_l.reshape(1, W_A))


_LEVELS = [CHUNK >> (i + 1) for i in range(int(math.log2(CHUNK)))]


def _cumsum_matrix():
    t = np.arange(CHUNK)
    return (t[None, :] <= t[:, None]).astype(np.float32)


def _anchor_rows(b, w):
    if 2 * w >= SUBLANES:
        g = b.reshape(CHUNK // (2 * w), 2 * w, b.shape[-1])
        return jnp.broadcast_to(g[:, w - 1:w, :], g.shape).reshape(b.shape)
    t = lax.broadcasted_iota(jnp.int32, b.shape, 0)
    off = (t & (2 * w - 1)) - (w - 1)
    out = b
    for d in range(-(w - 1), w + 1):
        if d != 0:
            out = jnp.where(off == d, pltpu.roll(b, d % CHUNK, 0), out)
    return out


def _split3(x):
    x1 = x.astype(BF16)
    r1 = x - x1.astype(F32)
    x2 = r1.astype(BF16)
    x3 = (r1 - x2.astype(F32)).astype(BF16)
    return x1, x2, x3


def _hgrn_gates(fx, qx, lb):
    f = lb + (1.0 - lb) * _sigmoid(fx)
    g = jnp.log(jnp.maximum(f, F_MIN))
    k = (1.0 - lb) * _sigmoid(-fx)
    q = qx * _sigmoid(qx)
    return g, k, q


def _hgrn_prompt_kernel(f_ref, q_ref, i_ref, og_ref, lb_ref, gain_ref, pm_ref, y_ref, s_ref,
                        oi_ref, qe_ref, u_ref, dl_ref, *, S):
    n_chunks = S // CHUNK
    lb = lb_ref[...]
    gain = gain_ref[...]
    pm = pm_ref[...]
    t_io = lax.broadcasted_iota(jnp.int32, (CHUNK, CHUNK), 0)
    s_io = lax.broadcasted_iota(jnp.int32, (CHUNK, CHUNK), 1)
    masks = [(((t_io ^ s_io) < 2 * w) & ((t_io & w) != 0)) & ((s_io & w) == 0) for w in _LEVELS]
    diag = t_io == s_io

    def group(gi, carry):
        cs = [gi * HGRN_UNROLL + u for u in range(HGRN_UNROLL)]
        rows = [pl.ds(pl.multiple_of(c * CHUNK, CHUNK), CHUNK) for c in cs]
        gkq = [_hgrn_gates(f_ref[r, :], q_ref[r, :], lb) for r in rows]
        vbs = [i_ref[r, :].astype(BF16) for r in rows]
        bs = []
        for g, _, _ in gkq:
            g1, g2, g3 = _split3(g)
            bs.append(jnp.dot(pm, g1, preferred_element_type=F32)
                      + jnp.dot(pm, g2, preferred_element_type=F32)
                      + jnp.dot(pm, g3, preferred_element_type=F32))
        a_s = [jnp.where(diag, _dot_nt(q.astype(BF16), k.astype(BF16)), 0.0) for _, k, q in gkq]
        for li, w in enumerate(_LEVELS):
            for u, (_, k, q) in enumerate(gkq):
                anc = _anchor_rows(bs[u], w)
                qs = (q * jnp.exp(jnp.minimum(bs[u] - anc, 0.0))).astype(BF16)
                ks = (k * jnp.exp(jnp.minimum(anc - bs[u], 0.0))).astype(BF16)
                a_s[u] = jnp.where(masks[li], _dot_nt(qs, ks), a_s[u])
        for u, (_, k, q) in enumerate(gkq):
            b = bs[u]
            b_last = b[CHUNK - 1:CHUNK, :]
            oi_ref[rows[u], :] = jnp.dot(a_s[u].astype(BF16), vbs[u], preferred_element_type=F32)
            qe_ref[rows[u], :] = (q * jnp.exp(b)).astype(BF16)
            kd = (k * jnp.exp(b_last - b)).astype(BF16)
            u_ref[cs[u]] = _dot_tn(vbs[u], kd)
            dl_ref[cs[u]] = jnp.exp(b_last)
        return carry

    lax.fori_loop(0, n_chunks // HGRN_UNROLL, group, 0)

    def phase_b(c, st):
        rows = pl.ds(pl.multiple_of(c * CHUNK, CHUNK), CHUNK)
        o = oi_ref[rows, :] + _dot_nt(qe_ref[rows, :], st.astype(BF16))
        y = _rms_head(o, gain) * _sigmoid(og_ref[rows, :])
        y_ref[rows, :] = y.astype(y_ref.dtype)
        return st * dl_ref[c] + u_ref[c]

    st = lax.fori_loop(0, n_chunks, phase_b, jnp.zeros((DV_B, DK_B), F32), unroll=HGRN_UNROLL)
    s_ref[...] = st.T


def hgrn_prompt(z, lb_l, hgrn_g_l, B, S):
    assert (S // CHUNK) % HGRN_UNROLL == 0
    fc, qc, ic, oc = (COLZ[n] // DK_B for n in ("f_b", "q_b", "i_b", "og_b"))
    pm = jnp.asarray(_cumsum_matrix(), BF16)
    kern = functools.partial(_hgrn_prompt_kernel, S=S)
    blk = lambda c0: pl.BlockSpec((S, DK_B), lambda b, h: (b, c0 + h))
    n_chunks = S // CHUNK
    return pl.pallas_call(
        kern,
        grid=(B, H_B),
        in_specs=[
            blk(fc), blk(qc), blk(ic), blk(oc),
            pl.BlockSpec((None, 1, DK_B), lambda b, h: (h, 0, 0)),
            pl.BlockSpec((None, 1, DV_B), lambda b, h: (h, 0, 0)),
            pl.BlockSpec(pm.shape, lambda b, h: (0, 0)),
        ],
        out_specs=[
            pl.BlockSpec((S, DV_B), lambda b, h: (b, h)),
            pl.BlockSpec((None, None, DK_B, DV_B), lambda b, h: (b, h, 0, 0)),
        ],
        out_shape=[
            jax.ShapeDtypeStruct((B * S, W_B), BF16),
            jax.ShapeDtypeStruct((B, H_B, DK_B, DV_B), F32),
        ],
        scratch_shapes=[pltpu.VMEM((S, DV_B), F32), pltpu.VMEM((S, DK_B), BF16),
                        pltpu.VMEM((n_chunks, DV_B, DK_B), F32), pltpu.VMEM((n_chunks, 1, DK_B), F32)],
        compiler_params=_cparams(2),
        name="hgrn_prompt",
    )(z, z, z, z, lb_l.reshape(H_B, 1, DK_B), hgrn_g_l.reshape(H_B, 1, DV_B), pm)


PAGE_ROWS = PAGE_SIZE * H_A
assert H_A & (H_A - 1) == 0


def _paged_attn_kernel(pt_ref, lam_ref, q_ref, kn_ref, vn_ref, bfar_ref, blast_ref, bnew_ref, g_ref, *rest,
                       n_groups, out_scale):
    G = PAGES_PER_STEP
    R = 2 * H_A
    k_refs = rest[:G]
    v_refs = rest[G:2 * G]
    o_ref = rest[2 * G]
    m_ref, l_ref, a_ref = rest[2 * G + 1:]
    j = pl.program_id(1)
    lane = lax.broadcasted_iota(jnp.int32, (1, 2 * DK_A), 1)
    q = q_ref[...] * (DK_A ** -0.5)
    qf = jnp.concatenate([jnp.where(lane < DK_A, q, 0.0), jnp.where(lane >= DK_A, q, 0.0)], axis=0)
    qb = qf.astype(BF16)
    col = lax.broadcasted_iota(jnp.int32, (R, PAGE_ROWS), 1)
    row = lax.broadcasted_iota(jnp.int32, (R, PAGE_ROWS), 0)
    valid = (col & (H_A - 1)) == (row & (H_A - 1))

    @pl.when(j == 0)
    def _():
        m_ref[...] = jnp.full(m_ref.shape, NEG_INF, F32)
        l_ref[...] = jnp.zeros(l_ref.shape, F32)
        a_ref[...] = jnp.zeros(a_ref.shape, F32)

    def scores(i, bias):
        s = _dot_nt(qb, k_refs[i][...].astype(BF16)) + bias
        return jnp.where(valid, s, NEG_INF)

    def absorb(s_list, new_token):
        m_old = m_ref[...]
        m_new = m_old
        for s in s_list:
            m_new = jnp.maximum(m_new, jnp.max(s, axis=-1, keepdims=True))
        if new_token:
            kn2 = jnp.concatenate([kn_ref[...], kn_ref[...]], axis=0)
            vn2 = jnp.concatenate([vn_ref[...], vn_ref[...]], axis=0)
            s_new = jnp.sum(qf * kn2, axis=-1, keepdims=True) + bnew_ref[...]
            m_new = jnp.maximum(m_new, s_new)
        r = jnp.exp(m_old - m_new)
        l = r * l_ref[...]
        acc = r * a_ref[...]
        for i, s in enumerate(s_list):
            p = jnp.exp(s - m_new)
            l = l + jnp.sum(p, axis=-1, keepdims=True)
            acc = acc + jnp.dot(p.astype(BF16), v_refs[i][...].astype(BF16), preferred_element_type=F32)
        if new_token:
            p_new = jnp.exp(s_new - m_new)
            l = l + p_new
            acc = acc + p_new * vn2
        m_ref[...] = m_new
        l_ref[...] = l
        a_ref[...] = acc

    far = bfar_ref[...]

    @pl.when(j < n_groups - 1)
    def _():
        absorb([scores(i, far) for i in range(G)], False)

    @pl.when(j == n_groups - 1)
    def _():
        absorb([scores(i, far) for i in range(G - 1)] + [scores(G - 1, blast_ref[...])], True)
        lam = lam_ref[0]
        o = a_ref[0:H_A] / l_ref[0:H_A] - lam * (a_ref[H_A:R] / l_ref[H_A:R])
        o_ref[...] = (_rms_head(o, g_ref[...]) * out_scale).astype(o_ref.dtype)


def sample_attention(q, k_new, v_new, cache_k, cache_v, page_table, rel_bias, lam, subln_g_l,
                     lam_init, layer):
    DB = q.shape[0]
    n_pages = page_table.shape[1]
    G = PAGES_PER_STEP
    R = 2 * H_A
    assert n_pages % G == 0
    n_groups = n_pages // G
    hmap = jnp.tile(jnp.arange(H_A), 2)
    bias_of = lambda dist: rel_bias[_t5_bucket(dist)].astype(F32)
    bfar = bias_of(jnp.array(2 * PAGE_SIZE))[hmap][:, None]
    bnew = bias_of(jnp.array(0))[hmap][:, None]
    blast = jnp.broadcast_to(bias_of(PAGE_SIZE - jnp.arange(PAGE_SIZE)).reshape(1, PAGE_ROWS), (R, PAGE_ROWS))
    kern = functools.partial(_paged_attn_kernel, n_groups=n_groups, out_scale=1.0 - lam_init)

    def page_spec(i):
        return pl.BlockSpec((None, None, PAGE_ROWS, DV_A),
                            lambda b, j, pt: (layer, pt[b * n_pages + j * G + i], 0, 0))

    tok = pl.BlockSpec((None, H_A, DV_A), lambda b, j, pt: (b, 0, 0))
    const = lambda shape: pl.BlockSpec(shape, lambda b, j, pt: (0,) * len(shape))
    grid_spec = pltpu.PrefetchScalarGridSpec(
        num_scalar_prefetch=1,
        grid=(DB, n_groups),
        in_specs=[
            pl.BlockSpec(memory_space=pltpu.SMEM),
            tok, tok, tok,
            const((R, 1)), const((R, PAGE_ROWS)), const((R, 1)), const((H_A, DV_A)),
        ] + [page_spec(i) for i in range(G)] + [page_spec(i) for i in range(G)],
        out_specs=tok,
        scratch_shapes=[pltpu.VMEM((R, 1), F32), pltpu.VMEM((R, 1), F32), pltpu.VMEM((R, DV_A), F32)],
    )
    return pl.pallas_call(
        kern,
        grid_spec=grid_spec,
        out_shape=jax.ShapeDtypeStruct((DB, H_A, DV_A), BF16),
        compiler_params=_cparams(2),
        name="sample_attention",
    )(page_table.reshape(-1), lam.reshape(1), q, k_new, v_new, bfar, blast, bnew,
      subln_g_l.reshape(H_A, DV_A), *([cache_k] * G), *([cache_v] * G))


def _hgrn_step_kernel(fx_ref, qx_ref, v_ref, og_ref, lb_ref, gain_ref, s_ref, y_ref, so_ref, *, DB):
    def columns(a):
        return jnp.concatenate([a, jnp.zeros((LANES - DB, a.shape[1]), F32)], axis=0).T

    g, k, q = _hgrn_gates(fx_ref[...], qx_ref[...], lb_ref[...])
    d_t, k_t, q_t = columns(jnp.exp(g)), columns(k), columns(q)
    v = v_ref[...]
    outs = []
    for b in range(DB):
        s_new = d_t[:, b:b + 1] * s_ref[b].astype(F32) + k_t[:, b:b + 1] * v[b:b + 1, :]
        so_ref[b] = s_new.astype(so_ref.dtype)
        outs.append(jnp.sum(q_t[:, b:b + 1] * s_new, axis=0, keepdims=True))
    o = jnp.concatenate(outs, axis=0)
    y = _rms_head(o, gain_ref[...]) * _sigmoid(og_ref[...])
    y_ref[...] = y.astype(y_ref.dtype)


def hgrn_step(z, MP, lb_l, hgrn_g_l, state, layer):
    DB = z.shape[0] - MP
    assert MP % DB == 0 and DB <= LANES and DK_B == LANES
    r0 = MP // DB
    blk = lambda name: pl.BlockSpec((DB, DK_B), lambda h: (r0, COLZ[name] // DK_B + h))
    return pl.pallas_call(
        functools.partial(_hgrn_step_kernel, DB=DB),
        grid=(H_B,),
        in_specs=[blk("f_b"), blk("q_b"), blk("i_b"), blk("og_b"),
                  pl.BlockSpec((None, 1, DK_B), lambda h: (h, 0, 0)),
                  pl.BlockSpec((None, 1, DV_B), lambda h: (h, 0, 0)),
                  pl.BlockSpec((None, DB, None, DK_B, DV_B), lambda h: (layer, 0, h, 0, 0))],
        out_specs=[pl.BlockSpec((DB, DV_B), lambda h: (0, h)),
                   pl.BlockSpec((DB, None, DK_B, DV_B), lambda h: (0, h, 0, 0))],
        out_shape=[jax.ShapeDtypeStruct((DB, W_B), BF16),
                   jax.ShapeDtypeStruct(state.shape[1:], state.dtype)],
        compiler_params=_cparams(1),
        name="hgrn_step",
    )(z, z, z, z, lb_l.reshape(H_B, 1, DK_B), hgrn_g_l.reshape(H_B, 1, DV_B), state)


def _router_kernel(x_ref, w_ref, idx_ref, gate_ref):
    x1, x2, _ = _split3(x_ref[...])
    w1, w2, _ = _split3(w_ref[...])
    lt = _dot_nt(w1, x1) + (_dot_nt(w1, x2) + _dot_nt(w2, x1))
    e_io = lax.broadcasted_iota(jnp.int32, lt.shape, 0)
    m1 = jnp.max(lt, axis=0, keepdims=True)
    i1 = jnp.min(jnp.where(lt == m1, e_io, N_EXPERTS), axis=0, keepdims=True)
    lt2 = jnp.where(e_io == i1, -jnp.inf, lt)
    m2 = jnp.max(lt2, axis=0, keepdims=True)
    i2 = jnp.min(jnp.where(lt2 == m2, e_io, N_EXPERTS), axis=0, keepdims=True)
    e2 = jnp.exp(m2 - m1)
    den = 1.0 + e2
    idx_ref[0:1, :] = i1
    idx_ref[1:2, :] = i2
    gate_ref[0:1, :] = 1.0 / den
    gate_ref[1:2, :] = e2 / den


def router(x, router_w_l, *, tm=512):
    M, D = x.shape
    return pl.pallas_call(
        _router_kernel,
        grid=(pl.cdiv(M, tm),),
        in_specs=[pl.BlockSpec((tm, D), lambda m: (m, 0)),
                  pl.BlockSpec((N_EXPERTS, D), lambda m: (0, 0))],
        out_specs=[pl.BlockSpec((TOP_K, tm), lambda m: (0, m)), pl.BlockSpec((TOP_K, tm), lambda m: (0, m))],
        out_shape=[jax.ShapeDtypeStruct((TOP_K, M), jnp.int32), jax.ShapeDtypeStruct((TOP_K, M), F32)],
        compiler_params=_cparams(1),
        name="router",
    )(x, router_w_l.T)


def _scatter_rows_kernel(pos_ref, x_ref, xs_in, xs_hbm, sem, *, rows, M):
    del xs_in
    base = pl.program_id(0) * rows
    n = jnp.minimum(rows, M - base)

    def row_copy(kk, r, dst_row):
        return pltpu.make_async_copy(x_ref.at[r], xs_hbm.at[dst_row], sem)

    def issue(r, c):
        for kk in range(TOP_K):
            row_copy(kk, r, pos_ref[kk, base + r]).start()
        return c

    lax.fori_loop(0, n, issue, 0)

    def drain(r, c):
        for kk in range(TOP_K):
            row_copy(kk, r, 0).wait()
        return c

    lax.fori_loop(0, n, drain, 0)


def scatter_rows(x, pos, P, *, rows=256):
    M, D = x.shape
    Mp = pl.cdiv(M, rows) * rows
    x3 = x.reshape(M, D // LANES, LANES)
    pos = jnp.pad(pos, ((0, 0), (0, Mp - M)))
    xs0 = jnp.zeros((P, D // LANES, LANES), x.dtype)
    out = pl.pallas_call(
        functools.partial(_scatter_rows_kernel, rows=rows, M=M),
        grid_spec=pltpu.PrefetchScalarGridSpec(
            num_scalar_prefetch=1,
            grid=(Mp // rows,),
            in_specs=[pl.BlockSpec((rows, D // LANES, LANES), lambda i, p: (i, 0, 0)),
                      pl.BlockSpec(memory_space=pl.ANY)],
            out_specs=pl.BlockSpec(memory_space=pl.ANY),
            scratch_shapes=[pltpu.SemaphoreType.DMA(())],
        ),
        out_shape=jax.ShapeDtypeStruct(xs0.shape, x.dtype),
        input_output_aliases={2: 0},
        compiler_params=_cparams(1),
        name="scatter_rows",
    )(pos, x3, xs0)
    return out.reshape(P, D)


def _for_used_rows(n_rows, tm, o_ref, fn):
    n_steps = (n_rows + MOE_ROW_STEP - 1) // MOE_ROW_STEP
    for k in range(tm // MOE_ROW_STEP + 1):
        @pl.when(n_steps == k)
        def _(k=k):
            r = k * MOE_ROW_STEP
            if r:
                o_ref[0:r, :] = fn(slice(0, r)).astype(o_ref.dtype)
            if r < tm:
                o_ref[r:tm, :] = jnp.zeros((tm - r, o_ref.shape[1]), o_ref.dtype)


def _moe_up_kernel(te_ref, tf_ref, tr_ref, x_ref, w1_ref, w3_ref, o_ref, w1b_ref, w3b_ref):
    t = pl.program_id(1)

    @pl.when(tf_ref[t] == 1)
    def _():
        w1b_ref[...] = w1_ref[...].astype(BF16)
        w3b_ref[...] = w3_ref[...].astype(BF16)

    def swiglu(rs):
        x = x_ref[rs, :]
        a = jnp.dot(x, w1b_ref[...], preferred_element_type=F32)
        b = jnp.dot(x, w3b_ref[...], preferred_element_type=F32)
        return a * _sigmoid(a) * b

    _for_used_rows(tr_ref[t], MOE_TM, o_ref, swiglu)


def moe_up(xs, w1, w3, layer, tile_expert, tile_first, tile_rows, *, tn):
    P, K = xs.shape
    N = w1.shape[3]
    tm = MOE_TM
    nt = P // tm
    wspec = pl.BlockSpec((None, None, K, tn), lambda n, t, te, tf, tr: (layer, te[t], 0, n))
    return pl.pallas_call(
        _moe_up_kernel,
        grid_spec=pltpu.PrefetchScalarGridSpec(
            num_scalar_prefetch=3,
            grid=(N // tn, nt),
            in_specs=[pl.BlockSpec((tm, K), lambda n, t, te, tf, tr: (t, 0)), wspec, wspec],
            out_specs=pl.BlockSpec((tm, tn), lambda n, t, te, tf, tr: (t, n)),
            scratch_shapes=[pltpu.VMEM((K, tn), BF16), pltpu.VMEM((K, tn), BF16)],
        ),
        out_shape=jax.ShapeDtypeStruct((P, N), BF16),
        compiler_params=_cparams(2),
        name="moe_up",
    )(tile_expert, tile_first, tile_rows, xs, w1, w3)


def _moe_down_kernel(te_ref, tf_ref, tr_ref, h_ref, w_ref, o_ref, wb_ref, *, sub, tm):
    t = pl.program_id(1)
    tt = t // sub
    part = t % sub

    @pl.when((tf_ref[tt] == 1) & (part == 0))
    def _():
        wb_ref[...] = w_ref[...].astype(BF16)

    n_rows = jnp.clip(tr_ref[tt] - part * tm, 0, tm)
    _for_used_rows(n_rows, tm, o_ref,
                   lambda rs: jnp.dot(h_ref[rs, :], wb_ref[...], preferred_element_type=F32))


def moe_down(h, w2, layer, tile_expert, tile_first, tile_rows, *, tm, tn):
    P, K = h.shape
    N = w2.shape[3]
    assert MOE_TM % tm == 0 and tm % MOE_ROW_STEP == 0
    sub = MOE_TM // tm
    return pl.pallas_call(
        functools.partial(_moe_down_kernel, sub=sub, tm=tm),
        grid_spec=pltpu.PrefetchScalarGridSpec(
            num_scalar_prefetch=3,
            grid=(N // tn, P // tm),
            in_specs=[pl.BlockSpec((tm, K), lambda n, t, te, tf, tr: (t, 0)),
                      pl.BlockSpec((None, None, K, tn), lambda n, t, te, tf, tr: (layer, te[t // sub], 0, n))],
            out_specs=pl.BlockSpec((tm, tn), lambda n, t, te, tf, tr: (t, n)),
            scratch_shapes=[pltpu.VMEM((K, tn), BF16)],
        ),
        out_shape=jax.ShapeDtypeStruct((P, N), F32),
        compiler_params=_cparams(2),
        name="moe_down",
    )(tile_expert, tile_first, tile_rows, h, w2)


def _moe_combine_ln_kernel(pos_ref, gate_ref, x_ref, g_ref, b_ref, e_hbm, op_ref, os_ref, buf_ref, sem, *,
                           rows, n_full, tail):
    i = pl.program_id(0)
    slot = i % 2

    def n_rows(step):
        return jnp.where(step < n_full, rows, tail)

    def row_copy(s, kk, r, src_row):
        return pltpu.make_async_copy(e_hbm.at[src_row], buf_ref.at[s, kk, r], sem.at[s])

    def issue(step, s):
        def body(r, c):
            for kk in range(TOP_K):
                row_copy(s, kk, r, pos_ref[kk, step * rows + r]).start()
            return c

        lax.fori_loop(0, n_rows(step), body, 0)

    @pl.when(i == 0)
    def _():
        issue(0, 0)

    @pl.when(i < n_full)
    def _():
        issue(i + 1, 1 - slot)

    def drain(r, c):
        for kk in range(TOP_K):
            row_copy(slot, kk, r, 0).wait()
        return c

    lax.fori_loop(0, n_rows(i), drain, 0)

    def finish(nr, out_ref):
        g0 = gate_ref[0:nr, 0:1]
        g1 = gate_ref[0:nr, 1:2]
        y = jnp.concatenate([g0 * buf_ref[slot, 0, 0:nr, j, :] + g1 * buf_ref[slot, 1, 0:nr, j, :]
                             for j in range(buf_ref.shape[3])], axis=-1)
        out_ref[...] = _add_ln(x_ref[0:nr, :], y, g_ref[...], b_ref[...])

    @pl.when(i < n_full)
    def _():
        finish(rows, op_ref)

    @pl.when(i == n_full)
    def _():
        finish(tail, os_ref)


def moe_combine_ln(eout, pos, gate, x, g, b, MP, *, rows=128):
    M, D = x.shape
    P = eout.shape[0]
    DB = M - MP
    assert MP % rows == 0 and 0 < DB < rows
    n_full = MP // rows
    e3 = eout.reshape(P, D // LANES, LANES)
    pos = jnp.pad(pos, ((0, 0), (0, (n_full + 1) * rows - M)))
    vec = pl.BlockSpec((1, D), lambda i, p: (0, 0))
    return pl.pallas_call(
        functools.partial(_moe_combine_ln_kernel, rows=rows, n_full=n_full, tail=DB),
        grid_spec=pltpu.PrefetchScalarGridSpec(
            num_scalar_prefetch=1,
            grid=(n_full + 1,),
            in_specs=[pl.BlockSpec((rows, TOP_K), lambda i, p: (i, 0)),
                      pl.BlockSpec((rows, D), lambda i, p: (i, 0)),
                      vec, vec,
                      pl.BlockSpec(memory_space=pl.ANY)],
            out_specs=[pl.BlockSpec((rows, D), lambda i, p: (jnp.minimum(i, n_full - 1), 0)),
                       pl.BlockSpec((DB, D), lambda i, p: (0, 0))],
            scratch_shapes=[pltpu.VMEM((2, TOP_K, rows, D // LANES, LANES), F32),
                            pltpu.SemaphoreType.DMA((2,))],
        ),
        out_shape=[jax.ShapeDtypeStruct((MP, D), F32), jax.ShapeDtypeStruct((DB, D), F32)],
        compiler_params=_cparams(1),
        name="moe_combine_ln",
    )(pos, gate, x, g.reshape(1, D), b.reshape(1, D), e3)


def _routing_tables(idx, tm):
    M = idx.shape[1]
    NP = TOP_K * M
    nb = pl.cdiv(NP, LANES)
    e_flat = jnp.pad(idx.reshape(-1), (0, nb * LANES - NP), constant_values=N_EXPERTS)
    oh = (e_flat[None, :] == jnp.arange(N_EXPERTS)[:, None]).astype(F32).reshape(N_EXPERTS, nb, LANES)
    tri = jnp.asarray(np.triu(np.ones((LANES, LANES), np.float32)))
    within = jnp.einsum("ebj,ji->ebi", oh, tri, precision=lax.Precision.HIGHEST)
    blk_tot = within[..., -1]
    blk_pre = jnp.cumsum(blk_tot, axis=1) - blk_tot
    counts = (blk_pre[:, -1] + blk_tot[:, -1]).astype(jnp.int32)
    tiles_per = (counts + tm - 1) // tm
    tile_start = jnp.cumsum(tiles_per) - tiles_per
    rem = counts - (tiles_per - 1) * tm
    rank = within + blk_pre[..., None] - 1.0
    remf = rem.astype(F32)[:, None, None]
    row_in_expert = jnp.where(rank < remf, rank, rank - remf + tm)
    pos = jnp.sum(oh * (row_in_expert + (tile_start * tm).astype(F32)[:, None, None]), axis=0)
    pos = pos.reshape(-1)[:NP].astype(jnp.int32).reshape(TOP_K, M)
    n_tiles = NP // tm + N_EXPERTS
    t_io = jnp.arange(n_tiles)
    used = jnp.sum(tiles_per)
    tile_expert = jnp.clip(jnp.sum(t_io[:, None] >= tile_start[None, :], axis=1) - 1, 0, N_EXPERTS - 1)
    tile_valid = (t_io < used).astype(jnp.int32)
    last_used_expert = jnp.max(jnp.where(tile_valid == 1, tile_expert, 0))
    tile_expert = jnp.where(tile_valid == 1, tile_expert, last_used_expert).astype(jnp.int32)
    prev = jnp.concatenate([jnp.array([-1], jnp.int32), tile_expert[:-1]])
    tile_first = (tile_expert != prev).astype(jnp.int32)
    own = jnp.arange(N_EXPERTS)[None, :] == tile_expert[:, None]
    of_tile = lambda per_expert: jnp.sum(jnp.where(own, per_expert[None, :], 0), axis=1)
    is_first_of_expert = t_io == of_tile(tile_start)
    tile_rows = jnp.where(tile_valid == 1, jnp.where(is_first_of_expert, of_tile(rem), tm), 0).astype(jnp.int32)
    return pos, tile_expert, tile_first, tile_rows, n_tiles * tm


def moe_ffn_ln(x, xb, router_w_l, w1, w3, w2, layer, g, b, MP):
    idx, gate = router(x, router_w_l)
    pos, tile_expert, tile_first, tile_rows, P = _routing_tables(idx, MOE_TM)
    xs = scatter_rows(xb, pos, P)
    h = moe_up(xs, w1, w3, layer, tile_expert, tile_first, tile_rows, tn=512)
    eout = moe_down(h, w2, layer, tile_expert, tile_first, tile_rows, tm=512, tn=512)
    return moe_combine_ln(eout, pos, gate.T, x, g, b, MP)


def kernel(x_prompt, x_sample, cache_k, cache_v, page_table, state_hgrn, rel_bias, w_in, w_pa, w_pb, w_out,
           lam_qk, subln_g, hgrn_g, lb_raw, ln_g, ln_b, ffn_w1, ffn_w3, ffn_w2, router_w, moe_w1, moe_w3, moe_w2):
    B, S, D = x_prompt.shape
    DB = x_sample.shape[0]
    MP = B * S
    lbs = jax.nn.softmax(lb_raw.astype(F32), axis=0)
    lbs = jnp.cumsum(lbs, axis=0) - lbs[0]
    bias_tiles = _bias_tiles_t(rel_bias, ATT_T) * LOG2E
    ck = cache_k.reshape(cache_k.shape[0], cache_k.shape[1], PAGE_ROWS, 2 * DK_A)
    cv = cache_v.reshape(cache_v.shape[0], cache_v.shape[1], PAGE_ROWS, DV_A)

    xp = x_prompt.reshape(MP, D)
    xs = x_sample.reshape(DB, D)
    x = (xp, xs)
    xb = stack_rows_bf16(xp, xs)
    kp = jnp.zeros((DEPTH, MP, W_A), F32)
    vp = jnp.zeros((DEPTH, MP, W_A), F32)
    ks = jnp.zeros((DEPTH, DB, W_A), F32)
    vs = jnp.zeros((DEPTH, DB, W_A), F32)
    sp, ss = [], []
    for l in range(DEPTH):
        lam_init = 0.8 - 0.6 * math.exp(-0.3 * l)
        lq = lam_qk[l].astype(F32)
        lam = jnp.exp(jnp.sum(lq[0] * lq[1])) - jnp.exp(jnp.sum(lq[2] * lq[3])) + lam_init

        z, kp, vp, ks, vs = in_proj(xb, w_in, l, kp, vp, ks, vs, tm=1024, tn=W_A)
        zs = z[MP:]
        ya = prompt_attention(z, kp, vp, l, bias_tiles, lam, subln_g[l], lam_init, B, S)
        yb, s_p = hgrn_prompt(z, lbs[l], hgrn_g[l], B, S)
        heads = lambda a: a.reshape(DB, H_A, DV_A)
        ya_s = sample_attention(heads(zs[:, COLZ["q_a"]:COLZ["q_a"] + W_A]), heads(ks[l]), heads(vs[l]),
                                ck, cv, page_table, rel_bias, lam, subln_g[l], lam_init, l)
        yb_s, s_s = hgrn_step(z, MP, lbs[l], hgrn_g[l], state_hgrn, l)
        mg = branch_merge(ya, yb, ya_s.reshape(DB, W_A), yb_s, z, w_pa, w_pb, l, tm=1024, tn=512)
        x, xb = proj_residual_ln(mg, w_out, l, x, ln_g[l, 0], ln_b[l, 0], MP)

        j = l // 2
        if l % 2 == 0:
            hh = swiglu_up(xb, ffn_w1, ffn_w3, j, tm=1024, tn=512)
            c = matmul(hh, ffn_w2, j, tm=512, tn=512)
            x, xb = residual_ln(x, c, ln_g[l, 1], ln_b[l, 1])
        else:
            assert l == DEPTH - 1
            y_prompt, y_sample = moe_ffn_ln(x, xb, router_w[j], moe_w1, moe_w3, moe_w2, j,
                                            ln_g[l, 1], ln_b[l, 1], MP)
        sp.append(s_p)
        ss.append(s_s)
    return (y_prompt.reshape(B, S, D), y_sample.reshape(DB, 1, D),
            kp.reshape(DEPTH, B, S, H_A, 2 * DK_A), vp.reshape(DEPTH, B, S, H_A, DV_A), jnp.stack(sp),
            ks.reshape(DEPTH, DB, 1, H_A, 2 * DK_A), vs.reshape(DEPTH, DB, 1, H_A, DV_A), jnp.stack(ss))
```

```python
import functools
import math

import numpy as np
import jax
import jax.numpy as jnp
from jax import lax
from jax.experimental import pallas as pl
from jax.experimental.pallas import tpu as pltpu

F32 = jnp.float32
BF16 = jnp.bfloat16

D_MODEL = 2048
DEPTH = 2
PAGE_SIZE = 128
H_A = 8
DK_A = 64
DV_A = 2 * DK_A
W_A = H_A * DV_A
H_B = 8
DK_B = 128
DV_B = 128
W_BK = H_B * DK_B
W_B = H_B * DV_B
CHUNK = 64
N_BUCKETS = 32
MAX_EXACT = N_BUCKETS // 2
MAX_DISTANCE = 128
N_EXPERTS = 8
TOP_K = 2
ALPHA = (2 * DEPTH) ** 0.25
LN_EPS = 1e-5
RMS_EPS = 1e-6
F_MIN = 1e-20
NEG_INF = -1e30
LOG2E = 1.0 / math.log(2.0)
SPLITS = [H_A * 2 * DK_A, H_A * 2 * DK_A, W_A, W_BK, W_BK, W_B, W_B, D_MODEL, D_MODEL]
N_IN = sum(SPLITS)
_NAMES = ["q_a", "k_a", "v_a", "f_b", "q_b", "i_b", "og_b", "gt_a", "gt_b"]
COL = {}
COLZ = {}
_acc = _accz = 0
for _name, _w in zip(_NAMES, SPLITS):
    COL[_name] = _acc
    _acc += _w
    if _name not in ("k_a", "v_a"):
        COLZ[_name] = _accz
        _accz += _w
N_Z = _accz

LANES = 128
SUBLANES = 8
V7X_VMEM_BYTES = 64 * 1024 * 1024
VMEM_LIMIT = V7X_VMEM_BYTES - 2 * 1024 * 1024

ATT_T = 256
ATT_HALF = LANES
ATT_HEADS = 2
MOE_TM = 1024
MOE_ROW_STEP = 128
PAGES_PER_STEP = 8
DMA_UNROLL = 8
HGRN_UNROLL = 4


def _cparams(n_axes):
    return pltpu.CompilerParams(
        dimension_semantics=("arbitrary",) * n_axes, vmem_limit_bytes=VMEM_LIMIT)


def _sigmoid(x):
    return jax.nn.sigmoid(x)


def _dot_nt(a, b):
    return lax.dot_general(a, b, (((1,), (1,)), ((), ())), preferred_element_type=F32)


def _dot_tn(a, b):
    return lax.dot_general(a, b, (((0,), (0,)), ((), ())), preferred_element_type=F32)


def _for_rows(m, M, tm, fn):
    n_full, tail = M // tm, M % tm
    if n_full:
        @pl.when(m < n_full)
        def _():
            fn(slice(None))
    if tail:
        @pl.when(m == n_full)
        def _():
            fn(slice(0, tail))


def _mm_kernel(x_ref, w_ref, o_ref, wb_ref, *, M, tm):
    m = pl.program_id(1)

    @pl.when(m == 0)
    def _():
        wb_ref[...] = w_ref[...].astype(BF16)

    def body(rs):
        o_ref[rs, :] = jnp.dot(x_ref[rs, :], wb_ref[...], preferred_element_type=F32).astype(o_ref.dtype)

    _for_rows(m, M, tm, body)


def matmul(x, w, layer, *, tm, tn, out_dtype=F32):
    M, K = x.shape
    N = w.shape[2]
    assert N % tn == 0
    return pl.pallas_call(
        functools.partial(_mm_kernel, M=M, tm=tm),
        grid=(N // tn, pl.cdiv(M, tm)),
        in_specs=[
            pl.BlockSpec((tm, K), lambda n, m: (m, 0)),
            pl.BlockSpec((None, K, tn), lambda n, m: (layer, 0, n)),
        ],
        out_specs=pl.BlockSpec((tm, tn), lambda n, m: (m, n)),
        out_shape=jax.ShapeDtypeStruct((M, N), out_dtype),
        scratch_shapes=[pltpu.VMEM((K, tn), BF16)],
        compiler_params=_cparams(2),
        name="matmul",
    )(x, w)


def _in_proj_kernel(x_ref, w_ref, kp_in, vp_in, ks_in, vs_in, z_ref, kp_ref, vp_ref, ks_ref, vs_ref, wb_ref, *,
                    n_full, tail, nk, nv):
    del kp_in, vp_in, ks_in, vs_in
    n = pl.program_id(0)
    m = pl.program_id(1)

    @pl.when(m == 0)
    def _():
        wb_ref[...] = w_ref[...].astype(BF16)

    def emit(cond, full_ref, tail_ref, tail_rows):
        @pl.when(cond & (m < n_full))
        def _():
            full_ref[...] = jnp.dot(x_ref[...], wb_ref[...], preferred_element_type=F32)

        @pl.when(cond & (m == n_full))
        def _():
            tail_ref[tail_rows, :] = jnp.dot(x_ref[0:tail, :], wb_ref[...], preferred_element_type=F32)

    emit((n != nk) & (n != nv), z_ref, z_ref, slice(0, tail))
    emit(n == nk, kp_ref, ks_ref, slice(None))
    emit(n == nv, vp_ref, vs_ref, slice(None))


def in_proj(x, w_in, layer, kp, vp, ks, vs, *, tm, tn):
    M, K = x.shape
    MP, DB = kp.shape[1], ks.shape[1]
    assert tn == W_A and M == MP + DB and MP % tm == 0 and DB < tm and N_IN % tn == 0
    n_full = MP // tm
    nk, nv = COL["k_a"] // tn, COL["v_a"] // tn
    assert nv == nk + 1 and COL["q_a"] == 0

    def z_idx(n, m):
        hold = (n == nk) | (n == nv)
        return (jnp.where(hold, n_full, m), jnp.where(n < nk, n, jnp.where(hold, nk - 1, n - 2)))

    def p_idx(n_own):
        def idx(n, m):
            return (layer, jnp.where(n < n_own, 0, jnp.where(n == n_own, jnp.minimum(m, n_full - 1), n_full - 1)), 0)
        return idx

    s_idx = lambda n, m: (layer, 0, 0)
    anyspec = pl.BlockSpec(memory_space=pl.ANY)
    return pl.pallas_call(
        functools.partial(_in_proj_kernel, n_full=n_full, tail=DB, nk=nk, nv=nv),
        grid=(N_IN // tn, n_full + 1),
        in_specs=[
            pl.BlockSpec((tm, K), lambda n, m: (m, 0)),
            pl.BlockSpec((None, K, tn), lambda n, m: (layer, 0, n)),
            anyspec, anyspec, anyspec, anyspec,
        ],
        out_specs=[
            pl.BlockSpec((tm, tn), z_idx),
            pl.BlockSpec((None, tm, tn), p_idx(nk)),
            pl.BlockSpec((None, tm, tn), p_idx(nv)),
            pl.BlockSpec((None, DB, tn), s_idx),
            pl.BlockSpec((None, DB, tn), s_idx),
        ],
        out_shape=[jax.ShapeDtypeStruct((M, N_Z), F32)] + [jax.ShapeDtypeStruct(a.shape, a.dtype)
                                                            for a in (kp, vp, ks, vs)],
        input_output_aliases={2: 1, 3: 2, 4: 3, 5: 4},
        scratch_shapes=[pltpu.VMEM((K, tn), BF16)],
        compiler_params=_cparams(2),
        name="in_proj",
    )(x, w_in, kp, vp, ks, vs)


def _swiglu_halves(x, w1b_ref, w3b_ref):
    half = w1b_ref.shape[1] // 2
    ab = [(jnp.dot(x, w1b_ref[:, c * half:(c + 1) * half], preferred_element_type=F32),
           jnp.dot(x, w3b_ref[:, c * half:(c + 1) * half], preferred_element_type=F32)) for c in range(2)]
    return jnp.concatenate([a * _sigmoid(a) * b for a, b in ab], axis=-1)


def _swiglu_kernel(x_ref, w1_ref, w3_ref, o_ref, w1b_ref, w3b_ref, *, M, tm):
    m = pl.program_id(1)

    @pl.when(m == 0)
    def _():
        w1b_ref[...] = w1_ref[...].astype(BF16)
        w3b_ref[...] = w3_ref[...].astype(BF16)

    def body(rs):
        o_ref[rs, :] = _swiglu_halves(x_ref[rs, :], w1b_ref, w3b_ref).astype(o_ref.dtype)

    _for_rows(m, M, tm, body)


def swiglu_up(x, w1, w3, layer, *, tm, tn):
    M, K = x.shape
    N = w1.shape[2]
    assert N % tn == 0
    return pl.pallas_call(
        functools.partial(_swiglu_kernel, M=M, tm=tm),
        grid=(N // tn, pl.cdiv(M, tm)),
        in_specs=[
            pl.BlockSpec((tm, K), lambda n, m: (m, 0)),
            pl.BlockSpec((None, K, tn), lambda n, m: (layer, 0, n)),
            pl.BlockSpec((None, K, tn), lambda n, m: (layer, 0, n)),
        ],
        out_specs=pl.BlockSpec((tm, tn), lambda n, m: (m, n)),
        out_shape=jax.ShapeDtypeStruct((M, N), BF16),
        scratch_shapes=[pltpu.VMEM((K, tn), BF16), pltpu.VMEM((K, tn), BF16)],
        compiler_params=_cparams(2),
        name="swiglu_up",
    )(x, w1, w3)


def _merge_kernel(ya_ref, yb_ref, yas_ref, ybs_ref, wa_ref, wb_ref, ga_ref, gb_ref, o_ref, wab_ref, wbb_ref, *,
                  n_full, tail):
    m = pl.program_id(1)

    @pl.when(m == 0)
    def _():
        wab_ref[...] = wa_ref[...].astype(BF16)
        wbb_ref[...] = wb_ref[...].astype(BF16)

    def merge(ya, yb, rs):
        pa = jnp.dot(ya, wab_ref[...], preferred_element_type=F32)
        pb = jnp.dot(yb, wbb_ref[...], preferred_element_type=F32)
        o_ref[rs, :] = (_sigmoid(ga_ref[rs, :]) * pa + _sigmoid(gb_ref[rs, :]) * pb).astype(o_ref.dtype)

    @pl.when(m < n_full)
    def _():
        merge(ya_ref[...], yb_ref[...], slice(None))

    @pl.when(m == n_full)
    def _():
        merge(yas_ref[...], ybs_ref[...], slice(0, tail))


def branch_merge(ya, yb, ya_s, yb_s, z, w_pa, w_pb, layer, *, tm, tn):
    MP, K = ya.shape
    DB = ya_s.shape[0]
    M = z.shape[0]
    N = w_pa.shape[2]
    assert N % tn == 0 and MP % tm == 0 and M == MP + DB and DB < tm
    n_full = MP // tm
    ga0, gb0 = COLZ["gt_a"] // tn, COLZ["gt_b"] // tn
    prow = pl.BlockSpec((tm, K), lambda n, m: (jnp.minimum(m, n_full - 1), 0))
    srow = pl.BlockSpec((DB, K), lambda n, m: (0, 0))
    return pl.pallas_call(
        functools.partial(_merge_kernel, n_full=n_full, tail=DB),
        grid=(N // tn, n_full + 1),
        in_specs=[
            prow, prow, srow, srow,
            pl.BlockSpec((None, K, tn), lambda n, m: (layer, 0, n)),
            pl.BlockSpec((None, K, tn), lambda n, m: (layer, 0, n)),
            pl.BlockSpec((tm, tn), lambda n, m: (m, ga0 + n)),
            pl.BlockSpec((tm, tn), lambda n, m: (m, gb0 + n)),
        ],
        out_specs=pl.BlockSpec((tm, tn), lambda n, m: (m, n)),
        out_shape=jax.ShapeDtypeStruct((M, N), BF16),
        scratch_shapes=[pltpu.VMEM((K, tn), BF16), pltpu.VMEM((K, tn), BF16)],
        compiler_params=_cparams(2),
        name="branch_merge",
    )(ya, yb, ya_s, yb_s, w_pa, w_pb, z, z)


def _add_ln(x, y, g, b):
    h = ALPHA * x + y
    mu = jnp.mean(h, axis=-1, keepdims=True)
    c = h - mu
    var = jnp.mean(c * c, axis=-1, keepdims=True)
    return c * lax.rsqrt(var + LN_EPS) * g + b


def _stack_bf16_kernel(xp_ref, xs_ref, o_ref, *, n_full, tail):
    m = pl.program_id(0)

    @pl.when(m < n_full)
    def _():
        o_ref[...] = xp_ref[...].astype(BF16)

    @pl.when(m == n_full)
    def _():
        o_ref[0:tail, :] = xs_ref[...].astype(BF16)


def stack_rows_bf16(xp, xs, *, tm=512):
    MP, D = xp.shape
    DB = xs.shape[0]
    assert MP % tm == 0 and 0 < DB < tm
    n_full = MP // tm
    return pl.pallas_call(
        functools.partial(_stack_bf16_kernel, n_full=n_full, tail=DB),
        grid=(n_full + 1,),
        in_specs=[pl.BlockSpec((tm, D), lambda m: (jnp.minimum(m, n_full - 1), 0)),
                  pl.BlockSpec((DB, D), lambda m: (0, 0))],
        out_specs=pl.BlockSpec((tm, D), lambda m: (m, 0)),
        out_shape=jax.ShapeDtypeStruct((MP + DB, D), BF16),
        compiler_params=_cparams(1),
        name="stack_rows_bf16",
    )(xp, xs)


def _proj_ln_kernel(*refs, n_full, tail, x_split):
    refs = list(refs)
    a_ref, w_ref, xp_ref = refs[:3]
    del refs[:3]
    xs_ref = refs.pop(0) if x_split else None
    g_ref, b_ref, o_ref, ob_ref, wb_ref = refs
    m = pl.program_id(0)

    @pl.when(m == 0)
    def _():
        wb_ref[...] = w_ref[...].astype(BF16)

    def emit(a, x, rs):
        y = jnp.dot(a, wb_ref[...], preferred_element_type=F32)
        o = _add_ln(x, y, g_ref[...], b_ref[...])
        o_ref[rs, :] = o
        ob_ref[rs, :] = o.astype(BF16)

    @pl.when(m < n_full)
    def _():
        emit(a_ref[...], xp_ref[...], slice(None))

    @pl.when(m == n_full)
    def _():
        emit(a_ref[0:tail, :], xs_ref[...] if x_split else xp_ref[0:tail, :], slice(0, tail))


def proj_residual_ln(a, w, layer, x, g, b, MP, *, tm=256):
    x_split = isinstance(x, tuple)
    M, K = a.shape
    N = w.shape[2]
    DB = M - MP
    assert MP % tm == 0 and 0 < DB < tm
    n_full = MP // tm
    row = pl.BlockSpec((tm, N), lambda m: (m, 0))
    prow = pl.BlockSpec((tm, N), lambda m: (jnp.minimum(m, n_full - 1), 0))
    srow = pl.BlockSpec((DB, N), lambda m: (0, 0))
    vec = pl.BlockSpec((1, N), lambda m: (0, 0))
    x_args = list(x) if x_split else [x]
    x_specs = [prow, srow] if x_split else [row]
    return pl.pallas_call(
        functools.partial(_proj_ln_kernel, n_full=n_full, tail=DB, x_split=x_split),
        grid=(n_full + 1,),
        in_specs=[pl.BlockSpec((tm, K), lambda m: (m, 0)),
                  pl.BlockSpec((None, K, N), lambda m: (layer, 0, 0))] + x_specs + [vec, vec],
        out_specs=[row, row],
        out_shape=[jax.ShapeDtypeStruct((M, N), F32), jax.ShapeDtypeStruct((M, N), BF16)],
        scratch_shapes=[pltpu.VMEM((K, N), BF16)],
        compiler_params=_cparams(1),
        name="proj_residual_ln",
    )(a, w, *x_args, g.reshape(1, N), b.reshape(1, N))


def _res_ln_kernel(x_ref, y_ref, g_ref, b_ref, o_ref, ob_ref, *, M, tm):
    def body(rs):
        o = _add_ln(x_ref[rs, :], y_ref[rs, :], g_ref[...], b_ref[...])
        o_ref[rs, :] = o
        ob_ref[rs, :] = o.astype(BF16)

    _for_rows(pl.program_id(0), M, tm, body)


def residual_ln(x, y, g, b, *, tm=256):
    M, D = y.shape
    row = pl.BlockSpec((tm, D), lambda m: (m, 0))
    vec = pl.BlockSpec((1, D), lambda m: (0, 0))
    return pl.pallas_call(
        functools.partial(_res_ln_kernel, M=M, tm=tm),
        grid=(pl.cdiv(M, tm),),
        in_specs=[row, row, vec, vec],
        out_specs=[row, row],
        out_shape=[jax.ShapeDtypeStruct((M, D), F32), jax.ShapeDtypeStruct((M, D), BF16)],
        compiler_params=_cparams(1),
        name="residual_ln",
    )(x, y, g.reshape(1, D), b.reshape(1, D))


def _t5_bucket(rel):
    n = jnp.maximum(rel, 0)
    large = MAX_EXACT + (jnp.log(jnp.maximum(n, 1).astype(F32) / MAX_EXACT)
                         / math.log(MAX_DISTANCE / MAX_EXACT) * (N_BUCKETS - MAX_EXACT)).astype(jnp.int32)
    large = jnp.clip(large, 0, N_BUCKETS - 1)
    return jnp.where(n < MAX_EXACT, n, large)


def _bucket_np(n):
    n = np.maximum(n, 0)
    large = MAX_EXACT + (np.log(np.maximum(n, 1) / MAX_EXACT)
                         / math.log(MAX_DISTANCE / MAX_EXACT) * (N_BUCKETS - MAX_EXACT)).astype(np.int64)
    return np.where(n < MAX_EXACT, n, np.clip(large, 0, N_BUCKETS - 1))


assert (_bucket_np(np.arange(min(ATT_T, PAGE_SIZE) + 1, 1 << 16)) == N_BUCKETS - 1).all()


def _bias_tiles_t(rel_bias, T):
    n = jnp.arange(-(T - 1), 3 * T)
    vec = jnp.where((n >= 0)[:, None], rel_bias[_t5_bucket(n)].astype(F32), NEG_INF).T
    L = 2 * T
    tiles = []
    for d in range(3):
        f0 = d * T + (T - 1)
        a = jnp.concatenate([vec[:, f0:f0 + T], vec[:, :1], vec[:, f0 - (T - 1):f0]], axis=1)
        skew = jnp.tile(a, (1, T))[:, :T * (L - 1)].reshape(H_A, T, L - 1)
        tiles.append(skew[:, :, :T])
    return jnp.stack(tiles, axis=1)


def _rms_head(o, g):
    return o * lax.rsqrt(jnp.mean(o * o, axis=-1, keepdims=True) + RMS_EPS) * g


def _prompt_attn_kernel(lam_ref, q_ref, k_ref, v_ref, bias_ref, g_ref, o_ref, kb_ref, vb_ref, s_ref, p_ref, *,
                        T, out_scale):
    qi = pl.program_id(2)
    n_half = T // ATT_HALF

    @pl.when(qi == 0)
    def _():
        kb_ref[...] = k_ref[...].astype(BF16)
        vb_ref[...] = v_ref[...].astype(BF16)

    lane = lax.broadcasted_iota(jnp.int32, (1, 2 * DK_A), 1)
    qm = []
    for hd in range(ATT_HEADS):
        q = q_ref[:, hd * DV_A:(hd + 1) * DV_A] * (DK_A ** -0.5 * LOG2E)
        qm.append((jnp.where(lane < DK_A, q, 0.0).astype(BF16), jnp.where(lane >= DK_A, q, 0.0).astype(BF16)))
    pairs = [(hd, mi) for hd in range(ATT_HEADS) for mi in range(2)]
    blocks = [(pi, hi) for pi in range(len(pairs)) for hi in range(n_half)]

    def kv_rows(kj):
        return pl.ds(pl.multiple_of(kj * T, T), T)

    def head_cols(hd):
        return slice(hd * DV_A, (hd + 1) * DV_A)

    def half_cols(hi):
        return slice(hi * ATT_HALF, (hi + 1) * ATT_HALF)

    def scores(kj):
        tile = jnp.clip(qi - kj, 0, 2)
        ks = [kb_ref[kv_rows(kj), head_cols(hd)] for hd in range(ATT_HEADS)]
        return [_dot_nt(ks[hd], qm[hd][mi]) + bias_ref[hd, tile] for hd, mi in pairs]

    def pv(kj):
        vs = [vb_ref[kv_rows(kj), head_cols(hd)] for hd in range(ATT_HEADS)]
        return [_dot_tn(vs[hd], p_ref[pi]) for pi, (hd, _) in enumerate(pairs)]

    for pi, s in enumerate(scores(0)):
        s_ref[pi] = s
    p_ref[...] = jnp.zeros(p_ref.shape, p_ref.dtype)

    def step(kj, carry):
        s_next = scores(jnp.minimum(kj + 1, qi))
        pv_prev = pv(jnp.maximum(kj - 1, 0))
        out = []
        for (pi, hi), (m, l, b) in zip(blocks, carry):
            s = s_ref[pi, :, half_cols(hi)]
            m_new = jnp.maximum(m, jnp.max(s, axis=0, keepdims=True))
            p = jnp.exp2(s - m_new)
            r = jnp.exp2(m - m_new)
            l_new = r * l + jnp.sum(p, axis=0, keepdims=True)
            p_ref[pi, :, half_cols(hi)] = p.astype(BF16)
            out.append((m_new, l_new, r * (b + pv_prev[pi][:, half_cols(hi)])))
        for pi in range(len(pairs)):
            s_ref[pi] = s_next[pi]
        return tuple(out)

    init = tuple((jnp.full((1, ATT_HALF), NEG_INF, F32), jnp.zeros((1, ATT_HALF), F32),
                  jnp.zeros((DV_A, ATT_HALF), F32)) for _ in blocks)
    res = lax.fori_loop(0, qi + 1, step, init)
    pv_last = pv(qi)
    lam = lam_ref[0]

    def normalised(hd, mi, hi):
        pi = pairs.index((hd, mi))
        _, l, b = res[blocks.index((pi, hi))]
        return (b + pv_last[pi][:, half_cols(hi)]) / l

    for hd in range(ATT_HEADS):
        for hi in range(n_half):
            ot = normalised(hd, 0, hi) - lam * normalised(hd, 1, hi)
            ot = ot * lax.rsqrt(jnp.mean(ot * ot, axis=0, keepdims=True) + RMS_EPS)
            o_ref[half_cols(hi), head_cols(hd)] = (
                ot.T * g_ref[:, head_cols(hd)] * out_scale).astype(o_ref.dtype)


def prompt_attention(z, k_all, v_all, layer, bias_tiles, lam, subln_g_l, lam_init, B, S):
    T = ATT_T
    nq = S // T
    W = ATT_HEADS * DV_A
    assert COLZ["q_a"] == 0 and H_A % ATT_HEADS == 0
    n_pairs = 2 * ATT_HEADS
    kern = functools.partial(_prompt_attn_kernel, T=T, out_scale=1.0 - lam_init)
    return pl.pallas_call(
        kern,
        grid=(B, H_A // ATT_HEADS, nq),
        in_specs=[
            pl.BlockSpec(memory_space=pltpu.SMEM),
            pl.BlockSpec((T, W), lambda b, h, i: (b * nq + i, h)),
            pl.BlockSpec((None, S, W), lambda b, h, i: (layer, b, h)),
            pl.BlockSpec((None, S, W), lambda b, h, i: (layer, b, h)),
            pl.BlockSpec((ATT_HEADS, 3, T, T), lambda b, h, i: (h, 0, 0, 0)),
            pl.BlockSpec((1, W), lambda b, h, i: (0, h)),
        ],
        out_specs=pl.BlockSpec((T, W), lambda b, h, i: (b * nq + i, h)),
        out_shape=jax.ShapeDtypeStruct((B * S, W_A), BF16),
        scratch_shapes=[pltpu.VMEM((S, W), BF16), pltpu.VMEM((S, W), BF16),
                        pltpu.VMEM((n_pairs, T, T), F32),
                        pltpu.VMEM((n_pairs, T, T), BF16)],
        compiler_params=_cparams(3),
        name="prompt_attention",
    )(lam.reshape(1), z, k_all, v_all, bias_tiles, subln_g_l.reshape(1, W_A))


_LEVELS = [CHUNK >> (i + 1) for i in range(int(math.log2(CHUNK)))]


def _cumsum_matrix():
    t = np.arange(CHUNK)
    return (t[None, :] <= t[:, None]).astype(np.float32)


def _anchor_rows(b, w):
    if 2 * w >= SUBLANES:
        g = b.reshape(CHUNK // (2 * w), 2 * w, b.shape[-1])
        return jnp.broadcast_to(g[:, w - 1:w, :], g.shape).reshape(b.shape)
    t = lax.broadcasted_iota(jnp.int32, b.shape, 0)
    off = (t & (2 * w - 1)) - (w - 1)
    out = b
    for d in range(-(w - 1), w + 1):
        if d != 0:
            out = jnp.where(off == d, pltpu.roll(b, d % CHUNK, 0), out)
    return out


def _split3(x):
    x1 = x.astype(BF16)
    r1 = x - x1.astype(F32)
    x2 = r1.astype(BF16)
    x3 = (r1 - x2.astype(F32)).astype(BF16)
    return x1, x2, x3


def _hgrn_gates(fx, qx, lb):
    f = lb + (1.0 - lb) * _sigmoid(fx)
    g = jnp.log(jnp.maximum(f, F_MIN))
    k = (1.0 - lb) * _sigmoid(-fx)
    q = qx * _sigmoid(qx)
    return g, k, q


def _hgrn_prompt_kernel(f_ref, q_ref, i_ref, og_ref, lb_ref, gain_ref, pm_ref, y_ref, s_ref,
                        oi_ref, qe_ref, u_ref, dl_ref, *, S):
    n_chunks = S // CHUNK
    lb = lb_ref[...]
    gain = gain_ref[...]
    pm = pm_ref[...]
    t_io = lax.broadcasted_iota(jnp.int32, (CHUNK, CHUNK), 0)
    s_io = lax.broadcasted_iota(jnp.int32, (CHUNK, CHUNK), 1)
    masks = [(((t_io ^ s_io) < 2 * w) & ((t_io & w) != 0)) & ((s_io & w) == 0) for w in _LEVELS]
    diag = t_io == s_io

    def group(gi, carry):
        cs = [gi * HGRN_UNROLL + u for u in range(HGRN_UNROLL)]
        rows = [pl.ds(pl.multiple_of(c * CHUNK, CHUNK), CHUNK) for c in cs]
        gkq = [_hgrn_gates(f_ref[r, :], q_ref[r, :], lb) for r in rows]
        vbs = [i_ref[r, :].astype(BF16) for r in rows]
        bs = []
        for g, _, _ in gkq:
            g1, g2, g3 = _split3(g)
            bs.append(jnp.dot(pm, g1, preferred_element_type=F32)
                      + jnp.dot(pm, g2, preferred_element_type=F32)
                      + jnp.dot(pm, g3, preferred_element_type=F32))
        a_s = [jnp.where(diag, _dot_nt(q.astype(BF16), k.astype(BF16)), 0.0) for _, k, q in gkq]
        for li, w in enumerate(_LEVELS):
            for u, (_, k, q) in enumerate(gkq):
                anc = _anchor_rows(bs[u], w)
                qs = (q * jnp.exp(jnp.minimum(bs[u] - anc, 0.0))).astype(BF16)
                ks = (k * jnp.exp(jnp.minimum(anc - bs[u], 0.0))).astype(BF16)
                a_s[u] = jnp.where(masks[li], _dot_nt(qs, ks), a_s[u])
        for u, (_, k, q) in enumerate(gkq):
            b = bs[u]
            b_last = b[CHUNK - 1:CHUNK, :]
            oi_ref[rows[u], :] = jnp.dot(a_s[u].astype(BF16), vbs[u], preferred_element_type=F32)
            qe_ref[rows[u], :] = (q * jnp.exp(b)).astype(BF16)
            kd = (k * jnp.exp(b_last - b)).astype(BF16)
            u_ref[cs[u]] = _dot_tn(vbs[u], kd)
            dl_ref[cs[u]] = jnp.exp(b_last)
        return carry

    lax.fori_loop(0, n_chunks // HGRN_UNROLL, group, 0)

    def phase_b(c, st):
        rows = pl.ds(pl.multiple_of(c * CHUNK, CHUNK), CHUNK)
        o = oi_ref[rows, :] + _dot_nt(qe_ref[rows, :], st.astype(BF16))
        y = _rms_head(o, gain) * _sigmoid(og_ref[rows, :])
        y_ref[rows, :] = y.astype(y_ref.dtype)
        return st * dl_ref[c] + u_ref[c]

    st = lax.fori_loop(0, n_chunks, phase_b, jnp.zeros((DV_B, DK_B), F32), unroll=HGRN_UNROLL)
    s_ref[...] = st.T


def hgrn_prompt(z, lb_l, hgrn_g_l, B, S):
    assert (S // CHUNK) % HGRN_UNROLL == 0
    fc, qc, ic, oc = (COLZ[n] // DK_B for n in ("f_b", "q_b", "i_b", "og_b"))
    pm = jnp.asarray(_cumsum_matrix(), BF16)
    kern = functools.partial(_hgrn_prompt_kernel, S=S)
    blk = lambda c0: pl.BlockSpec((S, DK_B), lambda b, h: (b, c0 + h))
    n_chunks = S // CHUNK
    return pl.pallas_call(
        kern,
        grid=(B, H_B),
        in_specs=[
            blk(fc), blk(qc), blk(ic), blk(oc),
            pl.BlockSpec((None, 1, DK_B), lambda b, h: (h, 0, 0)),
            pl.BlockSpec((None, 1, DV_B), lambda b, h: (h, 0, 0)),
            pl.BlockSpec(pm.shape, lambda b, h: (0, 0)),
        ],
        out_specs=[
            pl.BlockSpec((S, DV_B), lambda b, h: (b, h)),
            pl.BlockSpec((None, None, DK_B, DV_B), lambda b, h: (b, h, 0, 0)),
        ],
        out_shape=[
            jax.ShapeDtypeStruct((B * S, W_B), BF16),
            jax.ShapeDtypeStruct((B, H_B, DK_B, DV_B), F32),
        ],
        scratch_shapes=[pltpu.VMEM((S, DV_B), F32), pltpu.VMEM((S, DK_B), BF16),
                        pltpu.VMEM((n_chunks, DV_B, DK_B), F32), pltpu.VMEM((n_chunks, 1, DK_B), F32)],
        compiler_params=_cparams(2),
        name="hgrn_prompt",
    )(z, z, z, z, lb_l.reshape(H_B, 1, DK_B), hgrn_g_l.reshape(H_B, 1, DV_B), pm)


PAGE_ROWS = PAGE_SIZE * H_A
assert H_A & (H_A - 1) == 0


def _paged_attn_kernel(pt_ref, lam_ref, q_ref, kn_ref, vn_ref, bfar_ref, blast_ref, bnew_ref, g_ref, *rest,
                       n_groups, out_scale):
    G = PAGES_PER_STEP
    R = 2 * H_A
    k_refs = rest[:G]
    v_refs = rest[G:2 * G]
    o_ref = rest[2 * G]
    m_ref, l_ref, a_ref = rest[2 * G + 1:]
    j = pl.program_id(1)
    lane = lax.broadcasted_iota(jnp.int32, (1, 2 * DK_A), 1)
    q = q_ref[...] * (DK_A ** -0.5)
    qf = jnp.concatenate([jnp.where(lane < DK_A, q, 0.0), jnp.where(lane >= DK_A, q, 0.0)], axis=0)
    qb = qf.astype(BF16)
    col = lax.broadcasted_iota(jnp.int32, (R, PAGE_ROWS), 1)
    row = lax.broadcasted_iota(jnp.int32, (R, PAGE_ROWS), 0)
    valid = (col & (H_A - 1)) == (row & (H_A - 1))

    @pl.when(j == 0)
    def _():
        m_ref[...] = jnp.full(m_ref.shape, NEG_INF, F32)
        l_ref[...] = jnp.zeros(l_ref.shape, F32)
        a_ref[...] = jnp.zeros(a_ref.shape, F32)

    def scores(i, bias):
        s = _dot_nt(qb, k_refs[i][...].astype(BF16)) + bias
        return jnp.where(valid, s, NEG_INF)

    def absorb(s_list, new_token):
        m_old = m_ref[...]
        m_new = m_old
        for s in s_list:
            m_new = jnp.maximum(m_new, jnp.max(s, axis=-1, keepdims=True))
        if new_token:
            kn2 = jnp.concatenate([kn_ref[...], kn_ref[...]], axis=0)
            vn2 = jnp.concatenate([vn_ref[...], vn_ref[...]], axis=0)
            s_new = jnp.sum(qf * kn2, axis=-1, keepdims=True) + bnew_ref[...]
            m_new = jnp.maximum(m_new, s_new)
        r = jnp.exp(m_old - m_new)
        l = r * l_ref[...]
        acc = r * a_ref[...]
        for i, s in enumerate(s_list):
            p = jnp.exp(s - m_new)
            l = l + jnp.sum(p, axis=-1, keepdims=True)
            acc = acc + jnp.dot(p.astype(BF16), v_refs[i][...].astype(BF16), preferred_element_type=F32)
        if new_token:
            p_new = jnp.exp(s_new - m_new)
            l = l + p_new
            acc = acc + p_new * vn2
        m_ref[...] = m_new
        l_ref[...] = l
        a_ref[...] = acc

    far = bfar_ref[...]

    @pl.when(j < n_groups - 1)
    def _():
        absorb([scores(i, far) for i in range(G)], False)

    @pl.when(j == n_groups - 1)
    def _():
        absorb([scores(i, far) for i in range(G - 1)] + [scores(G - 1, blast_ref[...])], True)
        lam = lam_ref[0]
        o = a_ref[0:H_A] / l_ref[0:H_A] - lam * (a_ref[H_A:R] / l_ref[H_A:R])
        o_ref[...] = (_rms_head(o, g_ref[...]) * out_scale).astype(o_ref.dtype)


def sample_attention(q, k_new, v_new, cache_k, cache_v, page_table, rel_bias, lam, subln_g_l,
                     lam_init, layer):
    DB = q.shape[0]
    n_pages = page_table.shape[1]
    G = PAGES_PER_STEP
    R = 2 * H_A
    assert n_pages % G == 0
    n_groups = n_pages // G
    hmap = jnp.tile(jnp.arange(H_A), 2)
    bias_of = lambda dist: rel_bias[_t5_bucket(dist)].astype(F32)
    bfar = bias_of(jnp.array(2 * PAGE_SIZE))[hmap][:, None]
    bnew = bias_of(jnp.array(0))[hmap][:, None]
    blast = jnp.broadcast_to(bias_of(PAGE_SIZE - jnp.arange(PAGE_SIZE)).reshape(1, PAGE_ROWS), (R, PAGE_ROWS))
    kern = functools.partial(_paged_attn_kernel, n_groups=n_groups, out_scale=1.0 - lam_init)

    def page_spec(i):
        return pl.BlockSpec((None, None, PAGE_ROWS, DV_A),
                            lambda b, j, pt: (layer, pt[b * n_pages + j * G + i], 0, 0))

    tok = pl.BlockSpec((None, H_A, DV_A), lambda b, j, pt: (b, 0, 0))
    const = lambda shape: pl.BlockSpec(shape, lambda b, j, pt: (0,) * len(shape))
    grid_spec = pltpu.PrefetchScalarGridSpec(
        num_scalar_prefetch=1,
        grid=(DB, n_groups),
        in_specs=[
            pl.BlockSpec(memory_space=pltpu.SMEM),
            tok, tok, tok,
            const((R, 1)), const((R, PAGE_ROWS)), const((R, 1)), const((H_A, DV_A)),
        ] + [page_spec(i) for i in range(G)] + [page_spec(i) for i in range(G)],
        out_specs=tok,
        scratch_shapes=[pltpu.VMEM((R, 1), F32), pltpu.VMEM((R, 1), F32), pltpu.VMEM((R, DV_A), F32)],
    )
    return pl.pallas_call(
        kern,
        grid_spec=grid_spec,
        out_shape=jax.ShapeDtypeStruct((DB, H_A, DV_A), BF16),
        compiler_params=_cparams(2),
        name="sample_attention",
    )(page_table.reshape(-1), lam.reshape(1), q, k_new, v_new, bfar, blast, bnew,
      subln_g_l.reshape(H_A, DV_A), *([cache_k] * G), *([cache_v] * G))


def _hgrn_step_kernel(fx_ref, qx_ref, v_ref, og_ref, lb_ref, gain_ref, s_ref, so_in, y_ref, so_ref, *, DB):
    del so_in

    def columns(a):
        return jnp.concatenate([a, jnp.zeros((LANES - DB, a.shape[1]), F32)], axis=0).T

    g, k, q = _hgrn_gates(fx_ref[...], qx_ref[...], lb_ref[...])
    d_t, k_t, q_t = columns(jnp.exp(g)), columns(k), columns(q)
    v = v_ref[...]
    outs = []
    for b in range(DB):
        s_new = d_t[:, b:b + 1] * s_ref[b].astype(F32) + k_t[:, b:b + 1] * v[b:b + 1, :]
        so_ref[b] = s_new.astype(so_ref.dtype)
        outs.append(jnp.sum(q_t[:, b:b + 1] * s_new, axis=0, keepdims=True))
    o = jnp.concatenate(outs, axis=0)
    y = _rms_head(o, gain_ref[...]) * _sigmoid(og_ref[...])
    y_ref[...] = y.astype(y_ref.dtype)


def hgrn_step(z, MP, lb_l, hgrn_g_l, state, new_state, layer):
    DB = z.shape[0] - MP
    assert MP % DB == 0 and DB <= LANES and DK_B == LANES and new_state.shape == state.shape
    r0 = MP // DB
    blk = lambda name: pl.BlockSpec((DB, DK_B), lambda h: (r0, COLZ[name] // DK_B + h))
    st = pl.BlockSpec((None, DB, None, DK_B, DV_B), lambda h: (layer, 0, h, 0, 0))
    return pl.pallas_call(
        functools.partial(_hgrn_step_kernel, DB=DB),
        grid=(H_B,),
        in_specs=[blk("f_b"), blk("q_b"), blk("i_b"), blk("og_b"),
                  pl.BlockSpec((None, 1, DK_B), lambda h: (h, 0, 0)),
                  pl.BlockSpec((None, 1, DV_B), lambda h: (h, 0, 0)),
                  st, pl.BlockSpec(memory_space=pl.ANY)],
        out_specs=[pl.BlockSpec((DB, DV_B), lambda h: (0, h)), st],
        out_shape=[jax.ShapeDtypeStruct((DB, W_B), BF16),
                   jax.ShapeDtypeStruct(new_state.shape, new_state.dtype)],
        input_output_aliases={7: 1},
        compiler_params=_cparams(1),
        name="hgrn_step",
    )(z, z, z, z, lb_l.reshape(H_B, 1, DK_B), hgrn_g_l.reshape(H_B, 1, DV_B), state, new_state)


def _router_kernel(x_ref, w_ref, idx_ref, gate_ref):
    x1, x2, _ = _split3(x_ref[...])
    w1, w2, _ = _split3(w_ref[...])
    lt = _dot_nt(w1, x1) + (_dot_nt(w1, x2) + _dot_nt(w2, x1))
    e_io = lax.broadcasted_iota(jnp.int32, lt.shape, 0)
    m1 = jnp.max(lt, axis=0, keepdims=True)
    i1 = jnp.min(jnp.where(lt == m1, e_io, N_EXPERTS), axis=0, keepdims=True)
    lt2 = jnp.where(e_io == i1, -jnp.inf, lt)
    m2 = jnp.max(lt2, axis=0, keepdims=True)
    i2 = jnp.min(jnp.where(lt2 == m2, e_io, N_EXPERTS), axis=0, keepdims=True)
    e2 = jnp.exp(m2 - m1)
    den = 1.0 + e2
    idx_ref[0:1, :] = i1
    idx_ref[1:2, :] = i2
    gate_ref[0:1, :] = 1.0 / den
    gate_ref[1:2, :] = e2 / den


def router(x, router_w_l, *, tm=512):
    M, D = x.shape
    return pl.pallas_call(
        _router_kernel,
        grid=(pl.cdiv(M, tm),),
        in_specs=[pl.BlockSpec((tm, D), lambda m: (m, 0)),
                  pl.BlockSpec((N_EXPERTS, D), lambda m: (0, 0))],
        out_specs=[pl.BlockSpec((TOP_K, tm), lambda m: (0, m)), pl.BlockSpec((TOP_K, tm), lambda m: (0, m))],
        out_shape=[jax.ShapeDtypeStruct((TOP_K, M), jnp.int32), jax.ShapeDtypeStruct((TOP_K, M), F32)],
        compiler_params=_cparams(1),
        name="router",
    )(x, router_w_l.T)


def _scatter_rows_kernel(pos_ref, x_ref, xs_in, xs_hbm, sem, *, rows, M):
    del xs_in
    base = pl.program_id(0) * rows
    n = jnp.minimum(rows, M - base)

    def row_copy(kk, r, dst_row):
        return pltpu.make_async_copy(x_ref.at[r], xs_hbm.at[dst_row], sem)

    def issue(g, c):
        for u in range(DMA_UNROLL):
            r = g * DMA_UNROLL + u
            for kk in range(TOP_K):
                row_copy(kk, r, pos_ref[kk, base + r]).start()
        return c

    lax.fori_loop(0, n // DMA_UNROLL, issue, 0)

    def drain(g, c):
        for u in range(DMA_UNROLL):
            for kk in range(TOP_K):
                row_copy(kk, g * DMA_UNROLL + u, 0).wait()
        return c

    lax.fori_loop(0, n // DMA_UNROLL, drain, 0)


def scatter_rows(x, pos, P, *, rows=256):
    M, D = x.shape
    Mp = pl.cdiv(M, rows) * rows
    assert rows % DMA_UNROLL == 0 and M % DMA_UNROLL == 0
    x3 = x.reshape(M, D // LANES, LANES)
    pos = jnp.pad(pos, ((0, 0), (0, Mp - M)))
    xs0 = jnp.zeros((P, D // LANES, LANES), x.dtype)
    out = pl.pallas_call(
        functools.partial(_scatter_rows_kernel, rows=rows, M=M),
        grid_spec=pltpu.PrefetchScalarGridSpec(
            num_scalar_prefetch=1,
            grid=(Mp // rows,),
            in_specs=[pl.BlockSpec((rows, D // LANES, LANES), lambda i, p: (i, 0, 0)),
                      pl.BlockSpec(memory_space=pl.ANY)],
            out_specs=pl.BlockSpec(memory_space=pl.ANY),
            scratch_shapes=[pltpu.SemaphoreType.DMA(())],
        ),
        out_shape=jax.ShapeDtypeStruct(xs0.shape, x.dtype),
        input_output_aliases={2: 0},
        compiler_params=_cparams(1),
        name="scatter_rows",
    )(pos, x3, xs0)
    return out.reshape(P, D)


def _for_used_rows(n_rows, tm, o_ref, fn):
    n_steps = (n_rows + MOE_ROW_STEP - 1) // MOE_ROW_STEP
    for k in range(tm // MOE_ROW_STEP + 1):
        @pl.when(n_steps == k)
        def _(k=k):
            r = k * MOE_ROW_STEP
            if r:
                o_ref[0:r, :] = fn(slice(0, r)).astype(o_ref.dtype)
            if r < tm:
                o_ref[r:tm, :] = jnp.zeros((tm - r, o_ref.shape[1]), o_ref.dtype)


def _moe_up_kernel(te_ref, tf_ref, tr_ref, x_ref, w1_ref, w3_ref, o_ref, w1b_ref, w3b_ref):
    t = pl.program_id(1)

    @pl.when(tf_ref[t] == 1)
    def _():
        w1b_ref[...] = w1_ref[...].astype(BF16)
        w3b_ref[...] = w3_ref[...].astype(BF16)

    _for_used_rows(tr_ref[t], MOE_TM, o_ref, lambda rs: _swiglu_halves(x_ref[rs, :], w1b_ref, w3b_ref))


def moe_up(xs, w1, w3, layer, tile_expert, tile_first, tile_rows, *, tn):
    P, K = xs.shape
    N = w1.shape[3]
    tm = MOE_TM
    nt = P // tm
    wspec = pl.BlockSpec((None, None, K, tn), lambda n, t, te, tf, tr: (layer, te[t], 0, n))
    return pl.pallas_call(
        _moe_up_kernel,
        grid_spec=pltpu.PrefetchScalarGridSpec(
            num_scalar_prefetch=3,
            grid=(N // tn, nt),
            in_specs=[pl.BlockSpec((tm, K), lambda n, t, te, tf, tr: (t, 0)), wspec, wspec],
            out_specs=pl.BlockSpec((tm, tn), lambda n, t, te, tf, tr: (t, n)),
            scratch_shapes=[pltpu.VMEM((K, tn), BF16), pltpu.VMEM((K, tn), BF16)],
        ),
        out_shape=jax.ShapeDtypeStruct((P, N), BF16),
        compiler_params=_cparams(2),
        name="moe_up",
    )(tile_expert, tile_first, tile_rows, xs, w1, w3)


def _moe_down_kernel(te_ref, tf_ref, tr_ref, h_ref, w_ref, o_ref, wb_ref, *, sub, tm):
    t = pl.program_id(1)
    tt = t // sub
    part = t % sub

    @pl.when((tf_ref[tt] == 1) & (part == 0))
    def _():
        wb_ref[...] = w_ref[...].astype(BF16)

    n_rows = jnp.clip(tr_ref[tt] - part * tm, 0, tm)
    _for_used_rows(n_rows, tm, o_ref,
                   lambda rs: jnp.dot(h_ref[rs, :], wb_ref[...], preferred_element_type=F32))


def moe_down(h, w2, layer, tile_expert, tile_first, tile_rows, *, tm, tn):
    P, K = h.shape
    N = w2.shape[3]
    assert MOE_TM % tm == 0 and tm % MOE_ROW_STEP == 0
    sub = MOE_TM // tm
    return pl.pallas_call(
        functools.partial(_moe_down_kernel, sub=sub, tm=tm),
        grid_spec=pltpu.PrefetchScalarGridSpec(
            num_scalar_prefetch=3,
            grid=(N // tn, P // tm),
            in_specs=[pl.BlockSpec((tm, K), lambda n, t, te, tf, tr: (t, 0)),
                      pl.BlockSpec((None, None, K, tn), lambda n, t, te, tf, tr: (layer, te[t // sub], 0, n))],
            out_specs=pl.BlockSpec((tm, tn), lambda n, t, te, tf, tr: (t, n)),
            scratch_shapes=[pltpu.VMEM((K, tn), BF16)],
        ),
        out_shape=jax.ShapeDtypeStruct((P, N), F32),
        compiler_params=_cparams(2),
        name="moe_down",
    )(tile_expert, tile_first, tile_rows, h, w2)


def _moe_combine_ln_kernel(pos_ref, gate_ref, x_ref, g_ref, b_ref, e_hbm, op_ref, os_ref, buf_ref, sem, *,
                           rows, n_full, tail):
    i = pl.program_id(0)
    slot = i % 2

    def n_rows(step):
        return jnp.where(step < n_full, rows, tail)

    def row_copy(s, kk, r, src_row):
        return pltpu.make_async_copy(e_hbm.at[src_row], buf_ref.at[s, kk, r], sem.at[s])

    def issue(step, s):
        def body(g, c):
            for u in range(DMA_UNROLL):
                r = g * DMA_UNROLL + u
                for kk in range(TOP_K):
                    row_copy(s, kk, r, pos_ref[kk, step * rows + r]).start()
            return c

        lax.fori_loop(0, n_rows(step) // DMA_UNROLL, body, 0)

    @pl.when(i == 0)
    def _():
        issue(0, 0)

    @pl.when(i < n_full)
    def _():
        issue(i + 1, 1 - slot)

    def drain(g, c):
        for u in range(DMA_UNROLL):
            for kk in range(TOP_K):
                row_copy(slot, kk, g * DMA_UNROLL + u, 0).wait()
        return c

    lax.fori_loop(0, n_rows(i) // DMA_UNROLL, drain, 0)

    def finish(nr, out_ref):
        g0 = gate_ref[0:nr, 0:1]
        g1 = gate_ref[0:nr, 1:2]
        y = jnp.concatenate([g0 * buf_ref[slot, 0, 0:nr, j, :] + g1 * buf_ref[slot, 1, 0:nr, j, :]
                             for j in range(buf_ref.shape[3])], axis=-1)
        out_ref[...] = _add_ln(x_ref[0:nr, :], y, g_ref[...], b_ref[...])

    @pl.when(i < n_full)
    def _():
        finish(rows, op_ref)

    @pl.when(i == n_full)
    def _():
        finish(tail, os_ref)


def moe_combine_ln(eout, pos, gate, x, g, b, MP, *, rows=128):
    M, D = x.shape
    P = eout.shape[0]
    DB = M - MP
    assert MP % rows == 0 and 0 < DB < rows and rows % DMA_UNROLL == 0 and DB % DMA_UNROLL == 0
    n_full = MP // rows
    e3 = eout.reshape(P, D // LANES, LANES)
    pos = jnp.pad(pos, ((0, 0), (0, (n_full + 1) * rows - M)))
    vec = pl.BlockSpec((1, D), lambda i, p: (0, 0))
    return pl.pallas_call(
        functools.partial(_moe_combine_ln_kernel, rows=rows, n_full=n_full, tail=DB),
        grid_spec=pltpu.PrefetchScalarGridSpec(
            num_scalar_prefetch=1,
            grid=(n_full + 1,),
            in_specs=[pl.BlockSpec((rows, TOP_K), lambda i, p: (i, 0)),
                      pl.BlockSpec((rows, D), lambda i, p: (i, 0)),
                      vec, vec,
                      pl.BlockSpec(memory_space=pl.ANY)],
            out_specs=[pl.BlockSpec((rows, D), lambda i, p: (jnp.minimum(i, n_full - 1), 0)),
                       pl.BlockSpec((DB, D), lambda i, p: (0, 0))],
            scratch_shapes=[pltpu.VMEM((2, TOP_K, rows, D // LANES, LANES), F32),
                            pltpu.SemaphoreType.DMA((2,))],
        ),
        out_shape=[jax.ShapeDtypeStruct((MP, D), F32), jax.ShapeDtypeStruct((DB, D), F32)],
        compiler_params=_cparams(1),
        name="moe_combine_ln",
    )(pos, gate, x, g.reshape(1, D), b.reshape(1, D), e3)


def _routing_tables(idx, tm):
    M = idx.shape[1]
    NP = TOP_K * M
    nb = pl.cdiv(NP, LANES)
    e_flat = jnp.pad(idx.reshape(-1), (0, nb * LANES - NP), constant_values=N_EXPERTS)
    oh = (e_flat[None, :] == jnp.arange(N_EXPERTS)[:, None]).astype(F32).reshape(N_EXPERTS, nb, LANES)
    tri = jnp.asarray(np.triu(np.ones((LANES, LANES), np.float32)))
    within = jnp.einsum("ebj,ji->ebi", oh, tri, precision=lax.Precision.HIGHEST)
    blk_tot = within[..., -1]
    blk_pre = jnp.cumsum(blk_tot, axis=1) - blk_tot
    counts = (blk_pre[:, -1] + blk_tot[:, -1]).astype(jnp.int32)
    tiles_per = (counts + tm - 1) // tm
    tile_start = jnp.cumsum(tiles_per) - tiles_per
    rem = counts - (tiles_per - 1) * tm
    rank = within + blk_pre[..., None] - 1.0
    remf = rem.astype(F32)[:, None, None]
    row_in_expert = jnp.where(rank < remf, rank, rank - remf + tm)
    pos = jnp.sum(oh * (row_in_expert + (tile_start * tm).astype(F32)[:, None, None]), axis=0)
    pos = pos.reshape(-1)[:NP].astype(jnp.int32).reshape(TOP_K, M)
    n_tiles = NP // tm + N_EXPERTS
    t_io = jnp.arange(n_tiles)
    used = jnp.sum(tiles_per)
    tile_expert = jnp.clip(jnp.sum(t_io[:, None] >= tile_start[None, :], axis=1) - 1, 0, N_EXPERTS - 1)
    tile_valid = (t_io < used).astype(jnp.int32)
    last_used_expert = jnp.max(jnp.where(tile_valid == 1, tile_expert, 0))
    tile_expert = jnp.where(tile_valid == 1, tile_expert, last_used_expert).astype(jnp.int32)
    prev = jnp.concatenate([jnp.array([-1], jnp.int32), tile_expert[:-1]])
    tile_first = (tile_expert != prev).astype(jnp.int32)
    own = jnp.arange(N_EXPERTS)[None, :] == tile_expert[:, None]
    of_tile = lambda per_expert: jnp.sum(jnp.where(own, per_expert[None, :], 0), axis=1)
    is_first_of_expert = t_io == of_tile(tile_start)
    tile_rows = jnp.where(tile_valid == 1, jnp.where(is_first_of_expert, of_tile(rem), tm), 0).astype(jnp.int32)
    return pos, tile_expert, tile_first, tile_rows, n_tiles * tm


def moe_ffn_ln(x, xb, router_w_l, w1, w3, w2, layer, g, b, MP):
    idx, gate = router(x, router_w_l)
    pos, tile_expert, tile_first, tile_rows, P = _routing_tables(idx, MOE_TM)
    xs = scatter_rows(xb, pos, P)
    h = moe_up(xs, w1, w3, layer, tile_expert, tile_first, tile_rows, tn=1024)
    eout = moe_down(h, w2, layer, tile_expert, tile_first, tile_rows, tm=512, tn=512)
    return moe_combine_ln(eout, pos, gate.T, x, g, b, MP)


def kernel(x_prompt, x_sample, cache_k, cache_v, page_table, state_hgrn, rel_bias, w_in, w_pa, w_pb, w_out,
           lam_qk, subln_g, hgrn_g, lb_raw, ln_g, ln_b, ffn_w1, ffn_w3, ffn_w2, router_w, moe_w1, moe_w3, moe_w2):
    B, S, D = x_prompt.shape
    DB = x_sample.shape[0]
    MP = B * S
    lbs = jax.nn.softmax(lb_raw.astype(F32), axis=0)
    lbs = jnp.cumsum(lbs, axis=0) - lbs[0]
    bias_tiles = _bias_tiles_t(rel_bias, ATT_T) * LOG2E
    ck = cache_k.reshape(cache_k.shape[0], cache_k.shape[1], PAGE_ROWS, 2 * DK_A)
    cv = cache_v.reshape(cache_v.shape[0], cache_v.shape[1], PAGE_ROWS, DV_A)

    xp = x_prompt.reshape(MP, D)
    xs = x_sample.reshape(DB, D)
    x = (xp, xs)
    xb = stack_rows_bf16(xp, xs)
    kp = jnp.zeros((DEPTH, MP, W_A), F32)
    vp = jnp.zeros((DEPTH, MP, W_A), F32)
    ks = jnp.zeros((DEPTH, DB, W_A), F32)
    vs = jnp.zeros((DEPTH, DB, W_A), F32)
    ss = jnp.zeros(state_hgrn.shape, state_hgrn.dtype)
    sp = []
    for l in range(DEPTH):
        lam_init = 0.8 - 0.6 * math.exp(-0.3 * l)
        lq = lam_qk[l].astype(F32)
        lam = jnp.exp(jnp.sum(lq[0] * lq[1])) - jnp.exp(jnp.sum(lq[2] * lq[3])) + lam_init

        z, kp, vp, ks, vs = in_proj(xb, w_in, l, kp, vp, ks, vs, tm=1024, tn=W_A)
        zs = z[MP:]
        ya = prompt_attention(z, kp, vp, l, bias_tiles, lam, subln_g[l], lam_init, B, S)
        yb, s_p = hgrn_prompt(z, lbs[l], hgrn_g[l], B, S)
        heads = lambda a: a.reshape(DB, H_A, DV_A)
        ya_s = sample_attention(heads(zs[:, COLZ["q_a"]:COLZ["q_a"] + W_A]), heads(ks[l]), heads(vs[l]),
                                ck, cv, page_table, rel_bias, lam, subln_g[l], lam_init, l)
        yb_s, ss = hgrn_step(z, MP, lbs[l], hgrn_g[l], state_hgrn, ss, l)
        mg = branch_merge(ya, yb, ya_s.reshape(DB, W_A), yb_s, z, w_pa, w_pb, l, tm=1024, tn=512)
        x, xb = proj_residual_ln(mg, w_out, l, x, ln_g[l, 0], ln_b[l, 0], MP)

        j = l // 2
        if l % 2 == 0:
            hh = swiglu_up(xb, ffn_w1, ffn_w3, j, tm=1024, tn=512)
            c = matmul(hh, ffn_w2, j, tm=512, tn=512)
            x, xb = residual_ln(x, c, ln_g[l, 1], ln_b[l, 1])
        else:
            assert l == DEPTH - 1
            y_prompt, y_sample = moe_ffn_ln(x, xb, router_w[j], moe_w1, moe_w3, moe_w2, j,
                                            ln_g[l, 1], ln_b[l, 1], MP)
        sp.append(s_p)
    return (y_prompt.reshape(B, S, D), y_sample.reshape(DB, 1, D),
            kp.reshape(DEPTH, B, S, H_A, 2 * DK_A), vp.reshape(DEPTH, B, S, H_A, DV_A), jnp.stack(sp),
            ks.reshape(DEPTH, DB, 1, H_A, 2 * DK_A), vs.reshape(DEPTH, DB, 1, H_A, DV_A), ss)
```

```python
import functools
import math

import numpy as np
import jax
import jax.numpy as jnp
from jax import lax
from jax.experimental import pallas as pl
from jax.experimental.pallas import tpu as pltpu

F32 = jnp.float32
BF16 = jnp.bfloat16

D_MODEL = 2048
DEPTH = 2
PAGE_SIZE = 128
H_A = 8
DK_A = 64
DV_A = 2 * DK_A
W_A = H_A * DV_A
H_B = 8
DK_B = 128
DV_B = 128
W_BK = H_B * DK_B
W_B = H_B * DV_B
CHUNK = 64
N_BUCKETS = 32
MAX_EXACT = N_BUCKETS // 2
MAX_DISTANCE = 128
N_EXPERTS = 8
TOP_K = 2
ALPHA = (2 * DEPTH) ** 0.25
LN_EPS = 1e-5
RMS_EPS = 1e-6
F_MIN = 1e-20
NEG_INF = -1e30
LOG2E = 1.0 / math.log(2.0)
SPLITS = [H_A * 2 * DK_A, H_A * 2 * DK_A, W_A, W_BK, W_BK, W_B, W_B, D_MODEL, D_MODEL]
N_IN = sum(SPLITS)
_NAMES = ["q_a", "k_a", "v_a", "f_b", "q_b", "i_b", "og_b", "gt_a", "gt_b"]
COL = {}
COLZ = {}
_acc = _accz = 0
for _name, _w in zip(_NAMES, SPLITS):
    COL[_name] = _acc
    _acc += _w
    if _name not in ("k_a", "v_a"):
        COLZ[_name] = _accz
        _accz += _w
N_Z = _accz

LANES = 128
SUBLANES = 8
V7X_VMEM_BYTES = 64 * 1024 * 1024
VMEM_LIMIT = V7X_VMEM_BYTES - 2 * 1024 * 1024

ATT_T = 256
ATT_HALF = LANES
ATT_HEADS = 2
MOE_TM = 1024
MOE_ROW_STEP = 128
PAGES_PER_STEP = 8
DMA_UNROLL = 8
HGRN_UNROLL = 4


def _cparams(n_axes):
    return pltpu.CompilerParams(
        dimension_semantics=("arbitrary",) * n_axes, vmem_limit_bytes=VMEM_LIMIT)


def _sigmoid(x):
    return jax.nn.sigmoid(x)


def _dot_nt(a, b):
    return lax.dot_general(a, b, (((1,), (1,)), ((), ())), preferred_element_type=F32)


def _dot_tn(a, b):
    return lax.dot_general(a, b, (((0,), (0,)), ((), ())), preferred_element_type=F32)


def _for_rows(m, M, tm, fn):
    n_full, tail = M // tm, M % tm
    if n_full:
        @pl.when(m < n_full)
        def _():
            fn(slice(None))
    if tail:
        @pl.when(m == n_full)
        def _():
            fn(slice(0, tail))


def _mm_kernel(x_ref, w_ref, o_ref, wb_ref, *, M, tm):
    m = pl.program_id(1)

    @pl.when(m == 0)
    def _():
        wb_ref[...] = w_ref[...].astype(BF16)

    def body(rs):
        o_ref[rs, :] = jnp.dot(x_ref[rs, :], wb_ref[...], preferred_element_type=F32).astype(o_ref.dtype)

    _for_rows(m, M, tm, body)


def matmul(x, w, layer, *, tm, tn, out_dtype=F32):
    M, K = x.shape
    N = w.shape[2]
    assert N % tn == 0
    return pl.pallas_call(
        functools.partial(_mm_kernel, M=M, tm=tm),
        grid=(N // tn, pl.cdiv(M, tm)),
        in_specs=[
            pl.BlockSpec((tm, K), lambda n, m: (m, 0)),
            pl.BlockSpec((None, K, tn), lambda n, m: (layer, 0, n)),
        ],
        out_specs=pl.BlockSpec((tm, tn), lambda n, m: (m, n)),
        out_shape=jax.ShapeDtypeStruct((M, N), out_dtype),
        scratch_shapes=[pltpu.VMEM((K, tn), BF16)],
        compiler_params=_cparams(2),
        name="matmul",
    )(x, w)


def _in_proj_kernel(x_ref, w_ref, kp_in, vp_in, ks_in, vs_in, z_ref, kp_ref, vp_ref, ks_ref, vs_ref, wb_ref, *,
                    n_full, tail, nk, nv):
    del kp_in, vp_in, ks_in, vs_in
    n = pl.program_id(0)
    m = pl.program_id(1)

    @pl.when(m == 0)
    def _():
        wb_ref[...] = w_ref[...].astype(BF16)

    def emit(cond, full_ref, tail_ref, tail_rows):
        @pl.when(cond & (m < n_full))
        def _():
            full_ref[...] = jnp.dot(x_ref[...], wb_ref[...], preferred_element_type=F32)

        @pl.when(cond & (m == n_full))
        def _():
            tail_ref[tail_rows, :] = jnp.dot(x_ref[0:tail, :], wb_ref[...], preferred_element_type=F32)

    emit((n != nk) & (n != nv), z_ref, z_ref, slice(0, tail))
    emit(n == nk, kp_ref, ks_ref, slice(None))
    emit(n == nv, vp_ref, vs_ref, slice(None))


def in_proj(x, w_in, layer, kp, vp, ks, vs, *, tm, tn):
    M, K = x.shape
    MP, DB = kp.shape[1], ks.shape[1]
    assert tn == W_A and M == MP + DB and MP % tm == 0 and DB < tm and N_IN % tn == 0
    n_full = MP // tm
    nk, nv = COL["k_a"] // tn, COL["v_a"] // tn
    assert nv == nk + 1 and COL["q_a"] == 0

    def z_idx(n, m):
        hold = (n == nk) | (n == nv)
        return (jnp.where(hold, n_full, m), jnp.where(n < nk, n, jnp.where(hold, nk - 1, n - 2)))

    def p_idx(n_own):
        def idx(n, m):
            return (layer, jnp.where(n < n_own, 0, jnp.where(n == n_own, jnp.minimum(m, n_full - 1), n_full - 1)), 0)
        return idx

    s_idx = lambda n, m: (layer, 0, 0)
    anyspec = pl.BlockSpec(memory_space=pl.ANY)
    return pl.pallas_call(
        functools.partial(_in_proj_kernel, n_full=n_full, tail=DB, nk=nk, nv=nv),
        grid=(N_IN // tn, n_full + 1),
        in_specs=[
            pl.BlockSpec((tm, K), lambda n, m: (m, 0)),
            pl.BlockSpec((None, K, tn), lambda n, m: (layer, 0, n)),
            anyspec, anyspec, anyspec, anyspec,
        ],
        out_specs=[
            pl.BlockSpec((tm, tn), z_idx),
            pl.BlockSpec((None, tm, tn), p_idx(nk)),
            pl.BlockSpec((None, tm, tn), p_idx(nv)),
            pl.BlockSpec((None, DB, tn), s_idx),
            pl.BlockSpec((None, DB, tn), s_idx),
        ],
        out_shape=[jax.ShapeDtypeStruct((M, N_Z), F32)] + [jax.ShapeDtypeStruct(a.shape, a.dtype)
                                                            for a in (kp, vp, ks, vs)],
        input_output_aliases={2: 1, 3: 2, 4: 3, 5: 4},
        scratch_shapes=[pltpu.VMEM((K, tn), BF16)],
        compiler_params=_cparams(2),
        name="in_proj",
    )(x, w_in, kp, vp, ks, vs)


def _swiglu_halves(x, w1b_ref, w3b_ref):
    half = w1b_ref.shape[1] // 2
    ab = [(jnp.dot(x, w1b_ref[:, c * half:(c + 1) * half], preferred_element_type=F32),
           jnp.dot(x, w3b_ref[:, c * half:(c + 1) * half], preferred_element_type=F32)) for c in range(2)]
    return jnp.concatenate([a * _sigmoid(a) * b for a, b in ab], axis=-1)


def _swiglu_kernel(x_ref, w1_ref, w3_ref, o_ref, w1b_ref, w3b_ref, *, M, tm):
    m = pl.program_id(1)

    @pl.when(m == 0)
    def _():
        w1b_ref[...] = w1_ref[...].astype(BF16)
        w3b_ref[...] = w3_ref[...].astype(BF16)

    def body(rs):
        o_ref[rs, :] = _swiglu_halves(x_ref[rs, :], w1b_ref, w3b_ref).astype(o_ref.dtype)

    _for_rows(m, M, tm, body)


def swiglu_up(x, w1, w3, layer, *, tm, tn):
    M, K = x.shape
    N = w1.shape[2]
    assert N % tn == 0
    return pl.pallas_call(
        functools.partial(_swiglu_kernel, M=M, tm=tm),
        grid=(N // tn, pl.cdiv(M, tm)),
        in_specs=[
            pl.BlockSpec((tm, K), lambda n, m: (m, 0)),
            pl.BlockSpec((None, K, tn), lambda n, m: (layer, 0, n)),
            pl.BlockSpec((None, K, tn), lambda n, m: (layer, 0, n)),
        ],
        out_specs=pl.BlockSpec((tm, tn), lambda n, m: (m, n)),
        out_shape=jax.ShapeDtypeStruct((M, N), BF16),
        scratch_shapes=[pltpu.VMEM((K, tn), BF16), pltpu.VMEM((K, tn), BF16)],
        compiler_params=_cparams(2),
        name="swiglu_up",
    )(x, w1, w3)


def _merge_kernel(ya_ref, yb_ref, yas_ref, ybs_ref, wa_ref, wb_ref, ga_ref, gb_ref, o_ref, wab_ref, wbb_ref, *,
                  n_full, tail):
    m = pl.program_id(1)

    @pl.when(m == 0)
    def _():
        wab_ref[...] = wa_ref[...].astype(BF16)
        wbb_ref[...] = wb_ref[...].astype(BF16)

    def merge(ya, yb, rs):
        pa = jnp.dot(ya, wab_ref[...], preferred_element_type=F32)
        pb = jnp.dot(yb, wbb_ref[...], preferred_element_type=F32)
        o_ref[rs, :] = (_sigmoid(ga_ref[rs, :]) * pa + _sigmoid(gb_ref[rs, :]) * pb).astype(o_ref.dtype)

    @pl.when(m < n_full)
    def _():
        merge(ya_ref[...], yb_ref[...], slice(None))

    @pl.when(m == n_full)
    def _():
        merge(yas_ref[...], ybs_ref[...], slice(0, tail))


def branch_merge(ya, yb, ya_s, yb_s, z, w_pa, w_pb, layer, *, tm, tn):
    MP, K = ya.shape
    DB = ya_s.shape[0]
    M = z.shape[0]
    N = w_pa.shape[2]
    assert N % tn == 0 and MP % tm == 0 and M == MP + DB and DB < tm
    n_full = MP // tm
    ga0, gb0 = COLZ["gt_a"] // tn, COLZ["gt_b"] // tn
    prow = pl.BlockSpec((tm, K), lambda n, m: (jnp.minimum(m, n_full - 1), 0))
    srow = pl.BlockSpec((DB, K), lambda n, m: (0, 0))
    return pl.pallas_call(
        functools.partial(_merge_kernel, n_full=n_full, tail=DB),
        grid=(N // tn, n_full + 1),
        in_specs=[
            prow, prow, srow, srow,
            pl.BlockSpec((None, K, tn), lambda n, m: (layer, 0, n)),
            pl.BlockSpec((None, K, tn), lambda n, m: (layer, 0, n)),
            pl.BlockSpec((tm, tn), lambda n, m: (m, ga0 + n)),
            pl.BlockSpec((tm, tn), lambda n, m: (m, gb0 + n)),
        ],
        out_specs=pl.BlockSpec((tm, tn), lambda n, m: (m, n)),
        out_shape=jax.ShapeDtypeStruct((M, N), BF16),
        scratch_shapes=[pltpu.VMEM((K, tn), BF16), pltpu.VMEM((K, tn), BF16)],
        compiler_params=_cparams(2),
        name="branch_merge",
    )(ya, yb, ya_s, yb_s, w_pa, w_pb, z, z)


def _add_ln(x, y, g, b):
    h = ALPHA * x + y
    mu = jnp.mean(h, axis=-1, keepdims=True)
    c = h - mu
    var = jnp.mean(c * c, axis=-1, keepdims=True)
    return c * lax.rsqrt(var + LN_EPS) * g + b


def _stack_bf16_kernel(xp_ref, xs_ref, o_ref, *, n_full, tail):
    m = pl.program_id(0)

    @pl.when(m < n_full)
    def _():
        o_ref[...] = xp_ref[...].astype(BF16)

    @pl.when(m == n_full)
    def _():
        o_ref[0:tail, :] = xs_ref[...].astype(BF16)


def stack_rows_bf16(xp, xs, *, tm=512):
    MP, D = xp.shape
    DB = xs.shape[0]
    assert MP % tm == 0 and 0 < DB < tm
    n_full = MP // tm
    return pl.pallas_call(
        functools.partial(_stack_bf16_kernel, n_full=n_full, tail=DB),
        grid=(n_full + 1,),
        in_specs=[pl.BlockSpec((tm, D), lambda m: (jnp.minimum(m, n_full - 1), 0)),
                  pl.BlockSpec((DB, D), lambda m: (0, 0))],
        out_specs=pl.BlockSpec((tm, D), lambda m: (m, 0)),
        out_shape=jax.ShapeDtypeStruct((MP + DB, D), BF16),
        compiler_params=_cparams(1),
        name="stack_rows_bf16",
    )(xp, xs)


def _proj_ln_kernel(*refs, n_full, tail, x_split):
    refs = list(refs)
    a_ref, w_ref, xp_ref = refs[:3]
    del refs[:3]
    xs_ref = refs.pop(0) if x_split else None
    g_ref, b_ref, o_ref, ob_ref, wb_ref = refs
    m = pl.program_id(0)

    @pl.when(m == 0)
    def _():
        wb_ref[...] = w_ref[...].astype(BF16)

    def emit(a, x, rs):
        y = jnp.dot(a, wb_ref[...], preferred_element_type=F32)
        o = _add_ln(x, y, g_ref[...], b_ref[...])
        o_ref[rs, :] = o
        ob_ref[rs, :] = o.astype(BF16)

    @pl.when(m < n_full)
    def _():
        emit(a_ref[...], xp_ref[...], slice(None))

    @pl.when(m == n_full)
    def _():
        emit(a_ref[0:tail, :], xs_ref[...] if x_split else xp_ref[0:tail, :], slice(0, tail))


def proj_residual_ln(a, w, layer, x, g, b, MP, *, tm=256):
    x_split = isinstance(x, tuple)
    M, K = a.shape
    N = w.shape[2]
    DB = M - MP
    assert MP % tm == 0 and 0 < DB < tm
    n_full = MP // tm
    row = pl.BlockSpec((tm, N), lambda m: (m, 0))
    prow = pl.BlockSpec((tm, N), lambda m: (jnp.minimum(m, n_full - 1), 0))
    srow = pl.BlockSpec((DB, N), lambda m: (0, 0))
    vec = pl.BlockSpec((1, N), lambda m: (0, 0))
    x_args = list(x) if x_split else [x]
    x_specs = [prow, srow] if x_split else [row]
    return pl.pallas_call(
        functools.partial(_proj_ln_kernel, n_full=n_full, tail=DB, x_split=x_split),
        grid=(n_full + 1,),
        in_specs=[pl.BlockSpec((tm, K), lambda m: (m, 0)),
                  pl.BlockSpec((None, K, N), lambda m: (layer, 0, 0))] + x_specs + [vec, vec],
        out_specs=[row, row],
        out_shape=[jax.ShapeDtypeStruct((M, N), F32), jax.ShapeDtypeStruct((M, N), BF16)],
        scratch_shapes=[pltpu.VMEM((K, N), BF16)],
        compiler_params=_cparams(1),
        name="proj_residual_ln",
    )(a, w, *x_args, g.reshape(1, N), b.reshape(1, N))


def _res_ln_kernel(x_ref, y_ref, g_ref, b_ref, o_ref, ob_ref, *, M, tm):
    def body(rs):
        o = _add_ln(x_ref[rs, :], y_ref[rs, :], g_ref[...], b_ref[...])
        o_ref[rs, :] = o
        ob_ref[rs, :] = o.astype(BF16)

    _for_rows(pl.program_id(0), M, tm, body)


def residual_ln(x, y, g, b, *, tm=256):
    M, D = y.shape
    row = pl.BlockSpec((tm, D), lambda m: (m, 0))
    vec = pl.BlockSpec((1, D), lambda m: (0, 0))
    return pl.pallas_call(
        functools.partial(_res_ln_kernel, M=M, tm=tm),
        grid=(pl.cdiv(M, tm),),
        in_specs=[row, row, vec, vec],
        out_specs=[row, row],
        out_shape=[jax.ShapeDtypeStruct((M, D), F32), jax.ShapeDtypeStruct((M, D), BF16)],
        compiler_params=_cparams(1),
        name="residual_ln",
    )(x, y, g.reshape(1, D), b.reshape(1, D))


def _t5_bucket(rel):
    n = jnp.maximum(rel, 0)
    large = MAX_EXACT + (jnp.log(jnp.maximum(n, 1).astype(F32) / MAX_EXACT)
                         / math.log(MAX_DISTANCE / MAX_EXACT) * (N_BUCKETS - MAX_EXACT)).astype(jnp.int32)
    large = jnp.clip(large, 0, N_BUCKETS - 1)
    return jnp.where(n < MAX_EXACT, n, large)


def _bucket_np(n):
    n = np.maximum(n, 0)
    large = MAX_EXACT + (np.log(np.maximum(n, 1) / MAX_EXACT)
                         / math.log(MAX_DISTANCE / MAX_EXACT) * (N_BUCKETS - MAX_EXACT)).astype(np.int64)
    return np.where(n < MAX_EXACT, n, np.clip(large, 0, N_BUCKETS - 1))


assert (_bucket_np(np.arange(min(ATT_T, PAGE_SIZE) + 1, 1 << 16)) == N_BUCKETS - 1).all()


def _bias_tiles_t(rel_bias, T):
    n = jnp.arange(-(T - 1), 3 * T)
    vec = jnp.where((n >= 0)[:, None], rel_bias[_t5_bucket(n)].astype(F32), NEG_INF).T
    L = 2 * T
    tiles = []
    for d in range(3):
        f0 = d * T + (T - 1)
        a = jnp.concatenate([vec[:, f0:f0 + T], vec[:, :1], vec[:, f0 - (T - 1):f0]], axis=1)
        skew = jnp.tile(a, (1, T))[:, :T * (L - 1)].reshape(H_A, T, L - 1)
        tiles.append(skew[:, :, :T])
    return jnp.stack(tiles, axis=1)


def _rms_head(o, g):
    return o * lax.rsqrt(jnp.mean(o * o, axis=-1, keepdims=True) + RMS_EPS) * g


def _prompt_attn_kernel(lam_ref, q_ref, k_ref, v_ref, bias_ref, g_ref, o_ref, kb_ref, vb_ref, s_ref, p_ref, *,
                        T, out_scale):
    qi = pl.program_id(2)
    n_half = T // ATT_HALF

    @pl.when(qi == 0)
    def _():
        kb_ref[...] = k_ref[...].astype(BF16)
        vb_ref[...] = v_ref[...].astype(BF16)

    lane = lax.broadcasted_iota(jnp.int32, (1, 2 * DK_A), 1)
    qm = []
    for hd in range(ATT_HEADS):
        q = q_ref[:, hd * DV_A:(hd + 1) * DV_A] * (DK_A ** -0.5 * LOG2E)
        qm.append((jnp.where(lane < DK_A, q, 0.0).astype(BF16), jnp.where(lane >= DK_A, q, 0.0).astype(BF16)))
    pairs = [(hd, mi) for hd in range(ATT_HEADS) for mi in range(2)]
    blocks = [(pi, hi) for pi in range(len(pairs)) for hi in range(n_half)]

    def kv_rows(kj):
        return pl.ds(pl.multiple_of(kj * T, T), T)

    def head_cols(hd):
        return slice(hd * DV_A, (hd + 1) * DV_A)

    def half_cols(hi):
        return slice(hi * ATT_HALF, (hi + 1) * ATT_HALF)

    def scores(kj):
        tile = jnp.clip(qi - kj, 0, 2)
        ks = [kb_ref[kv_rows(kj), head_cols(hd)] for hd in range(ATT_HEADS)]
        return [_dot_nt(ks[hd], qm[hd][mi]) + bias_ref[hd, tile] for hd, mi in pairs]

    def pv(kj):
        vs = [vb_ref[kv_rows(kj), head_cols(hd)] for hd in range(ATT_HEADS)]
        return [_dot_tn(vs[hd], p_ref[pi]) for pi, (hd, _) in enumerate(pairs)]

    for pi, s in enumerate(scores(0)):
        s_ref[pi] = s
    p_ref[...] = jnp.zeros(p_ref.shape, p_ref.dtype)

    def step(kj, carry):
        s_next = scores(jnp.minimum(kj + 1, qi))
        pv_prev = pv(jnp.maximum(kj - 1, 0))
        out = []
        for (pi, hi), (m, l, b) in zip(blocks, carry):
            s = s_ref[pi, :, half_cols(hi)]
            m_new = jnp.maximum(m, jnp.max(s, axis=0, keepdims=True))
            p = jnp.exp2(s - m_new)
            r = jnp.exp2(m - m_new)
            l_new = r * l + jnp.sum(p, axis=0, keepdims=True)
            p_ref[pi, :, half_cols(hi)] = p.astype(BF16)
            out.append((m_new, l_new, r * (b + pv_prev[pi][:, half_cols(hi)])))
        for pi in range(len(pairs)):
            s_ref[pi] = s_next[pi]
        return tuple(out)

    init = tuple((jnp.full((1, ATT_HALF), NEG_INF, F32), jnp.zeros((1, ATT_HALF), F32),
                  jnp.zeros((DV_A, ATT_HALF), F32)) for _ in blocks)
    res = lax.fori_loop(0, qi + 1, step, init)
    pv_last = pv(qi)
    lam = lam_ref[0]

    def normalised(hd, mi, hi):
        pi = pairs.index((hd, mi))
        _, l, b = res[blocks.index((pi, hi))]
        return (b + pv_last[pi][:, half_cols(hi)]) / l

    for hd in range(ATT_HEADS):
        for hi in range(n_half):
            ot = normalised(hd, 0, hi) - lam * normalised(hd, 1, hi)
            ot = ot * lax.rsqrt(jnp.mean(ot * ot, axis=0, keepdims=True) + RMS_EPS)
            o_ref[half_cols(hi), head_cols(hd)] = (
                ot.T * g_ref[:, head_cols(hd)] * out_scale).astype(o_ref.dtype)


def prompt_attention(z, k_all, v_all, layer, bias_tiles, lam, subln_g_l, lam_init, B, S):
    T = ATT_T
    nq = S // T
    W = ATT_HEADS * DV_A
    assert COLZ["q_a"] == 0 and H_A % ATT_HEADS == 0
    n_pairs = 2 * ATT_HEADS
    kern = functools.partial(_prompt_attn_kernel, T=T, out_scale=1.0 - lam_init)
    return pl.pallas_call(
        kern,
        grid=(B, H_A // ATT_HEADS, nq),
        in_specs=[
            pl.BlockSpec(memory_space=pltpu.SMEM),
            pl.BlockSpec((T, W), lambda b, h, i: (b * nq + i, h)),
            pl.BlockSpec((None, S, W), lambda b, h, i: (layer, b, h)),
            pl.BlockSpec((None, S, W), lambda b, h, i: (layer, b, h)),
            pl.BlockSpec((ATT_HEADS, 3, T, T), lambda b, h, i: (h, 0, 0, 0)),
            pl.BlockSpec((1, W), lambda b, h, i: (0, h)),
        ],
        out_specs=pl.BlockSpec((T, W), lambda b, h, i: (b * nq + i, h)),
        out_shape=jax.ShapeDtypeStruct((B * S, W_A), BF16),
        scratch_shapes=[pltpu.VMEM((S, W), BF16), pltpu.VMEM((S, W), BF16),
                        pltpu.VMEM((n_pairs, T, T), F32),
                        pltpu.VMEM((n_pairs, T, T), BF16)],
        compiler_params=_cparams(3),
        name="prompt_attention",
    )(lam.reshape(1), z, k_all, v_all, bias_tiles, subln_g_l.reshape(1, W_A))


_LEVELS = [CHUNK >> (i + 1) for i in range(int(math.log2(CHUNK)))]


def _cumsum_matrix():
    t = np.arange(CHUNK)
    return (t[None, :] <= t[:, None]).astype(np.float32)


def _anchor_rows(b, w):
    if 2 * w >= SUBLANES:
        g = b.reshape(CHUNK // (2 * w), 2 * w, b.shape[-1])
        return jnp.broadcast_to(g[:, w - 1:w, :], g.shape).reshape(b.shape)
    t = lax.broadcasted_iota(jnp.int32, b.shape, 0)
    off = (t & (2 * w - 1)) - (w - 1)
    out = b
    for d in range(-(w - 1), w + 1):
        if d != 0:
            out = jnp.where(off == d, pltpu.roll(b, d % CHUNK, 0), out)
    return out


def _split3(x):
    x1 = x.astype(BF16)
    r1 = x - x1.astype(F32)
    x2 = r1.astype(BF16)
    x3 = (r1 - x2.astype(F32)).astype(BF16)
    return x1, x2, x3


def _hgrn_gates(fx, qx, lb):
    f = lb + (1.0 - lb) * _sigmoid(fx)
    g = jnp.log(jnp.maximum(f, F_MIN))
    k = (1.0 - lb) * _sigmoid(-fx)
    q = qx * _sigmoid(qx)
    return g, k, q


def _hgrn_prompt_kernel(f_ref, q_ref, i_ref, og_ref, lb_ref, gain_ref, pm_ref, y_ref, s_ref,
                        oi_ref, qe_ref, u_ref, dl_ref, *, S):
    n_chunks = S // CHUNK
    lb = lb_ref[...]
    gain = gain_ref[...]
    pm = pm_ref[...]
    t_io = lax.broadcasted_iota(jnp.int32, (CHUNK, CHUNK), 0)
    s_io = lax.broadcasted_iota(jnp.int32, (CHUNK, CHUNK), 1)
    masks = [(((t_io ^ s_io) < 2 * w) & ((t_io & w) != 0)) & ((s_io & w) == 0) for w in _LEVELS]
    diag = t_io == s_io

    def group(gi, carry):
        cs = [gi * HGRN_UNROLL + u for u in range(HGRN_UNROLL)]
        rows = [pl.ds(pl.multiple_of(c * CHUNK, CHUNK), CHUNK) for c in cs]
        gkq = [_hgrn_gates(f_ref[r, :], q_ref[r, :], lb) for r in rows]
        vbs = [i_ref[r, :].astype(BF16) for r in rows]
        bs = []
        for g, _, _ in gkq:
            g1, g2, g3 = _split3(g)
            bs.append(jnp.dot(pm, g1, preferred_element_type=F32)
                      + jnp.dot(pm, g2, preferred_element_type=F32)
                      + jnp.dot(pm, g3, preferred_element_type=F32))
        a_s = [jnp.where(diag, _dot_nt(q.astype(BF16), k.astype(BF16)), 0.0) for _, k, q in gkq]
        for li, w in enumerate(_LEVELS):
            for u, (_, k, q) in enumerate(gkq):
                anc = _anchor_rows(bs[u], w)
                qs = (q * jnp.exp(jnp.minimum(bs[u] - anc, 0.0))).astype(BF16)
                ks = (k * jnp.exp(jnp.minimum(anc - bs[u], 0.0))).astype(BF16)
                a_s[u] = jnp.where(masks[li], _dot_nt(qs, ks), a_s[u])
        for u, (_, k, q) in enumerate(gkq):
            b = bs[u]
            b_last = b[CHUNK - 1:CHUNK, :]
            oi_ref[rows[u], :] = jnp.dot(a_s[u].astype(BF16), vbs[u], preferred_element_type=F32)
            qe_ref[rows[u], :] = (q * jnp.exp(b)).astype(BF16)
            kd = (k * jnp.exp(b_last - b)).astype(BF16)
            u_ref[cs[u]] = _dot_tn(vbs[u], kd)
            dl_ref[cs[u]] = jnp.exp(b_last)
        return carry

    lax.fori_loop(0, n_chunks // HGRN_UNROLL, group, 0)

    def phase_b(c, st):
        rows = pl.ds(pl.multiple_of(c * CHUNK, CHUNK), CHUNK)
        o = oi_ref[rows, :] + _dot_nt(qe_ref[rows, :], st.astype(BF16))
        y = _rms_head(o, gain) * _sigmoid(og_ref[rows, :])
        y_ref[rows, :] = y.astype(y_ref.dtype)
        return st * dl_ref[c] + u_ref[c]

    st = lax.fori_loop(0, n_chunks, phase_b, jnp.zeros((DV_B, DK_B), F32), unroll=HGRN_UNROLL)
    s_ref[...] = st.T


def hgrn_prompt(z, lb_l, hgrn_g_l, B, S):
    assert (S // CHUNK) % HGRN_UNROLL == 0
    fc, qc, ic, oc = (COLZ[n] // DK_B for n in ("f_b", "q_b", "i_b", "og_b"))
    pm = jnp.asarray(_cumsum_matrix(), BF16)
    kern = functools.partial(_hgrn_prompt_kernel, S=S)
    blk = lambda c0: pl.BlockSpec((S, DK_B), lambda b, h: (b, c0 + h))
    n_chunks = S // CHUNK
    return pl.pallas_call(
        kern,
        grid=(B, H_B),
        in_specs=[
            blk(fc), blk(qc), blk(ic), blk(oc),
            pl.BlockSpec((None, 1, DK_B), lambda b, h: (h, 0, 0)),
            pl.BlockSpec((None, 1, DV_B), lambda b, h: (h, 0, 0)),
            pl.BlockSpec(pm.shape, lambda b, h: (0, 0)),
        ],
        out_specs=[
            pl.BlockSpec((S, DV_B), lambda b, h: (b, h)),
            pl.BlockSpec((None, None, DK_B, DV_B), lambda b, h: (b, h, 0, 0)),
        ],
        out_shape=[
            jax.ShapeDtypeStruct((B * S, W_B), BF16),
            jax.ShapeDtypeStruct((B, H_B, DK_B, DV_B), F32),
        ],
        scratch_shapes=[pltpu.VMEM((S, DV_B), F32), pltpu.VMEM((S, DK_B), BF16),
                        pltpu.VMEM((n_chunks, DV_B, DK_B), F32), pltpu.VMEM((n_chunks, 1, DK_B), F32)],
        compiler_params=_cparams(2),
        name="hgrn_prompt",
    )(z, z, z, z, lb_l.reshape(H_B, 1, DK_B), hgrn_g_l.reshape(H_B, 1, DV_B), pm)


PAGE_ROWS = PAGE_SIZE * H_A
assert H_A & (H_A - 1) == 0


def _paged_attn_kernel(pt_ref, lam_ref, q_ref, kn_ref, vn_ref, bfar_ref, blast_ref, bnew_ref, g_ref, *rest,
                       n_groups, out_scale):
    G = PAGES_PER_STEP
    R = 2 * H_A
    k_refs = rest[:G]
    v_refs = rest[G:2 * G]
    o_ref = rest[2 * G]
    m_ref, l_ref, a_ref = rest[2 * G + 1:]
    j = pl.program_id(1)
    lane = lax.broadcasted_iota(jnp.int32, (1, 2 * DK_A), 1)
    q = q_ref[...] * (DK_A ** -0.5)
    qf = jnp.concatenate([jnp.where(lane < DK_A, q, 0.0), jnp.where(lane >= DK_A, q, 0.0)], axis=0)
    qb = qf.astype(BF16)
    col = lax.broadcasted_iota(jnp.int32, (R, PAGE_ROWS), 1)
    row = lax.broadcasted_iota(jnp.int32, (R, PAGE_ROWS), 0)
    valid = (col & (H_A - 1)) == (row & (H_A - 1))

    @pl.when(j == 0)
    def _():
        m_ref[...] = jnp.full(m_ref.shape, NEG_INF, F32)
        l_ref[...] = jnp.zeros(l_ref.shape, F32)
        a_ref[...] = jnp.zeros(a_ref.shape, F32)

    def scores(i, bias):
        s = _dot_nt(qb, k_refs[i][...].astype(BF16)) + bias
        return jnp.where(valid, s, NEG_INF)

    def absorb(s_list, new_token):
        m_old = m_ref[...]
        m_new = m_old
        for s in s_list:
            m_new = jnp.maximum(m_new, jnp.max(s, axis=-1, keepdims=True))
        if new_token:
            kn2 = jnp.concatenate([kn_ref[...], kn_ref[...]], axis=0)
            vn2 = jnp.concatenate([vn_ref[...], vn_ref[...]], axis=0)
            s_new = jnp.sum(qf * kn2, axis=-1, keepdims=True) + bnew_ref[...]
            m_new = jnp.maximum(m_new, s_new)
        r = jnp.exp(m_old - m_new)
        l = r * l_ref[...]
        acc = r * a_ref[...]
        for i, s in enumerate(s_list):
            p = jnp.exp(s - m_new)
            l = l + jnp.sum(p, axis=-1, keepdims=True)
            acc = acc + jnp.dot(p.astype(BF16), v_refs[i][...].astype(BF16), preferred_element_type=F32)
        if new_token:
            p_new = jnp.exp(s_new - m_new)
            l = l + p_new
            acc = acc + p_new * vn2
        m_ref[...] = m_new
        l_ref[...] = l
        a_ref[...] = acc

    far = bfar_ref[...]

    @pl.when(j < n_groups - 1)
    def _():
        absorb([scores(i, far) for i in range(G)], False)

    @pl.when(j == n_groups - 1)
    def _():
        absorb([scores(i, far) for i in range(G - 1)] + [scores(G - 1, blast_ref[...])], True)
        lam = lam_ref[0]
        o = a_ref[0:H_A] / l_ref[0:H_A] - lam * (a_ref[H_A:R] / l_ref[H_A:R])
        o_ref[...] = (_rms_head(o, g_ref[...]) * out_scale).astype(o_ref.dtype)


def sample_attention(q, k_new, v_new, cache_k, cache_v, page_table, rel_bias, lam, subln_g_l,
                     lam_init, layer):
    DB = q.shape[0]
    n_pages = page_table.shape[1]
    G = PAGES_PER_STEP
    R = 2 * H_A
    assert n_pages % G == 0
    n_groups = n_pages // G
    hmap = jnp.tile(jnp.arange(H_A), 2)
    bias_of = lambda dist: rel_bias[_t5_bucket(dist)].astype(F32)
    bfar = bias_of(jnp.array(2 * PAGE_SIZE))[hmap][:, None]
    bnew = bias_of(jnp.array(0))[hmap][:, None]
    blast = jnp.broadcast_to(bias_of(PAGE_SIZE - jnp.arange(PAGE_SIZE)).reshape(1, PAGE_ROWS), (R, PAGE_ROWS))
    kern = functools.partial(_paged_attn_kernel, n_groups=n_groups, out_scale=1.0 - lam_init)

    def page_spec(i):
        return pl.BlockSpec((None, None, PAGE_ROWS, DV_A),
                            lambda b, j, pt: (layer, pt[b * n_pages + j * G + i], 0, 0))

    tok = pl.BlockSpec((None, H_A, DV_A), lambda b, j, pt: (b, 0, 0))
    const = lambda shape: pl.BlockSpec(shape, lambda b, j, pt: (0,) * len(shape))
    grid_spec = pltpu.PrefetchScalarGridSpec(
        num_scalar_prefetch=1,
        grid=(DB, n_groups),
        in_specs=[
            pl.BlockSpec(memory_space=pltpu.SMEM),
            tok, tok, tok,
            const((R, 1)), const((R, PAGE_ROWS)), const((R, 1)), const((H_A, DV_A)),
        ] + [page_spec(i) for i in range(G)] + [page_spec(i) for i in range(G)],
        out_specs=tok,
        scratch_shapes=[pltpu.VMEM((R, 1), F32), pltpu.VMEM((R, 1), F32), pltpu.VMEM((R, DV_A), F32)],
    )
    return pl.pallas_call(
        kern,
        grid_spec=grid_spec,
        out_shape=jax.ShapeDtypeStruct((DB, H_A, DV_A), BF16),
        compiler_params=_cparams(2),
        name="sample_attention",
    )(page_table.reshape(-1), lam.reshape(1), q, k_new, v_new, bfar, blast, bnew,
      subln_g_l.reshape(H_A, DV_A), *([cache_k] * G), *([cache_v] * G))


def _hgrn_step_kernel(fx_ref, qx_ref, v_ref, og_ref, lb_ref, gain_ref, s_ref, so_in, y_ref, so_ref, *, DB):
    del so_in

    def columns(a):
        return jnp.concatenate([a, jnp.zeros((LANES - DB, a.shape[1]), F32)], axis=0).T

    g, k, q = _hgrn_gates(fx_ref[...], qx_ref[...], lb_ref[...])
    d_t, k_t, q_t = columns(jnp.exp(g)), columns(k), columns(q)
    v = v_ref[...]
    outs = []
    for b in range(DB):
        s_new = d_t[:, b:b + 1] * s_ref[b].astype(F32) + k_t[:, b:b + 1] * v[b:b + 1, :]
        so_ref[b] = s_new.astype(so_ref.dtype)
        outs.append(jnp.sum(q_t[:, b:b + 1] * s_new, axis=0, keepdims=True))
    o = jnp.concatenate(outs, axis=0)
    y = _rms_head(o, gain_ref[...]) * _sigmoid(og_ref[...])
    y_ref[...] = y.astype(y_ref.dtype)


def hgrn_step(z, MP, lb_l, hgrn_g_l, state, new_state, layer):
    DB = z.shape[0] - MP
    assert MP % DB == 0 and DB <= LANES and DK_B == LANES and new_state.shape == state.shape
    r0 = MP // DB
    blk = lambda name: pl.BlockSpec((DB, DK_B), lambda h: (r0, COLZ[name] // DK_B + h))
    st = pl.BlockSpec((None, DB, None, DK_B, DV_B), lambda h: (layer, 0, h, 0, 0))
    return pl.pallas_call(
        functools.partial(_hgrn_step_kernel, DB=DB),
        grid=(H_B,),
        in_specs=[blk("f_b"), blk("q_b"), blk("i_b"), blk("og_b"),
                  pl.BlockSpec((None, 1, DK_B), lambda h: (h, 0, 0)),
                  pl.BlockSpec((None, 1, DV_B), lambda h: (h, 0, 0)),
                  st, pl.BlockSpec(memory_space=pl.ANY)],
        out_specs=[pl.BlockSpec((DB, DV_B), lambda h: (0, h)), st],
        out_shape=[jax.ShapeDtypeStruct((DB, W_B), BF16),
                   jax.ShapeDtypeStruct(new_state.shape, new_state.dtype)],
        input_output_aliases={7: 1},
        compiler_params=_cparams(1),
        name="hgrn_step",
    )(z, z, z, z, lb_l.reshape(H_B, 1, DK_B), hgrn_g_l.reshape(H_B, 1, DV_B), state, new_state)


def _router_kernel(x_ref, w_ref, idx_ref, gate_ref):
    x1, x2, _ = _split3(x_ref[...])
    w1, w2, _ = _split3(w_ref[...])
    lt = _dot_nt(w1, x1) + (_dot_nt(w1, x2) + _dot_nt(w2, x1))
    e_io = lax.broadcasted_iota(jnp.int32, lt.shape, 0)
    m1 = jnp.max(lt, axis=0, keepdims=True)
    i1 = jnp.min(jnp.where(lt == m1, e_io, N_EXPERTS), axis=0, keepdims=True)
    lt2 = jnp.where(e_io == i1, -jnp.inf, lt)
    m2 = jnp.max(lt2, axis=0, keepdims=True)
    i2 = jnp.min(jnp.where(lt2 == m2, e_io, N_EXPERTS), axis=0, keepdims=True)
    e2 = jnp.exp(m2 - m1)
    den = 1.0 + e2
    idx_ref[0:1, :] = i1
    idx_ref[1:2, :] = i2
    gate_ref[0:1, :] = 1.0 / den
    gate_ref[1:2, :] = e2 / den


def router(x, router_w_l, *, tm=512):
    M, D = x.shape
    return pl.pallas_call(
        _router_kernel,
        grid=(pl.cdiv(M, tm),),
        in_specs=[pl.BlockSpec((tm, D), lambda m: (m, 0)),
                  pl.BlockSpec((N_EXPERTS, D), lambda m: (0, 0))],
        out_specs=[pl.BlockSpec((TOP_K, tm), lambda m: (0, m)), pl.BlockSpec((TOP_K, tm), lambda m: (0, m))],
        out_shape=[jax.ShapeDtypeStruct((TOP_K, M), jnp.int32), jax.ShapeDtypeStruct((TOP_K, M), F32)],
        compiler_params=_cparams(1),
        name="router",
    )(x, router_w_l.T)


def _scatter_rows_kernel(pos_ref, x_ref, xs_in, xs_hbm, sem, *, rows, M):
    del xs_in
    base = pl.program_id(0) * rows
    n = jnp.minimum(rows, M - base)

    def row_copy(kk, r, dst_row):
        return pltpu.make_async_copy(x_ref.at[r], xs_hbm.at[dst_row], sem)

    def issue(g, c):
        for u in range(DMA_UNROLL):
            r = g * DMA_UNROLL + u
            for kk in range(TOP_K):
                row_copy(kk, r, pos_ref[kk, base + r]).start()
        return c

    lax.fori_loop(0, n // DMA_UNROLL, issue, 0)

    def drain(g, c):
        for u in range(DMA_UNROLL):
            for kk in range(TOP_K):
                row_copy(kk, g * DMA_UNROLL + u, 0).wait()
        return c

    lax.fori_loop(0, n // DMA_UNROLL, drain, 0)


def scatter_rows(x, pos, P, *, rows=512):
    M, D = x.shape
    Mp = pl.cdiv(M, rows) * rows
    assert rows % DMA_UNROLL == 0 and M % DMA_UNROLL == 0
    x3 = x.reshape(M, D // LANES, LANES)
    pos = jnp.pad(pos, ((0, 0), (0, Mp - M)))
    xs0 = jnp.zeros((P, D // LANES, LANES), x.dtype)
    out = pl.pallas_call(
        functools.partial(_scatter_rows_kernel, rows=rows, M=M),
        grid_spec=pltpu.PrefetchScalarGridSpec(
            num_scalar_prefetch=1,
            grid=(Mp // rows,),
            in_specs=[pl.BlockSpec((rows, D // LANES, LANES), lambda i, p: (i, 0, 0)),
                      pl.BlockSpec(memory_space=pl.ANY)],
            out_specs=pl.BlockSpec(memory_space=pl.ANY),
            scratch_shapes=[pltpu.SemaphoreType.DMA(())],
        ),
        out_shape=jax.ShapeDtypeStruct(xs0.shape, x.dtype),
        input_output_aliases={2: 0},
        compiler_params=_cparams(1),
        name="scatter_rows",
    )(pos, x3, xs0)
    return out.reshape(P, D)


def _for_used_rows(n_rows, tm, o_ref, fn):
    n_steps = (n_rows + MOE_ROW_STEP - 1) // MOE_ROW_STEP
    for k in range(tm // MOE_ROW_STEP + 1):
        @pl.when(n_steps == k)
        def _(k=k):
            r = k * MOE_ROW_STEP
            if r:
                o_ref[0:r, :] = fn(slice(0, r)).astype(o_ref.dtype)
            if r < tm:
                o_ref[r:tm, :] = jnp.zeros((tm - r, o_ref.shape[1]), o_ref.dtype)


def _moe_up_kernel(te_ref, tf_ref, tr_ref, x_ref, w1_ref, w3_ref, o_ref, w1b_ref, w3b_ref):
    t = pl.program_id(1)

    @pl.when(tf_ref[t] == 1)
    def _():
        w1b_ref[...] = w1_ref[...].astype(BF16)
        w3b_ref[...] = w3_ref[...].astype(BF16)

    _for_used_rows(tr_ref[t], MOE_TM, o_ref, lambda rs: _swiglu_halves(x_ref[rs, :], w1b_ref, w3b_ref))


def moe_up(xs, w1, w3, layer, tile_expert, tile_first, tile_rows, *, tn):
    P, K = xs.shape
    N = w1.shape[3]
    tm = MOE_TM
    nt = P // tm
    wspec = pl.BlockSpec((None, None, K, tn), lambda n, t, te, tf, tr: (layer, te[t], 0, n))
    return pl.pallas_call(
        _moe_up_kernel,
        grid_spec=pltpu.PrefetchScalarGridSpec(
            num_scalar_prefetch=3,
            grid=(N // tn, nt),
            in_specs=[pl.BlockSpec((tm, K), lambda n, t, te, tf, tr: (t, 0)), wspec, wspec],
            out_specs=pl.BlockSpec((tm, tn), lambda n, t, te, tf, tr: (t, n)),
            scratch_shapes=[pltpu.VMEM((K, tn), BF16), pltpu.VMEM((K, tn), BF16)],
        ),
        out_shape=jax.ShapeDtypeStruct((P, N), BF16),
        compiler_params=_cparams(2),
        name="moe_up",
    )(tile_expert, tile_first, tile_rows, xs, w1, w3)


def _moe_down_kernel(te_ref, tf_ref, tr_ref, h_ref, w_ref, o_ref, wb_ref, *, sub, tm):
    t = pl.program_id(1)
    tt = t // sub
    part = t % sub

    @pl.when((tf_ref[tt] == 1) & (part == 0))
    def _():
        wb_ref[...] = w_ref[...].astype(BF16)

    n_rows = jnp.clip(tr_ref[tt] - part * tm, 0, tm)
    _for_used_rows(n_rows, tm, o_ref,
                   lambda rs: jnp.dot(h_ref[rs, :], wb_ref[...], preferred_element_type=F32))


def moe_down(h, w2, layer, tile_expert, tile_first, tile_rows, *, tm, tn):
    P, K = h.shape
    N = w2.shape[3]
    assert MOE_TM % tm == 0 and tm % MOE_ROW_STEP == 0
    sub = MOE_TM // tm
    return pl.pallas_call(
        functools.partial(_moe_down_kernel, sub=sub, tm=tm),
        grid_spec=pltpu.PrefetchScalarGridSpec(
            num_scalar_prefetch=3,
            grid=(N // tn, P // tm),
            in_specs=[pl.BlockSpec((tm, K), lambda n, t, te, tf, tr: (t, 0)),
                      pl.BlockSpec((None, None, K, tn), lambda n, t, te, tf, tr: (layer, te[t // sub], 0, n))],
            out_specs=pl.BlockSpec((tm, tn), lambda n, t, te, tf, tr: (t, n)),
            scratch_shapes=[pltpu.VMEM((K, tn), BF16)],
        ),
        out_shape=jax.ShapeDtypeStruct((P, N), F32),
        compiler_params=_cparams(2),
        name="moe_down",
    )(tile_expert, tile_first, tile_rows, h, w2)


def _moe_combine_ln_kernel(pos_ref, gate_ref, x_ref, g_ref, b_ref, e_hbm, op_ref, os_ref, buf_ref, sem, *,
                           rows, n_full, tail):
    i = pl.program_id(0)
    slot = i % 2

    def n_rows(step):
        return jnp.where(step < n_full, rows, tail)

    def row_copy(s, kk, r, src_row):
        return pltpu.make_async_copy(e_hbm.at[src_row], buf_ref.at[s, kk, r], sem.at[s])

    def issue(step, s):
        def body(g, c):
            for u in range(DMA_UNROLL):
                r = g * DMA_UNROLL + u
                for kk in range(TOP_K):
                    row_copy(s, kk, r, pos_ref[kk, step * rows + r]).start()
            return c

        lax.fori_loop(0, n_rows(step) // DMA_UNROLL, body, 0)

    @pl.when(i == 0)
    def _():
        issue(0, 0)

    @pl.when(i < n_full)
    def _():
        issue(i + 1, 1 - slot)

    def drain(g, c):
        for u in range(DMA_UNROLL):
            for kk in range(TOP_K):
                row_copy(slot, kk, g * DMA_UNROLL + u, 0).wait()
        return c

    lax.fori_loop(0, n_rows(i) // DMA_UNROLL, drain, 0)

    def finish(nr, out_ref):
        g0 = gate_ref[0:nr, 0:1]
        g1 = gate_ref[0:nr, 1:2]
        y = jnp.concatenate([g0 * buf_ref[slot, 0, 0:nr, j, :] + g1 * buf_ref[slot, 1, 0:nr, j, :]
                             for j in range(buf_ref.shape[3])], axis=-1)
        out_ref[...] = _add_ln(x_ref[0:nr, :], y, g_ref[...], b_ref[...])

    @pl.when(i < n_full)
    def _():
        finish(rows, op_ref)

    @pl.when(i == n_full)
    def _():
        finish(tail, os_ref)


def moe_combine_ln(eout, pos, gate, x, g, b, MP, *, rows=256):
    M, D = x.shape
    P = eout.shape[0]
    DB = M - MP
    assert MP % rows == 0 and 0 < DB < rows and rows % DMA_UNROLL == 0 and DB % DMA_UNROLL == 0
    n_full = MP // rows
    e3 = eout.reshape(P, D // LANES, LANES)
    pos = jnp.pad(pos, ((0, 0), (0, (n_full + 1) * rows - M)))
    vec = pl.BlockSpec((1, D), lambda i, p: (0, 0))
    return pl.pallas_call(
        functools.partial(_moe_combine_ln_kernel, rows=rows, n_full=n_full, tail=DB),
        grid_spec=pltpu.PrefetchScalarGridSpec(
            num_scalar_prefetch=1,
            grid=(n_full + 1,),
            in_specs=[pl.BlockSpec((rows, TOP_K), lambda i, p: (i, 0)),
                      pl.BlockSpec((rows, D), lambda i, p: (i, 0)),
                      vec, vec,
                      pl.BlockSpec(memory_space=pl.ANY)],
            out_specs=[pl.BlockSpec((rows, D), lambda i, p: (jnp.minimum(i, n_full - 1), 0)),
                       pl.BlockSpec((DB, D), lambda i, p: (0, 0))],
            scratch_shapes=[pltpu.VMEM((2, TOP_K, rows, D // LANES, LANES), F32),
                            pltpu.SemaphoreType.DMA((2,))],
        ),
        out_shape=[jax.ShapeDtypeStruct((MP, D), F32), jax.ShapeDtypeStruct((DB, D), F32)],
        compiler_params=_cparams(1),
        name="moe_combine_ln",
    )(pos, gate, x, g.reshape(1, D), b.reshape(1, D), e3)


def _routing_tables(idx, tm):
    M = idx.shape[1]
    NP = TOP_K * M
    nb = pl.cdiv(NP, LANES)
    e_flat = jnp.pad(idx.reshape(-1), (0, nb * LANES - NP), constant_values=N_EXPERTS)
    oh = (e_flat[None, :] == jnp.arange(N_EXPERTS)[:, None]).astype(F32).reshape(N_EXPERTS, nb, LANES)
    tri = jnp.asarray(np.triu(np.ones((LANES, LANES), np.float32)))
    within = jnp.einsum("ebj,ji->ebi", oh, tri, precision=lax.Precision.HIGHEST)
    blk_tot = within[..., -1]
    blk_pre = jnp.cumsum(blk_tot, axis=1) - blk_tot
    counts = (blk_pre[:, -1] + blk_tot[:, -1]).astype(jnp.int32)
    tiles_per = (counts + tm - 1) // tm
    tile_start = jnp.cumsum(tiles_per) - tiles_per
    rem = counts - (tiles_per - 1) * tm
    rank = within + blk_pre[..., None] - 1.0
    remf = rem.astype(F32)[:, None, None]
    row_in_expert = jnp.where(rank < remf, rank, rank - remf + tm)
    pos = jnp.sum(oh * (row_in_expert + (tile_start * tm).astype(F32)[:, None, None]), axis=0)
    pos = pos.reshape(-1)[:NP].astype(jnp.int32).reshape(TOP_K, M)
    n_tiles = NP // tm + N_EXPERTS
    t_io = jnp.arange(n_tiles)
    used = jnp.sum(tiles_per)
    tile_expert = jnp.clip(jnp.sum(t_io[:, None] >= tile_start[None, :], axis=1) - 1, 0, N_EXPERTS - 1)
    tile_valid = (t_io < used).astype(jnp.int32)
    last_used_expert = jnp.max(jnp.where(tile_valid == 1, tile_expert, 0))
    tile_expert = jnp.where(tile_valid == 1, tile_expert, last_used_expert).astype(jnp.int32)
    prev = jnp.concatenate([jnp.array([-1], jnp.int32), tile_expert[:-1]])
    tile_first = (tile_expert != prev).astype(jnp.int32)
    own = jnp.arange(N_EXPERTS)[None, :] == tile_expert[:, None]
    of_tile = lambda per_expert: jnp.sum(jnp.where(own, per_expert[None, :], 0), axis=1)
    is_first_of_expert = t_io == of_tile(tile_start)
    tile_rows = jnp.where(tile_valid == 1, jnp.where(is_first_of_expert, of_tile(rem), tm), 0).astype(jnp.int32)
    return pos, tile_expert, tile_first, tile_rows, n_tiles * tm


def moe_ffn_ln(x, xb, router_w_l, w1, w3, w2, layer, g, b, MP):
    idx, gate = router(x, router_w_l)
    pos, tile_expert, tile_first, tile_rows, P = _routing_tables(idx, MOE_TM)
    xs = scatter_rows(xb, pos, P)
    h = moe_up(xs, w1, w3, layer, tile_expert, tile_first, tile_rows, tn=1024)
    eout = moe_down(h, w2, layer, tile_expert, tile_first, tile_rows, tm=512, tn=512)
    return moe_combine_ln(eout, pos, gate.T, x, g, b, MP)


def kernel(x_prompt, x_sample, cache_k, cache_v, page_table, state_hgrn, rel_bias, w_in, w_pa, w_pb, w_out,
           lam_qk, subln_g, hgrn_g, lb_raw, ln_g, ln_b, ffn_w1, ffn_w3, ffn_w2, router_w, moe_w1, moe_w3, moe_w2):
    B, S, D = x_prompt.shape
    DB = x_sample.shape[0]
    MP = B * S
    lbs = jax.nn.softmax(lb_raw.astype(F32), axis=0)
    lbs = jnp.cumsum(lbs, axis=0) - lbs[0]
    bias_tiles = _bias_tiles_t(rel_bias, ATT_T) * LOG2E
    ck = cache_k.reshape(cache_k.shape[0], cache_k.shape[1], PAGE_ROWS, 2 * DK_A)
    cv = cache_v.reshape(cache_v.shape[0], cache_v.shape[1], PAGE_ROWS, DV_A)

    xp = x_prompt.reshape(MP, D)
    xs = x_sample.reshape(DB, D)
    x = (xp, xs)
    xb = stack_rows_bf16(xp, xs)
    kp = jnp.zeros((DEPTH, MP, W_A), F32)
    vp = jnp.zeros((DEPTH, MP, W_A), F32)
    ks = jnp.zeros((DEPTH, DB, W_A), F32)
    vs = jnp.zeros((DEPTH, DB, W_A), F32)
    ss = jnp.zeros(state_hgrn.shape, state_hgrn.dtype)
    sp = []
    for l in range(DEPTH):
        lam_init = 0.8 - 0.6 * math.exp(-0.3 * l)
        lq = lam_qk[l].astype(F32)
        lam = jnp.exp(jnp.sum(lq[0] * lq[1])) - jnp.exp(jnp.sum(lq[2] * lq[3])) + lam_init

        z, kp, vp, ks, vs = in_proj(xb, w_in, l, kp, vp, ks, vs, tm=1024, tn=W_A)
        zs = z[MP:]
        ya = prompt_attention(z, kp, vp, l, bias_tiles, lam, subln_g[l], lam_init, B, S)
        yb, s_p = hgrn_prompt(z, lbs[l], hgrn_g[l], B, S)
        heads = lambda a: a.reshape(DB, H_A, DV_A)
        ya_s = sample_attention(heads(zs[:, COLZ["q_a"]:COLZ["q_a"] + W_A]), heads(ks[l]), heads(vs[l]),
                                ck, cv, page_table, rel_bias, lam, subln_g[l], lam_init, l)
        yb_s, ss = hgrn_step(z, MP, lbs[l], hgrn_g[l], state_hgrn, ss, l)
        mg = branch_merge(ya, yb, ya_s.reshape(DB, W_A), yb_s, z, w_pa, w_pb, l, tm=1024, tn=512)
        x, xb = proj_residual_ln(mg, w_out, l, x, ln_g[l, 0], ln_b[l, 0], MP)

        j = l // 2
        if l % 2 == 0:
            hh = swiglu_up(xb, ffn_w1, ffn_w3, j, tm=1024, tn=512)
            c = matmul(hh, ffn_w2, j, tm=512, tn=512)
            x, xb = residual_ln(x, c, ln_g[l, 1], ln_b[l, 1])
        else:
            assert l == DEPTH - 1
            y_prompt, y_sample = moe_ffn_ln(x, xb, router_w[j], moe_w1, moe_w3, moe_w2, j,
                                            ln_g[l, 1], ln_b[l, 1], MP)
        sp.append(s_p)
    return (y_prompt.reshape(B, S, D), y_sample.reshape(DB, 1, D),
            kp.reshape(DEPTH, B, S, H_A, 2 * DK_A), vp.reshape(DEPTH, B, S, H_A, DV_A), jnp.stack(sp),
            ks.reshape(DEPTH, DB, 1, H_A, 2 * DK_A), vs.reshape(DEPTH, DB, 1, H_A, DV_A), ss)
```
